```python
import math
import jax, jax.numpy as jnp
from jax import lax
import numpy as np

D_MODEL = 1024
BATCH = 8
SEQ = 2048
DEPTH = 1

LRU_WIDTH = D_MODEL
LRU_HEADS = 16
LRU_HEAD_DIM = LRU_WIDTH // LRU_HEADS
CONV_WIDTH = 4
LRU_C = 8.0
S5_WIDTH = D_MODEL // 2
S5_GROUP = 16
S5_GROUPS = S5_WIDTH // S5_GROUP
S5_STATE = 64
D_FF = 2816
EPS = 1e-6
N_MOD = 9
IN_COLS = 2 * LRU_WIDTH + S5_WIDTH + 2 * D_MODEL

kernel_name = 'hybrid_rglru_s5_macaron_adaln'


def rms_norm(x, g):
    xf = x.astype(jnp.float32)
    y = xf * lax.rsqrt(jnp.mean(xf * xf, axis=-1, keepdims=True) + EPS)
    return (y * g.astype(jnp.float32)).astype(x.dtype)


def modulate(n, shift, scale):
    return n * (1.0 + scale[:, None, :]) + shift[:, None, :]


def swiglu(u, w_up, w_down):
    a, b = jnp.split(u @ w_up, 2, axis=-1)
    return (jax.nn.silu(a) * b) @ w_down


def causal_depthwise_conv(x, w, b):
    s = x.shape[1]
    xp = jnp.pad(x, ((0, 0), (CONV_WIDTH - 1, 0), (0, 0)))
    y = b
    for k in range(CONV_WIDTH):
        y = y + xp[:, k:k + s, :] * w[k]
    return y


def rg_lru(x, w_r, b_r, w_i, b_i, lam):
    bsz, s, wd = x.shape
    xh = x.reshape(bsz, s, LRU_HEADS, LRU_HEAD_DIM)
    r = jax.nn.sigmoid(jnp.einsum('bshi,hij->bshj', xh, w_r).reshape(bsz, s, wd) + b_r)
    i = jax.nn.sigmoid(jnp.einsum('bshi,hij->bshj', xh, w_i).reshape(bsz, s, wd) + b_i)
    log_a = -LRU_C * r.astype(jnp.float32) * jax.nn.softplus(-lam.astype(jnp.float32))
    a = jnp.exp(log_a)
    mult = jnp.sqrt(-jnp.expm1(2.0 * log_a))
    u = mult * (i * x).astype(jnp.float32)

    def combine(left, right):
        a1, b1 = left
        a2, b2 = right
        return a1 * a2, a2 * b1 + b2

    _, h = lax.associative_scan(combine, (a, u), axis=1)
    return h.astype(x.dtype)


def s5_ssm(x, a_re, a_im, log_dt, b_re, b_im, c_re, c_im, d_skip):
    bsz, s, _ = x.shape
    xf = x.astype(jnp.float32)
    xg = xf.reshape(bsz, s, S5_GROUPS, S5_GROUP)
    a_re = a_re.astype(jnp.float32)
    a_im = a_im.astype(jnp.float32)
    dt = jnp.exp(log_dt.astype(jnp.float32))[:, None]
    mag = jnp.exp(a_re * dt)
    lr = mag * jnp.cos(a_im * dt)
    li = mag * jnp.sin(a_im * dt)
    den = a_re * a_re + a_im * a_im
    nr = lr - 1.0
    cr = (nr * a_re + li * a_im) / den
    ci = (li * a_re - nr * a_im) / den
    bx_re = jnp.einsum('bsgh,gph->bsgp', xg, b_re.astype(jnp.float32))
    bx_im = jnp.einsum('bsgh,gph->bsgp', xg, b_im.astype(jnp.float32))
    u_re = cr * bx_re - ci * bx_im
    u_im = cr * bx_im + ci * bx_re
    lam_re = jnp.broadcast_to(lr, (1, s, S5_GROUPS, S5_STATE))
    lam_im = jnp.broadcast_to(li, (1, s, S5_GROUPS, S5_STATE))

    def combine(left, right):
        ar1, ai1, br1, bi1 = left
        ar2, ai2, br2, bi2 = right
        return (ar2 * ar1 - ai2 * ai1,
                ar2 * ai1 + ai2 * ar1,
                ar2 * br1 - ai2 * bi1 + br2,
                ar2 * bi1 + ai2 * br1 + bi2)

    _, _, h_re, h_im = lax.associative_scan(combine, (lam_re, lam_im, u_re, u_im), axis=1)
    y = (jnp.einsum('bsgp,ghp->bsgh', h_re, c_re.astype(jnp.float32))
         - jnp.einsum('bsgp,ghp->bsgh', h_im, c_im.astype(jnp.float32)))
    y = y.reshape(bsz, s, S5_WIDTH) + d_skip.astype(jnp.float32) * xf
    return y.astype(x.dtype)


def setup_inputs(seed: int = 0) -> dict:
    key = jax.random.key(seed)
    ks = iter(jax.random.split(key, 40))
    L = DEPTH
    f32 = jnp.float32

    def nrm(shape, scale):
        return scale * jax.random.normal(next(ks), shape, f32)

    def gain(shape):
        return 1.0 + nrm(shape, 0.02)

    x = nrm((BATCH, SEQ, D_MODEL), 1.0)
    c = nrm((BATCH, D_MODEL), 1.0)
    mod_w = nrm((L, D_MODEL, N_MOD * D_MODEL), 0.3 * D_MODEL ** -0.5)
    mod_b = nrm((L, N_MOD * D_MODEL), 0.02)
    norm1_g = gain((L, D_MODEL))
    ffn1_w_up = nrm((L, D_MODEL, 2 * D_FF), D_MODEL ** -0.5)
    ffn1_w_down = nrm((L, D_FF, D_MODEL), D_FF ** -0.5)
    norm2_g = gain((L, D_MODEL))
    w_in = nrm((L, D_MODEL, IN_COLS), D_MODEL ** -0.5)
    b_in = nrm((L, IN_COLS), 0.02)
    conv_w = nrm((L, CONV_WIDTH, LRU_WIDTH), CONV_WIDTH ** -0.5)
    conv_b = nrm((L, LRU_WIDTH), 0.02)
    lru_w_r = nrm((L, LRU_HEADS, LRU_HEAD_DIM, LRU_HEAD_DIM), LRU_HEAD_DIM ** -0.5)
    lru_b_r = nrm((L, LRU_WIDTH), 0.02)
    lru_w_i = nrm((L, LRU_HEADS, LRU_HEAD_DIM, LRU_HEAD_DIM), LRU_HEAD_DIM ** -0.5)
    lru_b_i = nrm((L, LRU_WIDTH), 0.02)
    a0 = jax.random.uniform(next(ks), (L, LRU_WIDTH), f32, minval=0.9, maxval=0.999)
    base = a0 ** (1.0 / LRU_C)
    lru_lambda = jnp.log(base) - jnp.log1p(-base)
    proj_a = nrm((L, LRU_WIDTH, D_MODEL), LRU_WIDTH ** -0.5)
    s5_a_re = -0.5 + nrm((L, S5_GROUPS, S5_STATE), 0.01)
    s5_a_im = math.pi * jnp.arange(S5_STATE, dtype=f32)[None, None, :] + nrm((L, S5_GROUPS, S5_STATE), 0.01)
    s5_log_dt = jax.random.uniform(next(ks), (L, S5_GROUPS), f32, minval=math.log(1e-3), maxval=math.log(1e-1))
    s5_b_re = nrm((L, S5_GROUPS, S5_STATE, S5_GROUP), (2.0 * S5_GROUP) ** -0.5)
    s5_b_im = nrm((L, S5_GROUPS, S5_STATE, S5_GROUP), (2.0 * S5_GROUP) ** -0.5)
    s5_c_re = nrm((L, S5_GROUPS, S5_GROUP, S5_STATE), (2.0 * S5_STATE) ** -0.5)
    s5_c_im = nrm((L, S5_GROUPS, S5_GROUP, S5_STATE), (2.0 * S5_STATE) ** -0.5)
    s5_d = nrm((L, S5_WIDTH), 1.0)
    glu_w = nrm((L, S5_WIDTH, S5_WIDTH), S5_WIDTH ** -0.5)
    glu_b = nrm((L, S5_WIDTH), 0.02)
    proj_b = nrm((L, S5_WIDTH, D_MODEL), S5_WIDTH ** -0.5)
    w_out = nrm((L, D_MODEL, D_MODEL), D_MODEL ** -0.5)
    norm3_g = gain((L, D_MODEL))
    ffn2_w_up = nrm((L, D_MODEL, 2 * D_FF), D_MODEL ** -0.5)
    ffn2_w_down = nrm((L, D_FF, D_MODEL), D_FF ** -0.5)
    final_g = gain((D_MODEL,))
    return {'x': x, 'c': c, 'mod_w': mod_w, 'mod_b': mod_b,
            'norm1_g': norm1_g, 'ffn1_w_up': ffn1_w_up, 'ffn1_w_down': ffn1_w_down,
            'norm2_g': norm2_g, 'w_in': w_in, 'b_in': b_in,
            'conv_w': conv_w, 'conv_b': conv_b,
            'lru_w_r': lru_w_r, 'lru_b_r': lru_b_r, 'lru_w_i': lru_w_i, 'lru_b_i': lru_b_i,
            'lru_lambda': lru_lambda, 'proj_a': proj_a,
            's5_a_re': s5_a_re, 's5_a_im': s5_a_im, 's5_log_dt': s5_log_dt,
            's5_b_re': s5_b_re, 's5_b_im': s5_b_im, 's5_c_re': s5_c_re, 's5_c_im': s5_c_im,
            's5_d': s5_d, 'glu_w': glu_w, 'glu_b': glu_b, 'proj_b': proj_b,
            'w_out': w_out, 'norm3_g': norm3_g, 'ffn2_w_up': ffn2_w_up, 'ffn2_w_down': ffn2_w_down,
            'final_g': final_g}


def reference(x, c, mod_w, mod_b, norm1_g, ffn1_w_up, ffn1_w_down, norm2_g, w_in, b_in,
              conv_w, conv_b, lru_w_r, lru_b_r, lru_w_i, lru_b_i, lru_lambda, proj_a,
              s5_a_re, s5_a_im, s5_log_dt, s5_b_re, s5_b_im, s5_c_re, s5_c_im,
              s5_d, glu_w, glu_b, proj_b, w_out, norm3_g, ffn2_w_up, ffn2_w_down, final_g):
    c_act = jax.nn.silu(c)
    split_pts = [LRU_WIDTH, 2 * LRU_WIDTH, 2 * LRU_WIDTH + S5_WIDTH, 2 * LRU_WIDTH + S5_WIDTH + D_MODEL]
    for l in range(DEPTH):
        mod = c_act @ mod_w[l] + mod_b[l]
        sh1, sc1, g1, sh2, sc2, g2, sh3, sc3, g3 = jnp.split(mod, N_MOD, axis=-1)

        u = modulate(rms_norm(x, norm1_g[l]), sh1, sc1)
        x = x + 0.5 * g1[:, None, :] * swiglu(u, ffn1_w_up[l], ffn1_w_down[l])

        u = modulate(rms_norm(x, norm2_g[l]), sh2, sc2)
        z = u @ w_in[l] + b_in[l]
        xa, ga, xb, mg_a, mg_b = jnp.split(z, split_pts, axis=-1)
        xa = causal_depthwise_conv(xa, conv_w[l], conv_b[l])
        ya = rg_lru(xa, lru_w_r[l], lru_b_r[l], lru_w_i[l], lru_b_i[l], lru_lambda[l]) * jax.nn.gelu(ga)
        yb = jax.nn.gelu(s5_ssm(xb, s5_a_re[l], s5_a_im[l], s5_log_dt[l], s5_b_re[l], s5_b_im[l],
                                s5_c_re[l], s5_c_im[l], s5_d[l]))
        yb = yb * jax.nn.sigmoid(yb @ glu_w[l] + glu_b[l])
        m = jax.nn.sigmoid(mg_a) * (ya @ proj_a[l]) + jax.nn.sigmoid(mg_b) * (yb @ proj_b[l])
        x = x + g2[:, None, :] * (m @ w_out[l])

        u = modulate(rms_norm(x, norm3_g[l]), sh3, sc3)
        x = x + 0.5 * g3[:, None, :] * swiglu(u, ffn2_w_up[l], ffn2_w_down[l])
    return rms_norm(x, final_g)
```

```python
import functools
import math

import jax
import jax.numpy as jnp
from jax import lax
from jax.experimental import pallas as pl
from jax.experimental.pallas import tpu as pltpu

F32 = jnp.float32
BF16 = jnp.bfloat16

EPS = 1e-6
LRU_C = 8.0
CONV_WIDTH = 4
LRU_HEADS = 16
S5_GROUP = 16
S5_STATE = 64
N_MOD = 9

SUBLANES = 8
GATE_TILE = 256
S5_HALF_CH = 256
VMEM_LIMIT = 56 * 1024 * 1024

FFN_TM = 512
MIX_TS = 256
S5_LANE_CHUNK = 512


def _sigmoid(x):
    return 1.0 / (1.0 + jnp.exp(-x))


def _gelu_tanh(x):
    c = math.sqrt(2.0 / math.pi)
    return 0.5 * x * (1.0 + jnp.tanh(c * (x + 0.044715 * (x * x * x))))


def _rms_norm(x, g):
    ms = jnp.mean(x * x, axis=-1, keepdims=True)
    return x * lax.rsqrt(ms + EPS) * g


def _const_spec(shape):
    nd = len(shape)
    return pl.BlockSpec(shape, lambda *_: (0,) * nd, pipeline_mode=pl.Buffered(1))


def _mod_kernel(c_ref, w_ref, b_ref, o_ref):
    c = c_ref[...]
    ca = c * _sigmoid(c)
    o_ref[...] = jnp.dot(ca.astype(BF16), w_ref[...].astype(BF16),
                         preferred_element_type=F32) + b_ref[...]


def _mod(c, mod_w, mod_b):
    bsz, d = c.shape
    n = mod_w.shape[1]
    tn = d
    return pl.pallas_call(
        _mod_kernel,
        grid=(n // tn,),
        in_specs=[pl.BlockSpec((bsz, d), lambda j: (0, 0)),
                  pl.BlockSpec((d, tn), lambda j: (0, j)),
                  pl.BlockSpec((1, tn), lambda j: (0, j))],
        out_specs=pl.BlockSpec((bsz, tn), lambda j: (0, j)),
        out_shape=jax.ShapeDtypeStruct((bsz, n), F32),
        name="mod",
    )(c, mod_w, mod_b.reshape(1, n))


def _cmul(ar, ai, br, bi):
    return ar * br - ai * bi, ar * bi + ai * br


def _s5_discretise(a_re, a_im, log_dt):
    dt = jnp.exp(log_dt)
    mag = jnp.exp(a_re * dt)
    lr = mag * jnp.cos(a_im * dt)
    li = mag * jnp.sin(a_im * dt)
    den = a_re * a_re + a_im * a_im
    nr = lr - 1.0
    cr = (nr * a_re + li * a_im) / den
    ci = (li * a_re - nr * a_im) / den
    return lr, li, cr, ci


def _s5_prep_kernel(are_g, aim_g, ldt_g, are_r, aim_r, ldt_r, bre_t, bim_t, cre_t, cim_t,
                    lc_ref, bd_ref, cd_ref):
    lr, li, _, _ = _s5_discretise(are_g[...], aim_g[...], ldt_g[...])
    pw = [(lr, li)]
    for _ in range(1, SUBLANES):
        pw.append(_cmul(*pw[-1], lr, li))
    zero = jnp.zeros_like(lr)
    for t in range(SUBLANES):
        lc_ref[0, t] = pw[t][0]
        lc_ref[1, t] = pw[t][1]
        for k, d in enumerate((1, 2, 4)):
            lc_ref[2 + 2 * k, t] = pw[d - 1][0] if t >= d else zero
            lc_ref[3 + 2 * k, t] = pw[d - 1][1] if t >= d else zero

    _, _, cr, ci = _s5_discretise(are_r[...], aim_r[...], ldt_r[...])
    bb_re, bb_im = _cmul(cr, ci, bre_t[...], bim_t[...])
    half_states = (S5_HALF_CH // S5_GROUP) * S5_STATE
    erow = lax.broadcasted_iota(jnp.int32, (S5_STATE, half_states), 0)
    ecol = lax.broadcasted_iota(jnp.int32, (S5_STATE, half_states), 1)
    spread = jnp.where((ecol % S5_STATE) == erow, 1.0, 0.0).astype(BF16)
    brow_g = lax.broadcasted_iota(jnp.int32, (S5_HALF_CH, half_states), 0) // S5_GROUP
    bcol_g = lax.broadcasted_iota(jnp.int32, (S5_HALF_CH, half_states), 1) // S5_STATE
    bmask = brow_g == bcol_g
    crow_g = lax.broadcasted_iota(jnp.int32, (half_states, S5_HALF_CH), 0) // S5_STATE
    ccol_g = lax.broadcasted_iota(jnp.int32, (half_states, S5_HALF_CH), 1) // S5_GROUP
    cmask = crow_g == ccol_g
    reps = half_states // S5_STATE
    for h in range(bd_ref.shape[0]):
        rows = slice(h * S5_HALF_CH, (h + 1) * S5_HALF_CH)
        for k, bb in enumerate((bb_re, bb_im)):
            tiled = jnp.dot(bb[rows, :].astype(BF16), spread, preferred_element_type=F32)
            bd_ref[h, :, k * half_states:(k + 1) * half_states] = jnp.where(bmask, tiled, 0.0).astype(BF16)
        for k, (ct, sign) in enumerate(((cre_t, 1.0), (cim_t, -1.0))):
            tiled = jnp.concatenate([ct[:, rows]] * reps, axis=0)
            cd_ref[h, k * half_states:(k + 1) * half_states, :] = jnp.where(cmask, sign * tiled, 0.0).astype(BF16)


def _s5_prep(a_re, a_im, log_dt, b_re, b_im, c_re, c_im):
    g, p = a_re.shape
    hch = b_re.shape[2]
    width = g * hch
    halves = width // S5_HALF_CH
    half_states = (S5_HALF_CH // hch) * p
    rep = lambda a: jnp.repeat(a, hch, axis=0)
    ldt = log_dt.reshape(g, 1)
    bt = lambda b: jnp.transpose(b, (0, 2, 1)).reshape(width, p)
    ct = lambda c: jnp.transpose(c, (2, 0, 1)).reshape(p, width)
    lc, bd, cd = pl.pallas_call(
        _s5_prep_kernel,
        out_shape=(jax.ShapeDtypeStruct((8, SUBLANES, g, p), F32),
                   jax.ShapeDtypeStruct((halves, S5_HALF_CH, 2 * half_states), BF16),
                   jax.ShapeDtypeStruct((halves, 2 * half_states, S5_HALF_CH), BF16)),
        name="s5_prep",
    )(a_re, a_im, ldt, rep(a_re), rep(a_im), rep(ldt), bt(b_re), bt(b_im), ct(c_re), ct(c_im))
    return lc.reshape(8, SUBLANES, g * p), bd, cd


def _gate_prep_kernel(wr_ref, wi_ref, or_ref, oi_ref):
    hd = wr_ref.shape[1]
    per_tile = GATE_TILE // hd
    erow = lax.broadcasted_iota(jnp.int32, (hd, GATE_TILE), 0)
    ecol = lax.broadcasted_iota(jnp.int32, (hd, GATE_TILE), 1)
    for w_ref, o_ref in ((wr_ref, or_ref), (wi_ref, oi_ref)):
        for a in range(o_ref.shape[0]):
            for b in range(per_tile):
                place = jnp.where(ecol == erow + b * hd, 1.0, 0.0).astype(BF16)
                o_ref[a, b * hd:(b + 1) * hd, :] = jnp.dot(
                    w_ref[a * per_tile + b].astype(BF16), place, preferred_element_type=F32).astype(BF16)


def _gate_prep(w_r, w_i):
    heads, hd, _ = w_r.shape
    tiles = heads * hd // GATE_TILE
    shp = jax.ShapeDtypeStruct((tiles, GATE_TILE, GATE_TILE), BF16)
    return pl.pallas_call(_gate_prep_kernel, out_shape=(shp, shp), name="gate_prep")(w_r, w_i)


def _ffn_kernel(x_ref, mod_ref, g_ref, wup_ref, wdn_ref, fg_ref, o_ref, *, mod_base, final_norm):
    d = x_ref.shape[1]
    f = wdn_ref.shape[0]
    x = x_ref[...]
    sh = mod_ref[:, (mod_base + 0) * d:(mod_base + 1) * d]
    sc = mod_ref[:, (mod_base + 1) * d:(mod_base + 2) * d]
    gt = mod_ref[:, (mod_base + 2) * d:(mod_base + 3) * d]
    u = _rms_norm(x, g_ref[...]) * (1.0 + sc) + sh
    h = jnp.dot(u.astype(BF16), wup_ref[...], preferred_element_type=F32)
    a = h[:, :f]
    b = h[:, f:]
    act = (a * _sigmoid(a)) * b
    y = jnp.dot(act.astype(BF16), wdn_ref[...], preferred_element_type=F32)
    out = x + (0.5 * gt) * y
    if final_norm:
        out = _rms_norm(out, fg_ref[...])
    o_ref[...] = out


def _ffn(x2d, mod3, norm_g, w_up, w_down, final_g, *, seq, mod_base, final_norm):
    t, d = x2d.shape
    f = w_down.shape[0]
    tm = FFN_TM
    tiles_per_seq = seq // tm
    kern = functools.partial(_ffn_kernel, mod_base=mod_base, final_norm=final_norm)
    return pl.pallas_call(
        kern,
        grid=(t // tm,),
        in_specs=[pl.BlockSpec((tm, d), lambda i: (i, 0)),
                  pl.BlockSpec((None, 1, mod3.shape[2]), lambda i: (i // tiles_per_seq, 0, 0)),
                  _const_spec((1, d)),
                  _const_spec((d, 2 * f)),
                  _const_spec((f, d)),
                  _const_spec((1, d))],
        out_specs=pl.BlockSpec((tm, d), lambda i: (i, 0)),
        out_shape=jax.ShapeDtypeStruct((t, d), F32),
        compiler_params=pltpu.CompilerParams(dimension_semantics=("arbitrary",),
                                             vmem_limit_bytes=VMEM_LIMIT),
        name="ffn_final" if final_norm else "ffn",
    )(x2d, mod3, norm_g.reshape(1, d), w_up, w_down, final_g.reshape(1, d))


def _lru_scan(a_ref, u_ref, carry_ref):
    ts, d = u_ref.shape
    row = lax.broadcasted_iota(jnp.int32, (SUBLANES, d), 0)
    masks = [(s, row >= s) for s in (1, 2, 4)]

    def body(r, h):
        sl = pl.ds(pl.multiple_of(r * SUBLANES, SUBLANES), SUBLANES)
        a = a_ref[sl, :]
        u = u_ref[sl, :]
        for s, m in masks:
            ush = jnp.where(m, pltpu.roll(u, s, 0), 0.0)
            ash = jnp.where(m, pltpu.roll(a, s, 0), 1.0)
            u = a * ush + u
            a = a * ash
        hn = a * h + u
        u_ref[sl, :] = hn
        return jnp.broadcast_to(hn[SUBLANES - 1:SUBLANES, :], (SUBLANES, d))

    carry_ref[...] = lax.fori_loop(0, ts // SUBLANES, body, carry_ref[...])


def _s5_scan(u_ref, lc_ref, carry_ref, halves):
    ts = u_ref.shape[0]
    half_states = lc_ref.shape[2] // halves
    w = S5_LANE_CHUNK
    for h in range(halves):
        for j in range(half_states // w):
            cre = slice(2 * half_states * h + j * w, 2 * half_states * h + (j + 1) * w)
            cim = slice(2 * half_states * h + half_states + j * w, 2 * half_states * h + half_states + (j + 1) * w)
            lcs = slice(half_states * h + j * w, half_states * h + (j + 1) * w)

            def body(r, carry, cre=cre, cim=cim, lcs=lcs):
                hr, hi = carry
                sl = pl.ds(pl.multiple_of(r * SUBLANES, SUBLANES), SUBLANES)
                ur = u_ref[sl, cre]
                ui = u_ref[sl, cim]
                for k, s in enumerate((1, 2, 4)):
                    mr = lc_ref[2 + 2 * k, :, lcs]
                    mi = lc_ref[3 + 2 * k, :, lcs]
                    sr = pltpu.roll(ur, s, 0)
                    si = pltpu.roll(ui, s, 0)
                    ur, ui = ur + (mr * sr - mi * si), ui + (mr * si + mi * sr)
                pr = lc_ref[0, :, lcs]
                pi = lc_ref[1, :, lcs]
                nr = ur + (pr * hr - pi * hi)
                ni = ui + (pr * hi + pi * hr)
                u_ref[sl, cre] = nr
                u_ref[sl, cim] = ni
                return (jnp.broadcast_to(nr[SUBLANES - 1:SUBLANES, :], (SUBLANES, w)),
                        jnp.broadcast_to(ni[SUBLANES - 1:SUBLANES, :], (SUBLANES, w)))

            hr, hi = lax.fori_loop(0, ts // SUBLANES, body, (carry_ref[:, cre], carry_ref[:, cim]))
            carry_ref[:, cre] = hr
            carry_ref[:, cim] = hi


def _mixer_kernel(x_ref, mod_ref, g_ref, win_ref, bin_ref, cw_ref, cb_ref,
                  wr_ref, br_ref, wi_ref, bi_ref, lam_ref, pa_ref,
                  bd_ref, cd_ref, lc_ref, sd_ref, gw_ref, gb_ref, pb_ref, wo_ref,
                  o_ref,
                  xpad_ref, a_ref, u_ref, s5_ref, hl_ref, hs_ref):
    ts, d = x_ref.shape
    w5 = sd_ref.shape[1]
    halves = bd_ref.shape[0]
    hs2 = bd_ref.shape[2]
    pad = SUBLANES

    @pl.when(pl.program_id(1) == 0)
    def _():
        xpad_ref[0:pad, :] = jnp.zeros((pad, d), F32)
        hl_ref[...] = jnp.zeros_like(hl_ref)
        hs_ref[...] = jnp.zeros_like(hs_ref)

    x = x_ref[...]
    sh = mod_ref[:, 3 * d:4 * d]
    sc = mod_ref[:, 4 * d:5 * d]
    gt = mod_ref[:, 5 * d:6 * d]
    u_bf = (_rms_norm(x, g_ref[...]) * (1.0 + sc) + sh).astype(BF16)

    def in_proj(lo, hi):
        return jnp.dot(u_bf, win_ref[:, lo:hi], preferred_element_type=F32) + bin_ref[:, lo:hi]

    xpad_ref[pad:pad + ts, :] = in_proj(0, d)
    xc = cb_ref[...]
    for k in range(CONV_WIDTH):
        off = pad - (CONV_WIDTH - 1) + k
        xc = xc + xpad_ref[off:off + ts, :] * cw_ref[k:k + 1, :]
    xpad_ref[0:pad, :] = xpad_ref[ts:ts + pad, :]
    xc_bf = xc.astype(BF16)
    gates = []
    for w_ref, b_ref in ((wr_ref, br_ref), (wi_ref, bi_ref)):
        cols = [jnp.dot(xc_bf[:, j * GATE_TILE:(j + 1) * GATE_TILE], w_ref[j], preferred_element_type=F32)
                for j in range(w_ref.shape[0])]
        gates.append(_sigmoid(jnp.concatenate(cols, axis=1) + b_ref[...]))
    r_gate, i_gate = gates
    nl = -lam_ref[...]
    softplus = jnp.maximum(nl, 0.0) + jnp.log1p(jnp.exp(-jnp.abs(nl)))
    log_a = (-LRU_C * softplus) * r_gate
    a_gate = jnp.exp(log_a)
    a_ref[...] = a_gate
    u_ref[...] = jnp.sqrt(1.0 - a_gate * a_gate) * (i_gate * xc)
    _lru_scan(a_ref, u_ref, hl_ref)
    ya = u_ref[...] * _gelu_tanh(in_proj(d, 2 * d))
    proj_a = jnp.dot(ya.astype(BF16), pa_ref[...], preferred_element_type=F32)

    xb = in_proj(2 * d, 2 * d + w5)
    xb_bf = xb.astype(BF16)
    for h in range(halves):
        s5_ref[:, h * hs2:(h + 1) * hs2] = jnp.dot(
            xb_bf[:, h * S5_HALF_CH:(h + 1) * S5_HALF_CH], bd_ref[h], preferred_element_type=F32)
    _s5_scan(s5_ref, lc_ref, hs_ref, halves)
    ys = [jnp.dot(s5_ref[:, h * hs2:(h + 1) * hs2].astype(BF16), cd_ref[h], preferred_element_type=F32)
          for h in range(halves)]
    yb = _gelu_tanh(jnp.concatenate(ys, axis=1) + sd_ref[...] * xb)
    yb = yb * _sigmoid(jnp.dot(yb.astype(BF16), gw_ref[...], preferred_element_type=F32) + gb_ref[...])
    proj_b = jnp.dot(yb.astype(BF16), pb_ref[...], preferred_element_type=F32)

    m = (_sigmoid(in_proj(2 * d + w5, 3 * d + w5)) * proj_a
         + _sigmoid(in_proj(3 * d + w5, 4 * d + w5)) * proj_b)
    o_ref[...] = x + gt * jnp.dot(m.astype(BF16), wo_ref[...], preferred_element_type=F32)


def _mixer(x3d, mod3, norm_g, w_in, b_in, conv_w, conv_b, wr_bd, b_r, wi_bd, b_i, lam, proj_a,
           bd, cd, lc, s5_d, glu_w, glu_b, proj_b, w_out):
    bsz, seq, d = x3d.shape
    ts = MIX_TS
    w5 = s5_d.shape[0]
    n_states2 = bd.shape[0] * bd.shape[2]
    row = lambda v: v.reshape(1, -1)
    consts = [row(norm_g), w_in, row(b_in), conv_w, row(conv_b), wr_bd, row(b_r), wi_bd, row(b_i),
              row(lam), proj_a, bd, cd, lc, row(s5_d), glu_w, row(glu_b), proj_b, w_out]
    return pl.pallas_call(
        _mixer_kernel,
        grid=(bsz, seq // ts),
        in_specs=[pl.BlockSpec((None, ts, d), lambda b, s: (b, s, 0)),
                  pl.BlockSpec((None, 1, mod3.shape[2]), lambda b, s: (b, 0, 0))]
                 + [_const_spec(c.shape) for c in consts],
        out_specs=pl.BlockSpec((None, ts, d), lambda b, s: (b, s, 0)),
        out_shape=jax.ShapeDtypeStruct((bsz, seq, d), F32),
        scratch_shapes=[pltpu.VMEM((ts + SUBLANES, d), F32),
                        pltpu.VMEM((ts, d), F32),
                        pltpu.VMEM((ts, d), F32),
                        pltpu.VMEM((ts, n_states2), F32),
                        pltpu.VMEM((SUBLANES, d), F32),
                        pltpu.VMEM((SUBLANES, n_states2), F32)],
        compiler_params=pltpu.CompilerParams(dimension_semantics=("arbitrary", "arbitrary"),
                                             vmem_limit_bytes=VMEM_LIMIT),
        name="mixer",
    )(x3d, mod3, *consts)


def kernel(x, c, mod_w, mod_b, norm1_g, ffn1_w_up, ffn1_w_down, norm2_g, w_in, b_in, conv_w, conv_b, lru_w_r, lru_b_r, lru_w_i, lru_b_i, lru_lambda, proj_a, s5_a_re, s5_a_im, s5_log_dt, s5_b_re, s5_b_im, s5_c_re, s5_c_im, s5_d, glu_w, glu_b, proj_b, w_out, norm3_g, ffn2_w_up, ffn2_w_down, final_g):
    bsz, seq, d = x.shape
    depth = mod_w.shape[0]
    bf = lambda w: w.astype(BF16)
    c_act_in = c
    x2d = x.reshape(bsz * seq, d)
    for l in range(depth):
        mod3 = _mod(c_act_in, mod_w[l], mod_b[l]).reshape(bsz, 1, N_MOD * d)
        lc, bd, cd = _s5_prep(s5_a_re[l], s5_a_im[l], s5_log_dt[l], s5_b_re[l], s5_b_im[l],
                              s5_c_re[l], s5_c_im[l])
        wr_bd, wi_bd = _gate_prep(lru_w_r[l], lru_w_i[l])
        last = l == depth - 1
        x2d = _ffn(x2d, mod3, norm1_g[l], bf(ffn1_w_up[l]), bf(ffn1_w_down[l]), final_g,
                   seq=seq, mod_base=0, final_norm=False)
        x3d = _mixer(x2d.reshape(bsz, seq, d), mod3, norm2_g[l], bf(w_in[l]), b_in[l], conv_w[l], conv_b[l],
                     wr_bd, lru_b_r[l], wi_bd, lru_b_i[l], lru_lambda[l], bf(proj_a[l]),
                     bd, cd, lc, s5_d[l], bf(glu_w[l]), glu_b[l], bf(proj_b[l]), bf(w_out[l]))
        x2d = _ffn(x3d.reshape(bsz * seq, d), mod3, norm3_g[l], bf(ffn2_w_up[l]), bf(ffn2_w_down[l]), final_g,
                   seq=seq, mod_base=6, final_norm=last)
    return x2d.reshape(bsz, seq, d)
```

```python
import functools
import math

import jax
import jax.numpy as jnp
from jax import lax
from jax.experimental import pallas as pl
from jax.experimental.pallas import tpu as pltpu

F32 = jnp.float32
BF16 = jnp.bfloat16

EPS = 1e-6
LRU_C = 8.0
CONV_WIDTH = 4
S5_GROUP = 16
S5_STATE = 64
N_MOD = 9

SUBLANES = 8
LANES = 128
GATE_TILE = 256
S5_HALF_CH = 256
VMEM_LIMIT = 56 * 1024 * 1024

FFN_TM = 512
MIX_TT = 32
S5_LANE_CHUNK = 512


def _sigmoid(x):
    return 1.0 / (1.0 + jnp.exp(-x))


def _gelu_tanh(x):
    c = math.sqrt(2.0 / math.pi)
    return 0.5 * x * (1.0 + jnp.tanh(c * (x + 0.044715 * (x * x * x))))


def _rms_norm(x, g):
    ms = jnp.mean(x * x, axis=-1, keepdims=True)
    return x * lax.rsqrt(ms + EPS) * g


def _const_spec(shape):
    nd = len(shape)
    return pl.BlockSpec(shape, lambda *_: (0,) * nd, pipeline_mode=pl.Buffered(1))


def _mod_kernel(c_ref, w_ref, b_ref, o_ref):
    c = c_ref[...]
    ca = c * _sigmoid(c)
    o_ref[...] = jnp.dot(ca.astype(BF16), w_ref[...].astype(BF16),
                         preferred_element_type=F32) + b_ref[...]


def _mod(c, mod_w, mod_b):
    bsz, d = c.shape
    n = mod_w.shape[1]
    tn = d
    return pl.pallas_call(
        _mod_kernel,
        grid=(n // tn,),
        in_specs=[pl.BlockSpec((bsz, d), lambda j: (0, 0)),
                  pl.BlockSpec((d, tn), lambda j: (0, j)),
                  pl.BlockSpec((1, tn), lambda j: (0, j))],
        out_specs=pl.BlockSpec((bsz, tn), lambda j: (0, j)),
        out_shape=jax.ShapeDtypeStruct((bsz, n), F32),
        name="mod",
    )(c, mod_w, mod_b.reshape(1, n))


def _cmul(ar, ai, br, bi):
    return ar * br - ai * bi, ar * bi + ai * br


def _s5_discretise(a_re, a_im, log_dt):
    dt = jnp.exp(log_dt)
    mag = jnp.exp(a_re * dt)
    lr = mag * jnp.cos(a_im * dt)
    li = mag * jnp.sin(a_im * dt)
    den = a_re * a_re + a_im * a_im
    nr = lr - 1.0
    cr = (nr * a_re + li * a_im) / den
    ci = (li * a_re - nr * a_im) / den
    return lr, li, cr, ci


def _s5_prep_kernel(are_g, aim_g, ldt_g, are_r, aim_r, ldt_r, bre_t, bim_t, cre_t, cim_t,
                    lam_ref, bd_ref, cd_ref):
    lr, li, _, _ = _s5_discretise(are_g[...], aim_g[...], ldt_g[...])
    lam_ref[0] = lr
    lam_ref[1] = li

    _, _, cr, ci = _s5_discretise(are_r[...], aim_r[...], ldt_r[...])
    bb_re, bb_im = _cmul(cr, ci, bre_t[...], bim_t[...])
    half_states = (S5_HALF_CH // S5_GROUP) * S5_STATE
    erow = lax.broadcasted_iota(jnp.int32, (S5_STATE, half_states), 0)
    ecol = lax.broadcasted_iota(jnp.int32, (S5_STATE, half_states), 1)
    spread = jnp.where((ecol % S5_STATE) == erow, 1.0, 0.0).astype(BF16)
    brow_g = lax.broadcasted_iota(jnp.int32, (S5_HALF_CH, half_states), 0) // S5_GROUP
    bcol_g = lax.broadcasted_iota(jnp.int32, (S5_HALF_CH, half_states), 1) // S5_STATE
    bmask = brow_g == bcol_g
    crow_g = lax.broadcasted_iota(jnp.int32, (half_states, S5_HALF_CH), 0) // S5_STATE
    ccol_g = lax.broadcasted_iota(jnp.int32, (half_states, S5_HALF_CH), 1) // S5_GROUP
    cmask = crow_g == ccol_g
    reps = half_states // S5_STATE
    for h in range(bd_ref.shape[0]):
        rows = slice(h * S5_HALF_CH, (h + 1) * S5_HALF_CH)
        for k, bb in enumerate((bb_re, bb_im)):
            tiled = jnp.dot(bb[rows, :].astype(BF16), spread, preferred_element_type=F32)
            bd_ref[h, :, k * half_states:(k + 1) * half_states] = jnp.where(bmask, tiled, 0.0).astype(BF16)
        for k, (ct, sign) in enumerate(((cre_t, 1.0), (cim_t, -1.0))):
            tiled = jnp.concatenate([ct[:, rows]] * reps, axis=0)
            cd_ref[h, k * half_states:(k + 1) * half_states, :] = jnp.where(cmask, sign * tiled, 0.0).astype(BF16)


def _s5_prep(a_re, a_im, log_dt, b_re, b_im, c_re, c_im):
    g, p = a_re.shape
    hch = b_re.shape[2]
    assert hch == S5_GROUP and p == S5_STATE
    width = g * hch
    halves = width // S5_HALF_CH
    half_states = (S5_HALF_CH // hch) * p
    rep = lambda a: jnp.repeat(a, hch, axis=0)
    ldt = log_dt.reshape(g, 1)
    bt = lambda b: jnp.transpose(b, (0, 2, 1)).reshape(width, p)
    ct = lambda c: jnp.transpose(c, (2, 0, 1)).reshape(p, width)
    lam, bd, cd = pl.pallas_call(
        _s5_prep_kernel,
        out_shape=(jax.ShapeDtypeStruct((2, g, p), F32),
                   jax.ShapeDtypeStruct((halves, S5_HALF_CH, 2 * half_states), BF16),
                   jax.ShapeDtypeStruct((halves, 2 * half_states, S5_HALF_CH), BF16)),
        name="s5_prep",
    )(a_re, a_im, ldt, rep(a_re), rep(a_im), rep(ldt), bt(b_re), bt(b_im), ct(c_re), ct(c_im))
    return lam.reshape(2, g * p), bd, cd


def _gate_prep_kernel(wr_ref, wi_ref, or_ref, oi_ref):
    hd = wr_ref.shape[1]
    per_tile = GATE_TILE // hd
    erow = lax.broadcasted_iota(jnp.int32, (hd, GATE_TILE), 0)
    ecol = lax.broadcasted_iota(jnp.int32, (hd, GATE_TILE), 1)
    for w_ref, o_ref in ((wr_ref, or_ref), (wi_ref, oi_ref)):
        for a in range(o_ref.shape[0]):
            for b in range(per_tile):
                place = jnp.where(ecol == erow + b * hd, 1.0, 0.0).astype(BF16)
                o_ref[a, b * hd:(b + 1) * hd, :] = jnp.dot(
                    w_ref[a * per_tile + b].astype(BF16), place, preferred_element_type=F32).astype(BF16)


def _gate_prep(w_r, w_i):
    heads, hd, _ = w_r.shape
    tiles = heads * hd // GATE_TILE
    shp = jax.ShapeDtypeStruct((tiles, GATE_TILE, GATE_TILE), BF16)
    return pl.pallas_call(_gate_prep_kernel, out_shape=(shp, shp), name="gate_prep")(w_r, w_i)


def _ffn_kernel(x_ref, mod_ref, g_ref, wup_ref, wdn_ref, fg_ref, o_ref, *, mod_base, final_norm):
    d = x_ref.shape[1]
    f = wdn_ref.shape[0]
    x = x_ref[...]
    sh = mod_ref[:, (mod_base + 0) * d:(mod_base + 1) * d]
    sc = mod_ref[:, (mod_base + 1) * d:(mod_base + 2) * d]
    gt = mod_ref[:, (mod_base + 2) * d:(mod_base + 3) * d]
    u = _rms_norm(x, g_ref[...]) * (1.0 + sc) + sh
    h = jnp.dot(u.astype(BF16), wup_ref[...], preferred_element_type=F32)
    a = h[:, :f]
    b = h[:, f:]
    act = (a * _sigmoid(a)) * b
    y = jnp.dot(act.astype(BF16), wdn_ref[...], preferred_element_type=F32)
    out = x + (0.5 * gt) * y
    if final_norm:
        out = _rms_norm(out, fg_ref[...])
    o_ref[...] = out


def _ffn(x2d, mod3, norm_g, w_up, w_down, final_g, *, seq, mod_base, final_norm):
    t, d = x2d.shape
    f = w_down.shape[0]
    tm = FFN_TM
    tiles_per_seq = seq // tm
    kern = functools.partial(_ffn_kernel, mod_base=mod_base, final_norm=final_norm)
    return pl.pallas_call(
        kern,
        grid=(t // tm,),
        in_specs=[pl.BlockSpec((tm, d), lambda i: (i, 0)),
                  pl.BlockSpec((None, 1, mod3.shape[2]), lambda i: (i // tiles_per_seq, 0, 0)),
                  _const_spec((1, d)),
                  _const_spec((d, 2 * f)),
                  _const_spec((f, d)),
                  _const_spec((1, d))],
        out_specs=pl.BlockSpec((tm, d), lambda i: (i, 0)),
        out_shape=jax.ShapeDtypeStruct((t, d), F32),
        compiler_params=pltpu.CompilerParams(dimension_semantics=("arbitrary",),
                                             vmem_limit_bytes=VMEM_LIMIT),
        name="ffn_final" if final_norm else "ffn",
    )(x2d, mod3, norm_g.reshape(1, d), w_up, w_down, final_g.reshape(1, d))


def _lru_scan(a_ref, u_ref, carry_ref):
    ts = u_ref.shape[0]
    h = carry_ref[...]
    for t in range(ts // SUBLANES):
        sl = slice(t * SUBLANES, (t + 1) * SUBLANES)
        h = a_ref[sl, :] * h + u_ref[sl, :]
        u_ref[sl, :] = h
    carry_ref[...] = h


def _s5_scan(u_ref, lam_ref, carry_ref, halves):
    ts = u_ref.shape[0]
    half_states = lam_ref.shape[1] // halves
    w = S5_LANE_CHUNK
    for h in range(halves):
        for j in range(half_states // w):
            cre = slice(2 * half_states * h + j * w, 2 * half_states * h + (j + 1) * w)
            cim = slice(2 * half_states * h + half_states + j * w, 2 * half_states * h + half_states + (j + 1) * w)
            lcs = slice(half_states * h + j * w, half_states * h + (j + 1) * w)
            lr = jnp.broadcast_to(lam_ref[0:1, lcs], (SUBLANES, w))
            li = jnp.broadcast_to(lam_ref[1:2, lcs], (SUBLANES, w))
            hr = carry_ref[:, cre]
            hi = carry_ref[:, cim]
            for t in range(ts // SUBLANES):
                sl = slice(t * SUBLANES, (t + 1) * SUBLANES)
                hr, hi = (u_ref[sl, cre] + (lr * hr - li * hi),
                          u_ref[sl, cim] + (lr * hi + li * hr))
                u_ref[sl, cre] = hr
                u_ref[sl, cim] = hi
            carry_ref[:, cre] = hr
            carry_ref[:, cim] = hi


def _mixer_kernel(x_ref, mod_ref, g_ref, win_ref, bin_ref, cw_ref, cb_ref,
                  wr_ref, br_ref, wi_ref, bi_ref, lamp_ref, pa_ref,
                  bd_ref, cd_ref, lam_ref, sd_ref, gw_ref, gb_ref, pb_ref, wo_ref,
                  o_ref,
                  tm_ref, xpad_ref, a_ref, u_ref, s5_ref, hl_ref, hs_ref):
    nb, tt, d = x_ref.shape
    ts = nb * tt
    lane_chunks = tm_ref.shape[0]
    lanes = tm_ref.shape[2]
    w5 = sd_ref.shape[1]
    halves = bd_ref.shape[0]
    hs2 = bd_ref.shape[2]
    pad = (CONV_WIDTH - 1) * nb

    @pl.when(pl.program_id(0) == 0)
    def _():
        xpad_ref[0:pad, :] = jnp.zeros((pad, d), F32)
        hl_ref[...] = jnp.zeros_like(hl_ref)
        hs_ref[...] = jnp.zeros_like(hs_ref)

    def per_batch(v):
        return v[None, :, :]

    for b in range(nb):
        for c in range(lane_chunks):
            tm_ref[c, pl.ds(b, tt, stride=nb), :] = x_ref[b, :, c * lanes:(c + 1) * lanes]
    x = jnp.concatenate([tm_ref[c] for c in range(lane_chunks)], axis=1)
    x3 = x.reshape(tt, nb, d)
    sh = per_batch(mod_ref[:, 3 * d:4 * d])
    sc = per_batch(mod_ref[:, 4 * d:5 * d])
    gt = per_batch(mod_ref[:, 5 * d:6 * d])
    u_bf = (_rms_norm(x3, g_ref[...]) * (1.0 + sc) + sh).reshape(ts, d).astype(BF16)

    def in_proj(lo, hi):
        return jnp.dot(u_bf, win_ref[:, lo:hi], preferred_element_type=F32) + bin_ref[:, lo:hi]

    xpad_ref[pad:pad + ts, :] = in_proj(0, d)
    xc = cb_ref[...]
    for k in range(CONV_WIDTH):
        xc = xc + xpad_ref[k * nb:k * nb + ts, :] * cw_ref[k:k + 1, :]
    xpad_ref[0:pad, :] = xpad_ref[ts:ts + pad, :]
    xc_bf = xc.astype(BF16)
    gates = []
    for w_ref, b_ref in ((wr_ref, br_ref), (wi_ref, bi_ref)):
        cols = [jnp.dot(xc_bf[:, j * GATE_TILE:(j + 1) * GATE_TILE], w_ref[j], preferred_element_type=F32)
                for j in range(w_ref.shape[0])]
        gates.append(_sigmoid(jnp.concatenate(cols, axis=1) + b_ref[...]))
    r_gate, i_gate = gates
    nl = -lamp_ref[...]
    softplus = jnp.maximum(nl, 0.0) + jnp.log1p(jnp.exp(-jnp.abs(nl)))
    log_a = (-LRU_C * softplus) * r_gate
    a_gate = jnp.exp(log_a)
    a_ref[...] = a_gate
    u_ref[...] = jnp.sqrt(1.0 - a_gate * a_gate) * (i_gate * xc)
    _lru_scan(a_ref, u_ref, hl_ref)
    ya = u_ref[...] * _gelu_tanh(in_proj(d, 2 * d))
    proj_a = jnp.dot(ya.astype(BF16), pa_ref[...], preferred_element_type=F32)

    xb = in_proj(2 * d, 2 * d + w5)
    xb_bf = xb.astype(BF16)
    for h in range(halves):
        s5_ref[:, h * hs2:(h + 1) * hs2] = jnp.dot(
            xb_bf[:, h * S5_HALF_CH:(h + 1) * S5_HALF_CH], bd_ref[h], preferred_element_type=F32)
    _s5_scan(s5_ref, lam_ref, hs_ref, halves)
    ys = [jnp.dot(s5_ref[:, h * hs2:(h + 1) * hs2].astype(BF16), cd_ref[h], preferred_element_type=F32)
          for h in range(halves)]
    yb = _gelu_tanh(jnp.concatenate(ys, axis=1) + sd_ref[...] * xb)
    yb = yb * _sigmoid(jnp.dot(yb.astype(BF16), gw_ref[...], preferred_element_type=F32) + gb_ref[...])
    proj_b = jnp.dot(yb.astype(BF16), pb_ref[...], preferred_element_type=F32)

    m = (_sigmoid(in_proj(2 * d + w5, 3 * d + w5)) * proj_a
         + _sigmoid(in_proj(3 * d + w5, 4 * d + w5)) * proj_b)
    y = jnp.dot(m.astype(BF16), wo_ref[...], preferred_element_type=F32)
    out = (x3 + gt * y.reshape(tt, nb, d)).reshape(ts, d)
    for c in range(lane_chunks):
        tm_ref[c] = out[:, c * lanes:(c + 1) * lanes]
    for b in range(nb):
        for c in range(lane_chunks):
            o_ref[b, :, c * lanes:(c + 1) * lanes] = tm_ref[c, pl.ds(b, tt, stride=nb), :]


def _mixer(x3d, mod2, norm_g, w_in, b_in, conv_w, conv_b, wr_bd, b_r, wi_bd, b_i, lam_p, proj_a,
           bd, cd, lam, s5_d, glu_w, glu_b, proj_b, w_out):
    bsz, seq, d = x3d.shape
    tt = MIX_TT
    ts = tt * bsz
    n_states2 = bd.shape[0] * bd.shape[2]
    row = lambda v: v.reshape(1, -1)
    consts = [mod2, row(norm_g), w_in, row(b_in), conv_w, row(conv_b), wr_bd, row(b_r), wi_bd, row(b_i),
              row(lam_p), proj_a, bd, cd, lam, row(s5_d), glu_w, row(glu_b), proj_b, w_out]
    blk = pl.BlockSpec((bsz, tt, d), lambda i: (0, i, 0))
    return pl.pallas_call(
        _mixer_kernel,
        grid=(seq // tt,),
        in_specs=[blk] + [_const_spec(c.shape) for c in consts],
        out_specs=blk,
        out_shape=jax.ShapeDtypeStruct((bsz, seq, d), F32),
        scratch_shapes=[pltpu.VMEM((d // LANES, ts, LANES), F32),
                        pltpu.VMEM((ts + (CONV_WIDTH - 1) * bsz, d), F32),
                        pltpu.VMEM((ts, d), F32),
                        pltpu.VMEM((ts, d), F32),
                        pltpu.VMEM((ts, n_states2), F32),
                        pltpu.VMEM((bsz, d), F32),
                        pltpu.VMEM((bsz, n_states2), F32)],
        compiler_params=pltpu.CompilerParams(dimension_semantics=("arbitrary",),
                                             vmem_limit_bytes=VMEM_LIMIT),
        name="mixer",
    )(x3d, *consts)


def kernel(x, c, mod_w, mod_b, norm1_g, ffn1_w_up, ffn1_w_down, norm2_g, w_in, b_in, conv_w, conv_b, lru_w_r, lru_b_r, lru_w_i, lru_b_i, lru_lambda, proj_a, s5_a_re, s5_a_im, s5_log_dt, s5_b_re, s5_b_im, s5_c_re, s5_c_im, s5_d, glu_w, glu_b, proj_b, w_out, norm3_g, ffn2_w_up, ffn2_w_down, final_g):
    bsz, seq, d = x.shape
    assert bsz == SUBLANES, "time-major layout puts the batch on the 8 sublanes"
    depth = mod_w.shape[0]
    bf = lambda w: w.astype(BF16)
    for l in range(depth):
        mod2 = _mod(c, mod_w[l], mod_b[l])
        mod3 = mod2.reshape(bsz, 1, N_MOD * d)
        lam, bd, cd = _s5_prep(s5_a_re[l], s5_a_im[l], s5_log_dt[l], s5_b_re[l], s5_b_im[l],
                               s5_c_re[l], s5_c_im[l])
        wr_bd, wi_bd = _gate_prep(lru_w_r[l], lru_w_i[l])
        last = l == depth - 1
        x2d = _ffn(x.reshape(bsz * seq, d), mod3, norm1_g[l], bf(ffn1_w_up[l]), bf(ffn1_w_down[l]), final_g,
                   seq=seq, mod_base=0, final_norm=False)
        x3d = _mixer(x2d.reshape(bsz, seq, d), mod2, norm2_g[l], bf(w_in[l]), b_in[l], conv_w[l], conv_b[l],
                     wr_bd, lru_b_r[l], wi_bd, lru_b_i[l], lru_lambda[l], bf(proj_a[l]),
                     bd, cd, lam, s5_d[l], bf(glu_w[l]), glu_b[l], bf(proj_b[l]), bf(w_out[l]))
        x = _ffn(x3d.reshape(bsz * seq, d), mod3, norm3_g[l], bf(ffn2_w_up[l]), bf(ffn2_w_down[l]), final_g,
                 seq=seq, mod_base=6, final_norm=last).reshape(bsz, seq, d)
    return x
```

```python
import functools
import math

import jax
import jax.numpy as jnp
from jax import lax
from jax.experimental import pallas as pl
from jax.experimental.pallas import tpu as pltpu

F32 = jnp.float32
BF16 = jnp.bfloat16

EPS = 1e-6
LRU_C = 8.0
CONV_WIDTH = 4
S5_GROUP = 16
S5_STATE = 64
N_MOD = 9

SUBLANES = 8
LANES = 128
GATE_TILE = 256
S5_HALF_CH = 256
VMEM_LIMIT = 56 * 1024 * 1024

FFN_TM = 512
MIX_TT = 64
S5_LANE_CHUNK = 512


def _sigmoid(x):
    return 1.0 / (1.0 + jnp.exp(-x))


def _gelu_tanh(x):
    c = math.sqrt(2.0 / math.pi)
    return 0.5 * x * (1.0 + jnp.tanh(c * (x + 0.044715 * (x * x * x))))


def _rms_norm(x, g):
    ms = jnp.mean(x * x, axis=-1, keepdims=True)
    return x * lax.rsqrt(ms + EPS) * g


def _const_spec(shape):
    nd = len(shape)
    return pl.BlockSpec(shape, lambda *_: (0,) * nd, pipeline_mode=pl.Buffered(1))


def _mod_kernel(c_ref, w_ref, b_ref, o_ref):
    c = c_ref[...]
    ca = c * _sigmoid(c)
    o_ref[...] = jnp.dot(ca.astype(BF16), w_ref[...].astype(BF16),
                         preferred_element_type=F32) + b_ref[...]


def _mod(c, mod_w, mod_b):
    bsz, d = c.shape
    n = mod_w.shape[1]
    tn = d
    return pl.pallas_call(
        _mod_kernel,
        grid=(n // tn,),
        in_specs=[pl.BlockSpec((bsz, d), lambda j: (0, 0)),
                  pl.BlockSpec((d, tn), lambda j: (0, j)),
                  pl.BlockSpec((1, tn), lambda j: (0, j))],
        out_specs=pl.BlockSpec((bsz, tn), lambda j: (0, j)),
        out_shape=jax.ShapeDtypeStruct((bsz, n), F32),
        name="mod",
    )(c, mod_w, mod_b.reshape(1, n))


def _cmul(ar, ai, br, bi):
    return ar * br - ai * bi, ar * bi + ai * br


def _s5_discretise(a_re, a_im, log_dt):
    dt = jnp.exp(log_dt)
    mag = jnp.exp(a_re * dt)
    lr = mag * jnp.cos(a_im * dt)
    li = mag * jnp.sin(a_im * dt)
    den = a_re * a_re + a_im * a_im
    nr = lr - 1.0
    cr = (nr * a_re + li * a_im) / den
    ci = (li * a_re - nr * a_im) / den
    return lr, li, cr, ci


def _s5_prep_kernel(are_g, aim_g, ldt_g, are_r, aim_r, ldt_r, bre_t, bim_t, cre_t, cim_t,
                    lam_ref, bd_ref, cd_ref):
    lr, li, _, _ = _s5_discretise(are_g[...], aim_g[...], ldt_g[...])
    lam_ref[0] = lr
    lam_ref[1] = li

    _, _, cr, ci = _s5_discretise(are_r[...], aim_r[...], ldt_r[...])
    bb_re, bb_im = _cmul(cr, ci, bre_t[...], bim_t[...])
    half_states = (S5_HALF_CH // S5_GROUP) * S5_STATE
    erow = lax.broadcasted_iota(jnp.int32, (S5_STATE, half_states), 0)
    ecol = lax.broadcasted_iota(jnp.int32, (S5_STATE, half_states), 1)
    spread = jnp.where((ecol % S5_STATE) == erow, 1.0, 0.0).astype(BF16)
    brow_g = lax.broadcasted_iota(jnp.int32, (S5_HALF_CH, half_states), 0) // S5_GROUP
    bcol_g = lax.broadcasted_iota(jnp.int32, (S5_HALF_CH, half_states), 1) // S5_STATE
    bmask = brow_g == bcol_g
    crow_g = lax.broadcasted_iota(jnp.int32, (half_states, S5_HALF_CH), 0) // S5_STATE
    ccol_g = lax.broadcasted_iota(jnp.int32, (half_states, S5_HALF_CH), 1) // S5_GROUP
    cmask = crow_g == ccol_g
    reps = half_states // S5_STATE
    for h in range(bd_ref.shape[0]):
        rows = slice(h * S5_HALF_CH, (h + 1) * S5_HALF_CH)
        for k, bb in enumerate((bb_re, bb_im)):
            tiled = jnp.dot(bb[rows, :].astype(BF16), spread, preferred_element_type=F32)
            bd_ref[h, :, k * half_states:(k + 1) * half_states] = jnp.where(bmask, tiled, 0.0).astype(BF16)
        for k, (ct, sign) in enumerate(((cre_t, 1.0), (cim_t, -1.0))):
            tiled = jnp.concatenate([ct[:, rows]] * reps, axis=0)
            cd_ref[h, k * half_states:(k + 1) * half_states, :] = jnp.where(cmask, sign * tiled, 0.0).astype(BF16)


def _s5_prep(a_re, a_im, log_dt, b_re, b_im, c_re, c_im):
    g, p = a_re.shape
    hch = b_re.shape[2]
    assert hch == S5_GROUP and p == S5_STATE
    width = g * hch
    halves = width // S5_HALF_CH
    half_states = (S5_HALF_CH // hch) * p
    rep = lambda a: jnp.repeat(a, hch, axis=0)
    ldt = log_dt.reshape(g, 1)
    bt = lambda b: jnp.transpose(b, (0, 2, 1)).reshape(width, p)
    ct = lambda c: jnp.transpose(c, (2, 0, 1)).reshape(p, width)
    lam, bd, cd = pl.pallas_call(
        _s5_prep_kernel,
        out_shape=(jax.ShapeDtypeStruct((2, g, p), F32),
                   jax.ShapeDtypeStruct((halves, S5_HALF_CH, 2 * half_states), BF16),
                   jax.ShapeDtypeStruct((halves, 2 * half_states, S5_HALF_CH), BF16)),
        name="s5_prep",
    )(a_re, a_im, ldt, rep(a_re), rep(a_im), rep(ldt), bt(b_re), bt(b_im), ct(c_re), ct(c_im))
    return lam.reshape(2, g * p), bd, cd


def _gate_prep_kernel(wr_ref, wi_ref, or_ref, oi_ref):
    hd = wr_ref.shape[1]
    per_tile = GATE_TILE // hd
    erow = lax.broadcasted_iota(jnp.int32, (hd, GATE_TILE), 0)
    ecol = lax.broadcasted_iota(jnp.int32, (hd, GATE_TILE), 1)
    for w_ref, o_ref in ((wr_ref, or_ref), (wi_ref, oi_ref)):
        for a in range(o_ref.shape[0]):
            for b in range(per_tile):
                place = jnp.where(ecol == erow + b * hd, 1.0, 0.0).astype(BF16)
                o_ref[a, b * hd:(b + 1) * hd, :] = jnp.dot(
                    w_ref[a * per_tile + b].astype(BF16), place, preferred_element_type=F32).astype(BF16)


def _gate_prep(w_r, w_i):
    heads, hd, _ = w_r.shape
    tiles = heads * hd // GATE_TILE
    shp = jax.ShapeDtypeStruct((tiles, GATE_TILE, GATE_TILE), BF16)
    return pl.pallas_call(_gate_prep_kernel, out_shape=(shp, shp), name="gate_prep")(w_r, w_i)


def _ffn_kernel(x_ref, mod_ref, g_ref, wup_ref, wdn_ref, fg_ref, o_ref, *, mod_base, final_norm):
    d = x_ref.shape[1]
    f = wdn_ref.shape[0]
    x = x_ref[...]
    sh = mod_ref[:, (mod_base + 0) * d:(mod_base + 1) * d]
    sc = mod_ref[:, (mod_base + 1) * d:(mod_base + 2) * d]
    gt = mod_ref[:, (mod_base + 2) * d:(mod_base + 3) * d]
    u = _rms_norm(x, g_ref[...]) * (1.0 + sc) + sh
    h = jnp.dot(u.astype(BF16), wup_ref[...], preferred_element_type=F32)
    a = h[:, :f]
    b = h[:, f:]
    act = (a * _sigmoid(a)) * b
    y = jnp.dot(act.astype(BF16), wdn_ref[...], preferred_element_type=F32)
    out = x + (0.5 * gt) * y
    if final_norm:
        out = _rms_norm(out, fg_ref[...])
    o_ref[...] = out


def _ffn(x2d, mod3, norm_g, w_up, w_down, final_g, *, seq, mod_base, final_norm):
    t, d = x2d.shape
    f = w_down.shape[0]
    tm = FFN_TM
    tiles_per_seq = seq // tm
    kern = functools.partial(_ffn_kernel, mod_base=mod_base, final_norm=final_norm)
    return pl.pallas_call(
        kern,
        grid=(t // tm,),
        in_specs=[pl.BlockSpec((tm, d), lambda i: (i, 0)),
                  pl.BlockSpec((None, 1, mod3.shape[2]), lambda i: (i // tiles_per_seq, 0, 0)),
                  _const_spec((1, d)),
                  _const_spec((d, 2 * f)),
                  _const_spec((f, d)),
                  _const_spec((1, d))],
        out_specs=pl.BlockSpec((tm, d), lambda i: (i, 0)),
        out_shape=jax.ShapeDtypeStruct((t, d), F32),
        compiler_params=pltpu.CompilerParams(dimension_semantics=("arbitrary",),
                                             vmem_limit_bytes=VMEM_LIMIT),
        name="ffn_final" if final_norm else "ffn",
    )(x2d, mod3, norm_g.reshape(1, d), w_up, w_down, final_g.reshape(1, d))


def _lru_scan(a_ref, u_ref, carry_ref):
    ts = u_ref.shape[0]
    h = carry_ref[...]
    for t in range(ts // SUBLANES):
        sl = slice(t * SUBLANES, (t + 1) * SUBLANES)
        h = a_ref[sl, :] * h + u_ref[sl, :]
        u_ref[sl, :] = h
    carry_ref[...] = h


def _s5_scan(u_ref, lam_ref, carry_ref, halves):
    ts = u_ref.shape[0]
    half_states = lam_ref.shape[1] // halves
    w = S5_LANE_CHUNK
    for h in range(halves):
        for j in range(half_states // w):
            cre = slice(2 * half_states * h + j * w, 2 * half_states * h + (j + 1) * w)
            cim = slice(2 * half_states * h + half_states + j * w, 2 * half_states * h + half_states + (j + 1) * w)
            lcs = slice(half_states * h + j * w, half_states * h + (j + 1) * w)
            lr = jnp.broadcast_to(lam_ref[0:1, lcs], (SUBLANES, w))
            li = jnp.broadcast_to(lam_ref[1:2, lcs], (SUBLANES, w))
            hr = carry_ref[:, cre]
            hi = carry_ref[:, cim]
            for t in range(ts // SUBLANES):
                sl = slice(t * SUBLANES, (t + 1) * SUBLANES)
                hr, hi = (u_ref[sl, cre] + (lr * hr - li * hi),
                          u_ref[sl, cim] + (lr * hi + li * hr))
                u_ref[sl, cre] = hr
                u_ref[sl, cim] = hi
            carry_ref[:, cre] = hr
            carry_ref[:, cim] = hi


def _mixer_kernel(x_ref, mod_ref, g_ref, win_ref, bin_ref, cw_ref, cb_ref,
                  wr_ref, br_ref, wi_ref, bi_ref, lamp_ref, pa_ref,
                  bd_ref, cd_ref, lam_ref, sd_ref, gw_ref, gb_ref, pb_ref, wo_ref,
                  o_ref,
                  tm_ref, xpad_ref, a_ref, u_ref, s5_ref, hl_ref, hs_ref):
    nb, tt, d = x_ref.shape
    ts = nb * tt
    lane_chunks = tm_ref.shape[0]
    lanes = tm_ref.shape[2]
    w5 = sd_ref.shape[1]
    halves = bd_ref.shape[0]
    hs2 = bd_ref.shape[2]
    pad = (CONV_WIDTH - 1) * nb

    @pl.when(pl.program_id(0) == 0)
    def _():
        xpad_ref[0:pad, :] = jnp.zeros((pad, d), F32)
        hl_ref[...] = jnp.zeros_like(hl_ref)
        hs_ref[...] = jnp.zeros_like(hs_ref)

    def per_batch(v):
        return v[None, :, :]

    for b in range(nb):
        for c in range(lane_chunks):
            tm_ref[c, pl.ds(b, tt, stride=nb), :] = x_ref[b, :, c * lanes:(c + 1) * lanes]
    x = jnp.concatenate([tm_ref[c] for c in range(lane_chunks)], axis=1)
    x3 = x.reshape(tt, nb, d)
    sh = per_batch(mod_ref[:, 3 * d:4 * d])
    sc = per_batch(mod_ref[:, 4 * d:5 * d])
    gt = per_batch(mod_ref[:, 5 * d:6 * d])
    u_bf = (_rms_norm(x3, g_ref[...]) * (1.0 + sc) + sh).reshape(ts, d).astype(BF16)

    def in_proj(lo, hi):
        return jnp.dot(u_bf, win_ref[:, lo:hi], preferred_element_type=F32) + bin_ref[:, lo:hi]

    xpad_ref[pad:pad + ts, :] = in_proj(0, d)
    xc = cb_ref[...]
    for k in range(CONV_WIDTH):
        xc = xc + xpad_ref[k * nb:k * nb + ts, :] * cw_ref[k:k + 1, :]
    xpad_ref[0:pad, :] = xpad_ref[ts:ts + pad, :]
    xc_bf = xc.astype(BF16)
    gates = []
    for w_ref, b_ref in ((wr_ref, br_ref), (wi_ref, bi_ref)):
        cols = [jnp.dot(xc_bf[:, j * GATE_TILE:(j + 1) * GATE_TILE], w_ref[j], preferred_element_type=F32)
                for j in range(w_ref.shape[0])]
        gates.append(_sigmoid(jnp.concatenate(cols, axis=1) + b_ref[...]))
    r_gate, i_gate = gates
    nl = -lamp_ref[...]
    softplus = jnp.maximum(nl, 0.0) + jnp.log1p(jnp.exp(-jnp.abs(nl)))
    log_a = (-LRU_C * softplus) * r_gate
    a_gate = jnp.exp(log_a)
    a_ref[...] = a_gate
    u_ref[...] = jnp.sqrt(1.0 - a_gate * a_gate) * (i_gate * xc)
    _lru_scan(a_ref, u_ref, hl_ref)
    ya = u_ref[...] * _gelu_tanh(in_proj(d, 2 * d))
    proj_a = jnp.dot(ya.astype(BF16), pa_ref[...], preferred_element_type=F32)

    xb = in_proj(2 * d, 2 * d + w5)
    xb_bf = xb.astype(BF16)
    for h in range(halves):
        s5_ref[:, h * hs2:(h + 1) * hs2] = jnp.dot(
            xb_bf[:, h * S5_HALF_CH:(h + 1) * S5_HALF_CH], bd_ref[h], preferred_element_type=F32)
    _s5_scan(s5_ref, lam_ref, hs_ref, halves)
    ys = [jnp.dot(s5_ref[:, h * hs2:(h + 1) * hs2].astype(BF16), cd_ref[h], preferred_element_type=F32)
          for h in range(halves)]
    yb = _gelu_tanh(jnp.concatenate(ys, axis=1) + sd_ref[...] * xb)
    yb = yb * _sigmoid(jnp.dot(yb.astype(BF16), gw_ref[...], preferred_element_type=F32) + gb_ref[...])
    proj_b = jnp.dot(yb.astype(BF16), pb_ref[...], preferred_element_type=F32)

    m = (_sigmoid(in_proj(2 * d + w5, 3 * d + w5)) * proj_a
         + _sigmoid(in_proj(3 * d + w5, 4 * d + w5)) * proj_b)
    y = jnp.dot(m.astype(BF16), wo_ref[...], preferred_element_type=F32)
    out = (x3 + gt * y.reshape(tt, nb, d)).reshape(ts, d)
    for c in range(lane_chunks):
        tm_ref[c] = out[:, c * lanes:(c + 1) * lanes]
    for b in range(nb):
        for c in range(lane_chunks):
            o_ref[b, :, c * lanes:(c + 1) * lanes] = tm_ref[c, pl.ds(b, tt, stride=nb), :]


def _mixer(x3d, mod2, norm_g, w_in, b_in, conv_w, conv_b, wr_bd, b_r, wi_bd, b_i, lam_p, proj_a,
           bd, cd, lam, s5_d, glu_w, glu_b, proj_b, w_out):
    bsz, seq, d = x3d.shape
    tt = MIX_TT
    ts = tt * bsz
    n_states2 = bd.shape[0] * bd.shape[2]
    row = lambda v: v.reshape(1, -1)
    consts = [mod2, row(norm_g), w_in, row(b_in), conv_w, row(conv_b), wr_bd, row(b_r), wi_bd, row(b_i),
              row(lam_p), proj_a, bd, cd, lam, row(s5_d), glu_w, row(glu_b), proj_b, w_out]
    blk = pl.BlockSpec((bsz, tt, d), lambda i: (0, i, 0))
    return pl.pallas_call(
        _mixer_kernel,
        grid=(seq // tt,),
        in_specs=[blk] + [_const_spec(c.shape) for c in consts],
        out_specs=blk,
        out_shape=jax.ShapeDtypeStruct((bsz, seq, d), F32),
        scratch_shapes=[pltpu.VMEM((d // LANES, ts, LANES), F32),
                        pltpu.VMEM((ts + (CONV_WIDTH - 1) * bsz, d), F32),
                        pltpu.VMEM((ts, d), F32),
                        pltpu.VMEM((ts, d), F32),
                        pltpu.VMEM((ts, n_states2), F32),
                        pltpu.VMEM((bsz, d), F32),
                        pltpu.VMEM((bsz, n_states2), F32)],
        compiler_params=pltpu.CompilerParams(dimension_semantics=("arbitrary",),
                                             vmem_limit_bytes=VMEM_LIMIT),
        name="mixer",
    )(x3d, *consts)


def kernel(x, c, mod_w, mod_b, norm1_g, ffn1_w_up, ffn1_w_down, norm2_g, w_in, b_in, conv_w, conv_b, lru_w_r, lru_b_r, lru_w_i, lru_b_i, lru_lambda, proj_a, s5_a_re, s5_a_im, s5_log_dt, s5_b_re, s5_b_im, s5_c_re, s5_c_im, s5_d, glu_w, glu_b, proj_b, w_out, norm3_g, ffn2_w_up, ffn2_w_down, final_g):
    bsz, seq, d = x.shape
    assert bsz == SUBLANES, "time-major layout puts the batch on the 8 sublanes"
    depth = mod_w.shape[0]
    bf = lambda w: w.astype(BF16)
    for l in range(depth):
        mod2 = _mod(c, mod_w[l], mod_b[l])
        mod3 = mod2.reshape(bsz, 1, N_MOD * d)
        lam, bd, cd = _s5_prep(s5_a_re[l], s5_a_im[l], s5_log_dt[l], s5_b_re[l], s5_b_im[l],
                               s5_c_re[l], s5_c_im[l])
        wr_bd, wi_bd = _gate_prep(lru_w_r[l], lru_w_i[l])
        last = l == depth - 1
        x2d = _ffn(x.reshape(bsz * seq, d), mod3, norm1_g[l], bf(ffn1_w_up[l]), bf(ffn1_w_down[l]), final_g,
                   seq=seq, mod_base=0, final_norm=False)
        x3d = _mixer(x2d.reshape(bsz, seq, d), mod2, norm2_g[l], bf(w_in[l]), b_in[l], conv_w[l], conv_b[l],
                     wr_bd, lru_b_r[l], wi_bd, lru_b_i[l], lru_lambda[l], bf(proj_a[l]),
                     bd, cd, lam, s5_d[l], bf(glu_w[l]), glu_b[l], bf(proj_b[l]), bf(w_out[l]))
        x = _ffn(x3d.reshape(bsz * seq, d), mod3, norm3_g[l], bf(ffn2_w_up[l]), bf(ffn2_w_down[l]), final_g,
                 seq=seq, mod_base=6, final_norm=last).reshape(bsz, seq, d)
    return x
```

```python
import functools
import math

import jax
import jax.numpy as jnp
from jax import lax
from jax.experimental import pallas as pl
from jax.experimental.pallas import tpu as pltpu

F32 = jnp.float32
BF16 = jnp.bfloat16

EPS = 1e-6
LRU_C = 8.0
CONV_WIDTH = 4
S5_GROUP = 16
S5_STATE = 64
N_MOD = 9

SUBLANES = 8
LANES = 128
GATE_TILE = 256
S5_HALF_CH = 256
VMEM_LIMIT = 56 * 1024 * 1024

FFN_TM = 512
MIX_TT = 64
S5_LANE_CHUNK = 512


def _sigmoid(x):
    return 0.5 * jnp.tanh(0.5 * x) + 0.5


def _gelu_tanh(x):
    c = math.sqrt(2.0 / math.pi)
    return 0.5 * x * (1.0 + jnp.tanh(c * (x + 0.044715 * (x * x * x))))


def _rms_norm(x, g):
    ms = jnp.mean(x * x, axis=-1, keepdims=True)
    return x * lax.rsqrt(ms + EPS) * g


def _const_spec(shape):
    nd = len(shape)
    return pl.BlockSpec(shape, lambda *_: (0,) * nd, pipeline_mode=pl.Buffered(1))


def _mod_kernel(c_ref, w_ref, b_ref, o_ref):
    c = c_ref[...]
    ca = c * _sigmoid(c)
    o_ref[...] = jnp.dot(ca.astype(BF16), w_ref[...].astype(BF16),
                         preferred_element_type=F32) + b_ref[...]


def _mod(c, mod_w, mod_b):
    bsz, d = c.shape
    n = mod_w.shape[1]
    tn = d
    return pl.pallas_call(
        _mod_kernel,
        grid=(n // tn,),
        in_specs=[pl.BlockSpec((bsz, d), lambda j: (0, 0)),
                  pl.BlockSpec((d, tn), lambda j: (0, j)),
                  pl.BlockSpec((1, tn), lambda j: (0, j))],
        out_specs=pl.BlockSpec((bsz, tn), lambda j: (0, j)),
        out_shape=jax.ShapeDtypeStruct((bsz, n), F32),
        name="mod",
    )(c, mod_w, mod_b.reshape(1, n))


def _cmul(ar, ai, br, bi):
    return ar * br - ai * bi, ar * bi + ai * br


def _s5_discretise(a_re, a_im, log_dt):
    dt = jnp.exp(log_dt)
    mag = jnp.exp(a_re * dt)
    lr = mag * jnp.cos(a_im * dt)
    li = mag * jnp.sin(a_im * dt)
    den = a_re * a_re + a_im * a_im
    nr = lr - 1.0
    cr = (nr * a_re + li * a_im) / den
    ci = (li * a_re - nr * a_im) / den
    return lr, li, cr, ci


def _s5_prep_kernel(are_g, aim_g, ldt_g, are_r, aim_r, ldt_r, bre_t, bim_t, cre_t, cim_t,
                    lam_ref, bd_ref, cd_ref):
    lr, li, _, _ = _s5_discretise(are_g[...], aim_g[...], ldt_g[...])
    lam_ref[0] = lr
    lam_ref[1] = li

    _, _, cr, ci = _s5_discretise(are_r[...], aim_r[...], ldt_r[...])
    bb_re, bb_im = _cmul(cr, ci, bre_t[...], bim_t[...])
    half_states = (S5_HALF_CH // S5_GROUP) * S5_STATE
    erow = lax.broadcasted_iota(jnp.int32, (S5_STATE, half_states), 0)
    ecol = lax.broadcasted_iota(jnp.int32, (S5_STATE, half_states), 1)
    spread = jnp.where((ecol % S5_STATE) == erow, 1.0, 0.0).astype(BF16)
    brow_g = lax.broadcasted_iota(jnp.int32, (S5_HALF_CH, half_states), 0) // S5_GROUP
    bcol_g = lax.broadcasted_iota(jnp.int32, (S5_HALF_CH, half_states), 1) // S5_STATE
    bmask = brow_g == bcol_g
    crow_g = lax.broadcasted_iota(jnp.int32, (half_states, S5_HALF_CH), 0) // S5_STATE
    ccol_g = lax.broadcasted_iota(jnp.int32, (half_states, S5_HALF_CH), 1) // S5_GROUP
    cmask = crow_g == ccol_g
    reps = half_states // S5_STATE
    for h in range(bd_ref.shape[0]):
        rows = slice(h * S5_HALF_CH, (h + 1) * S5_HALF_CH)
        for k, bb in enumerate((bb_re, bb_im)):
            tiled = jnp.dot(bb[rows, :].astype(BF16), spread, preferred_element_type=F32)
            bd_ref[h, :, k * half_states:(k + 1) * half_states] = jnp.where(bmask, tiled, 0.0).astype(BF16)
        for k, (ct, sign) in enumerate(((cre_t, 1.0), (cim_t, -1.0))):
            tiled = jnp.concatenate([ct[:, rows]] * reps, axis=0)
            cd_ref[h, k * half_states:(k + 1) * half_states, :] = jnp.where(cmask, sign * tiled, 0.0).astype(BF16)


def _s5_prep(a_re, a_im, log_dt, b_re, b_im, c_re, c_im):
    g, p = a_re.shape
    hch = b_re.shape[2]
    assert hch == S5_GROUP and p == S5_STATE
    width = g * hch
    halves = width // S5_HALF_CH
    half_states = (S5_HALF_CH // hch) * p
    rep = lambda a: jnp.repeat(a, hch, axis=0)
    ldt = log_dt.reshape(g, 1)
    bt = lambda b: jnp.transpose(b, (0, 2, 1)).reshape(width, p)
    ct = lambda c: jnp.transpose(c, (2, 0, 1)).reshape(p, width)
    lam, bd, cd = pl.pallas_call(
        _s5_prep_kernel,
        out_shape=(jax.ShapeDtypeStruct((2, g, p), F32),
                   jax.ShapeDtypeStruct((halves, S5_HALF_CH, 2 * half_states), BF16),
                   jax.ShapeDtypeStruct((halves, 2 * half_states, S5_HALF_CH), BF16)),
        name="s5_prep",
    )(a_re, a_im, ldt, rep(a_re), rep(a_im), rep(ldt), bt(b_re), bt(b_im), ct(c_re), ct(c_im))
    return lam.reshape(2, g * p), bd, cd


def _gate_prep_kernel(wr_ref, wi_ref, or_ref, oi_ref):
    hd = wr_ref.shape[1]
    per_tile = GATE_TILE // hd
    erow = lax.broadcasted_iota(jnp.int32, (hd, GATE_TILE), 0)
    ecol = lax.broadcasted_iota(jnp.int32, (hd, GATE_TILE), 1)
    for w_ref, o_ref in ((wr_ref, or_ref), (wi_ref, oi_ref)):
        for a in range(o_ref.shape[0]):
            for b in range(per_tile):
                place = jnp.where(ecol == erow + b * hd, 1.0, 0.0).astype(BF16)
                o_ref[a, b * hd:(b + 1) * hd, :] = jnp.dot(
                    w_ref[a * per_tile + b].astype(BF16), place, preferred_element_type=F32).astype(BF16)


def _gate_prep(w_r, w_i):
    heads, hd, _ = w_r.shape
    tiles = heads * hd // GATE_TILE
    shp = jax.ShapeDtypeStruct((tiles, GATE_TILE, GATE_TILE), BF16)
    return pl.pallas_call(_gate_prep_kernel, out_shape=(shp, shp), name="gate_prep")(w_r, w_i)


def _ffn_kernel(x_ref, mod_ref, g_ref, wup_ref, wdn_ref, fg_ref, o_ref, *, mod_base, final_norm):
    d = x_ref.shape[1]
    f = wdn_ref.shape[0]
    x = x_ref[...]
    sh = mod_ref[:, (mod_base + 0) * d:(mod_base + 1) * d]
    sc = mod_ref[:, (mod_base + 1) * d:(mod_base + 2) * d]
    gt = mod_ref[:, (mod_base + 2) * d:(mod_base + 3) * d]
    u = _rms_norm(x, g_ref[...]) * (1.0 + sc) + sh
    h = jnp.dot(u.astype(BF16), wup_ref[...], preferred_element_type=F32)
    a = h[:, :f]
    b = h[:, f:]
    act = (a * _sigmoid(a)) * b
    y = jnp.dot(act.astype(BF16), wdn_ref[...], preferred_element_type=F32)
    out = x + (0.5 * gt) * y
    if final_norm:
        out = _rms_norm(out, fg_ref[...])
    o_ref[...] = out


def _ffn(x2d, mod3, norm_g, w_up, w_down, final_g, *, seq, mod_base, final_norm):
    t, d = x2d.shape
    f = w_down.shape[0]
    tm = FFN_TM
    tiles_per_seq = seq // tm
    kern = functools.partial(_ffn_kernel, mod_base=mod_base, final_norm=final_norm)
    return pl.pallas_call(
        kern,
        grid=(t // tm,),
        in_specs=[pl.BlockSpec((tm, d), lambda i: (i, 0)),
                  pl.BlockSpec((None, 1, mod3.shape[2]), lambda i: (i // tiles_per_seq, 0, 0)),
                  _const_spec((1, d)),
                  _const_spec((d, 2 * f)),
                  _const_spec((f, d)),
                  _const_spec((1, d))],
        out_specs=pl.BlockSpec((tm, d), lambda i: (i, 0)),
        out_shape=jax.ShapeDtypeStruct((t, d), F32),
        compiler_params=pltpu.CompilerParams(dimension_semantics=("arbitrary",),
                                             vmem_limit_bytes=VMEM_LIMIT),
        name="ffn_final" if final_norm else "ffn",
    )(x2d, mod3, norm_g.reshape(1, d), w_up, w_down, final_g.reshape(1, d))


def _lru_scan(a_ref, u_ref, carry_ref):
    ts = u_ref.shape[0]
    h = carry_ref[...]
    for t in range(ts // SUBLANES):
        sl = slice(t * SUBLANES, (t + 1) * SUBLANES)
        h = a_ref[sl, :] * h + u_ref[sl, :]
        u_ref[sl, :] = h
    carry_ref[...] = h


def _s5_scan(u_ref, lam_ref, carry_ref, halves):
    ts = u_ref.shape[0]
    half_states = lam_ref.shape[1] // halves
    w = S5_LANE_CHUNK
    for h in range(halves):
        for j in range(half_states // w):
            cre = slice(2 * half_states * h + j * w, 2 * half_states * h + (j + 1) * w)
            cim = slice(2 * half_states * h + half_states + j * w, 2 * half_states * h + half_states + (j + 1) * w)
            lcs = slice(half_states * h + j * w, half_states * h + (j + 1) * w)
            lr = jnp.broadcast_to(lam_ref[0:1, lcs], (SUBLANES, w))
            li = jnp.broadcast_to(lam_ref[1:2, lcs], (SUBLANES, w))
            hr = carry_ref[:, cre]
            hi = carry_ref[:, cim]
            for t in range(ts // SUBLANES):
                sl = slice(t * SUBLANES, (t + 1) * SUBLANES)
                hr, hi = (u_ref[sl, cre] + (lr * hr - li * hi),
                          u_ref[sl, cim] + (lr * hi + li * hr))
                u_ref[sl, cre] = hr
                u_ref[sl, cim] = hi
            carry_ref[:, cre] = hr
            carry_ref[:, cim] = hi


def _mixer_kernel(x_ref, mod_ref, g_ref, win_ref, bin_ref, cw_ref, cb_ref,
                  wr_ref, br_ref, wi_ref, bi_ref, lamp_ref, pa_ref,
                  bd_ref, cd_ref, lam_ref, sd_ref, gw_ref, gb_ref, pb_ref, wo_ref,
                  o_ref,
                  tm_ref, xpad_ref, a_ref, u_ref, s5_ref, hl_ref, hs_ref):
    nb, tt, d = x_ref.shape
    ts = nb * tt
    lane_chunks = tm_ref.shape[0]
    lanes = tm_ref.shape[2]
    w5 = sd_ref.shape[1]
    halves = bd_ref.shape[0]
    hs2 = bd_ref.shape[2]
    pad = (CONV_WIDTH - 1) * nb

    @pl.when(pl.program_id(0) == 0)
    def _():
        xpad_ref[0:pad, :] = jnp.zeros((pad, d), F32)
        hl_ref[...] = jnp.zeros_like(hl_ref)
        hs_ref[...] = jnp.zeros_like(hs_ref)

    def per_batch(v):
        return v[None, :, :]

    for b in range(nb):
        for c in range(lane_chunks):
            tm_ref[c, pl.ds(b, tt, stride=nb), :] = x_ref[b, :, c * lanes:(c + 1) * lanes]
    x = jnp.concatenate([tm_ref[c] for c in range(lane_chunks)], axis=1)
    x3 = x.reshape(tt, nb, d)
    sh = per_batch(mod_ref[:, 3 * d:4 * d])
    sc = per_batch(mod_ref[:, 4 * d:5 * d])
    gt = per_batch(mod_ref[:, 5 * d:6 * d])
    u_bf = (_rms_norm(x3, g_ref[...]) * (1.0 + sc) + sh).reshape(ts, d).astype(BF16)

    def in_proj(lo, hi):
        return jnp.dot(u_bf, win_ref[:, lo:hi], preferred_element_type=F32) + bin_ref[:, lo:hi]

    xpad_ref[pad:pad + ts, :] = in_proj(0, d)
    xc = cb_ref[...]
    for k in range(CONV_WIDTH):
        xc = xc + xpad_ref[k * nb:k * nb + ts, :] * cw_ref[k:k + 1, :]
    xpad_ref[0:pad, :] = xpad_ref[ts:ts + pad, :]
    xc_bf = xc.astype(BF16)
    gates = []
    for w_ref, b_ref in ((wr_ref, br_ref), (wi_ref, bi_ref)):
        cols = [jnp.dot(xc_bf[:, j * GATE_TILE:(j + 1) * GATE_TILE], w_ref[j], preferred_element_type=F32)
                for j in range(w_ref.shape[0])]
        gates.append(_sigmoid(jnp.concatenate(cols, axis=1) + b_ref[...]))
    r_gate, i_gate = gates
    nl = -lamp_ref[...]
    softplus = jnp.maximum(nl, 0.0) + jnp.log1p(jnp.exp(-jnp.abs(nl)))
    log_a = (-LRU_C * softplus) * r_gate
    a_gate = jnp.exp(log_a)
    a_ref[...] = a_gate
    one_m_a2 = 1.0 - a_gate * a_gate
    mult = jnp.where(one_m_a2 > 0.0, one_m_a2 * lax.rsqrt(one_m_a2), 0.0)
    u_ref[...] = mult * (i_gate * xc)
    _lru_scan(a_ref, u_ref, hl_ref)
    ya = u_ref[...] * _gelu_tanh(in_proj(d, 2 * d))
    proj_a = jnp.dot(ya.astype(BF16), pa_ref[...], preferred_element_type=F32)

    xb = in_proj(2 * d, 2 * d + w5)
    xb_bf = xb.astype(BF16)
    for h in range(halves):
        s5_ref[:, h * hs2:(h + 1) * hs2] = jnp.dot(
            xb_bf[:, h * S5_HALF_CH:(h + 1) * S5_HALF_CH], bd_ref[h], preferred_element_type=F32)
    _s5_scan(s5_ref, lam_ref, hs_ref, halves)
    ys = [jnp.dot(s5_ref[:, h * hs2:(h + 1) * hs2].astype(BF16), cd_ref[h], preferred_element_type=F32)
          for h in range(halves)]
    yb = _gelu_tanh(jnp.concatenate(ys, axis=1) + sd_ref[...] * xb)
    yb = yb * _sigmoid(jnp.dot(yb.astype(BF16), gw_ref[...], preferred_element_type=F32) + gb_ref[...])
    proj_b = jnp.dot(yb.astype(BF16), pb_ref[...], preferred_element_type=F32)

    m = (_sigmoid(in_proj(2 * d + w5, 3 * d + w5)) * proj_a
         + _sigmoid(in_proj(3 * d + w5, 4 * d + w5)) * proj_b)
    y = jnp.dot(m.astype(BF16), wo_ref[...], preferred_element_type=F32)
    out = (x3 + gt * y.reshape(tt, nb, d)).reshape(ts, d)
    for c in range(lane_chunks):
        tm_ref[c] = out[:, c * lanes:(c + 1) * lanes]
    for b in range(nb):
        for c in range(lane_chunks):
            o_ref[b, :, c * lanes:(c + 1) * lanes] = tm_ref[c, pl.ds(b, tt, stride=nb), :]


def _mixer(x3d, mod2, norm_g, w_in, b_in, conv_w, conv_b, wr_bd, b_r, wi_bd, b_i, lam_p, proj_a,
           bd, cd, lam, s5_d, glu_w, glu_b, proj_b, w_out):
    bsz, seq, d = x3d.shape
    tt = MIX_TT
    ts = tt * bsz
    n_states2 = bd.shape[0] * bd.shape[2]
    row = lambda v: v.reshape(1, -1)
    consts = [mod2, row(norm_g), w_in, row(b_in), conv_w, row(conv_b), wr_bd, row(b_r), wi_bd, row(b_i),
              row(lam_p), proj_a, bd, cd, lam, row(s5_d), glu_w, row(glu_b), proj_b, w_out]
    blk = pl.BlockSpec((bsz, tt, d), lambda i: (0, i, 0))
    return pl.pallas_call(
        _mixer_kernel,
        grid=(seq // tt,),
        in_specs=[blk] + [_const_spec(c.shape) for c in consts],
        out_specs=blk,
        out_shape=jax.ShapeDtypeStruct((bsz, seq, d), F32),
        scratch_shapes=[pltpu.VMEM((d // LANES, ts, LANES), F32),
                        pltpu.VMEM((ts + (CONV_WIDTH - 1) * bsz, d), F32),
                        pltpu.VMEM((ts, d), F32),
                        pltpu.VMEM((ts, d), F32),
                        pltpu.VMEM((ts, n_states2), F32),
                        pltpu.VMEM((bsz, d), F32),
                        pltpu.VMEM((bsz, n_states2), F32)],
        compiler_params=pltpu.CompilerParams(dimension_semantics=("arbitrary",),
                                             vmem_limit_bytes=VMEM_LIMIT),
        name="mixer",
    )(x3d, *consts)


def kernel(x, c, mod_w, mod_b, norm1_g, ffn1_w_up, ffn1_w_down, norm2_g, w_in, b_in, conv_w, conv_b, lru_w_r, lru_b_r, lru_w_i, lru_b_i, lru_lambda, proj_a, s5_a_re, s5_a_im, s5_log_dt, s5_b_re, s5_b_im, s5_c_re, s5_c_im, s5_d, glu_w, glu_b, proj_b, w_out, norm3_g, ffn2_w_up, ffn2_w_down, final_g):
    bsz, seq, d = x.shape
    assert bsz == SUBLANES, "time-major layout puts the batch on the 8 sublanes"
    depth = mod_w.shape[0]
    bf = lambda w: w.astype(BF16)
    for l in range(depth):
        mod2 = _mod(c, mod_w[l], mod_b[l])
        mod3 = mod2.reshape(bsz, 1, N_MOD * d)
        lam, bd, cd = _s5_prep(s5_a_re[l], s5_a_im[l], s5_log_dt[l], s5_b_re[l], s5_b_im[l],
                               s5_c_re[l], s5_c_im[l])
        wr_bd, wi_bd = _gate_prep(lru_w_r[l], lru_w_i[l])
        last = l == depth - 1
        x2d = _ffn(x.reshape(bsz * seq, d), mod3, norm1_g[l], bf(ffn1_w_up[l]), bf(ffn1_w_down[l]), final_g,
                   seq=seq, mod_base=0, final_norm=False)
        x3d = _mixer(x2d.reshape(bsz, seq, d), mod2, norm2_g[l], bf(w_in[l]), b_in[l], conv_w[l], conv_b[l],
                     wr_bd, lru_b_r[l], wi_bd, lru_b_i[l], lru_lambda[l], bf(proj_a[l]),
                     bd, cd, lam, s5_d[l], bf(glu_w[l]), glu_b[l], bf(proj_b[l]), bf(w_out[l]))
        x = _ffn(x3d.reshape(bsz * seq, d), mod3, norm3_g[l], bf(ffn2_w_up[l]), bf(ffn2_w_down[l]), final_g,
                 seq=seq, mod_base=6, final_norm=last).reshape(bsz, seq, d)
    return x
```

```python
import functools
import math

import jax
import jax.numpy as jnp
from jax import lax
from jax.experimental import pallas as pl
from jax.experimental.pallas import tpu as pltpu

F32 = jnp.float32
BF16 = jnp.bfloat16

EPS = 1e-6
LRU_C = 8.0
CONV_WIDTH = 4
S5_GROUP = 16
S5_STATE = 64
N_MOD = 9

SUBLANES = 8
LANES = 128
GATE_TILE = 256
S5_HALF_CH = 256
VMEM_LIMIT = 56 * 1024 * 1024

FFN_TM = 512
FFN_HEAD_COLS = 256
MIX_TT = 64
S5_LANE_CHUNK = 512


def _sigmoid(x):
    return 0.5 * jnp.tanh(0.5 * x) + 0.5


def _gelu_tanh(x):
    c = math.sqrt(2.0 / math.pi)
    return 0.5 * x * (1.0 + jnp.tanh(c * (x + 0.044715 * (x * x * x))))


def _rms_norm(x, g):
    ms = jnp.mean(x * x, axis=-1, keepdims=True)
    return x * lax.rsqrt(ms + EPS) * g


def _const_spec(shape):
    nd = len(shape)
    return pl.BlockSpec(shape, lambda *_: (0,) * nd, pipeline_mode=pl.Buffered(1))


def _mod_kernel(c_ref, w_ref, b_ref, o_ref):
    c = c_ref[...]
    ca = c * _sigmoid(c)
    o_ref[...] = jnp.dot(ca.astype(BF16), w_ref[...].astype(BF16),
                         preferred_element_type=F32) + b_ref[...]


def _mod(c, mod_w, mod_b):
    bsz, d = c.shape
    n = mod_w.shape[1]
    tn = d
    return pl.pallas_call(
        _mod_kernel,
        grid=(n // tn,),
        in_specs=[pl.BlockSpec((bsz, d), lambda j: (0, 0)),
                  pl.BlockSpec((d, tn), lambda j: (0, j)),
                  pl.BlockSpec((1, tn), lambda j: (0, j))],
        out_specs=pl.BlockSpec((bsz, tn), lambda j: (0, j)),
        out_shape=jax.ShapeDtypeStruct((bsz, n), F32),
        name="mod",
    )(c, mod_w, mod_b.reshape(1, n))


def _cmul(ar, ai, br, bi):
    return ar * br - ai * bi, ar * bi + ai * br


def _s5_discretise(a_re, a_im, log_dt):
    dt = jnp.exp(log_dt)
    mag = jnp.exp(a_re * dt)
    lr = mag * jnp.cos(a_im * dt)
    li = mag * jnp.sin(a_im * dt)
    den = a_re * a_re + a_im * a_im
    nr = lr - 1.0
    cr = (nr * a_re + li * a_im) / den
    ci = (li * a_re - nr * a_im) / den
    return lr, li, cr, ci


def _s5_prep_kernel(are_g, aim_g, ldt_g, are_r, aim_r, ldt_r, bre_t, bim_t, cre_t, cim_t,
                    lam_ref, bd_ref, cd_ref):
    lr, li, _, _ = _s5_discretise(are_g[...], aim_g[...], ldt_g[...])
    lam_ref[0] = lr
    lam_ref[1] = li

    _, _, cr, ci = _s5_discretise(are_r[...], aim_r[...], ldt_r[...])
    bb_re, bb_im = _cmul(cr, ci, bre_t[...], bim_t[...])
    half_states = (S5_HALF_CH // S5_GROUP) * S5_STATE
    erow = lax.broadcasted_iota(jnp.int32, (S5_STATE, half_states), 0)
    ecol = lax.broadcasted_iota(jnp.int32, (S5_STATE, half_states), 1)
    spread = jnp.where((ecol % S5_STATE) == erow, 1.0, 0.0).astype(BF16)
    brow_g = lax.broadcasted_iota(jnp.int32, (S5_HALF_CH, half_states), 0) // S5_GROUP
    bcol_g = lax.broadcasted_iota(jnp.int32, (S5_HALF_CH, half_states), 1) // S5_STATE
    bmask = brow_g == bcol_g
    crow_g = lax.broadcasted_iota(jnp.int32, (half_states, S5_HALF_CH), 0) // S5_STATE
    ccol_g = lax.broadcasted_iota(jnp.int32, (half_states, S5_HALF_CH), 1) // S5_GROUP
    cmask = crow_g == ccol_g
    reps = half_states // S5_STATE
    for h in range(bd_ref.shape[0]):
        rows = slice(h * S5_HALF_CH, (h + 1) * S5_HALF_CH)
        for k, bb in enumerate((bb_re, bb_im)):
            tiled = jnp.dot(bb[rows, :].astype(BF16), spread, preferred_element_type=F32)
            bd_ref[h, :, k * half_states:(k + 1) * half_states] = jnp.where(bmask, tiled, 0.0).astype(BF16)
        for k, (ct, sign) in enumerate(((cre_t, 1.0), (cim_t, -1.0))):
            tiled = jnp.concatenate([ct[:, rows]] * reps, axis=0)
            cd_ref[h, k * half_states:(k + 1) * half_states, :] = jnp.where(cmask, sign * tiled, 0.0).astype(BF16)


def _s5_prep(a_re, a_im, log_dt, b_re, b_im, c_re, c_im):
    g, p = a_re.shape
    hch = b_re.shape[2]
    assert hch == S5_GROUP and p == S5_STATE
    width = g * hch
    halves = width // S5_HALF_CH
    half_states = (S5_HALF_CH // hch) * p
    rep = lambda a: jnp.repeat(a, hch, axis=0)
    ldt = log_dt.reshape(g, 1)
    bt = lambda b: jnp.transpose(b, (0, 2, 1)).reshape(width, p)
    ct = lambda c: jnp.transpose(c, (2, 0, 1)).reshape(p, width)
    lam, bd, cd = pl.pallas_call(
        _s5_prep_kernel,
        out_shape=(jax.ShapeDtypeStruct((2, g, p), F32),
                   jax.ShapeDtypeStruct((halves, S5_HALF_CH, 2 * half_states), BF16),
                   jax.ShapeDtypeStruct((halves, 2 * half_states, S5_HALF_CH), BF16)),
        name="s5_prep",
    )(a_re, a_im, ldt, rep(a_re), rep(a_im), rep(ldt), bt(b_re), bt(b_im), ct(c_re), ct(c_im))
    return lam.reshape(2, g * p), bd, cd


def _gate_prep_kernel(wr_ref, wi_ref, or_ref, oi_ref):
    hd = wr_ref.shape[1]
    per_tile = GATE_TILE // hd
    erow = lax.broadcasted_iota(jnp.int32, (hd, GATE_TILE), 0)
    ecol = lax.broadcasted_iota(jnp.int32, (hd, GATE_TILE), 1)
    for w_ref, o_ref in ((wr_ref, or_ref), (wi_ref, oi_ref)):
        for a in range(o_ref.shape[0]):
            for b in range(per_tile):
                place = jnp.where(ecol == erow + b * hd, 1.0, 0.0).astype(BF16)
                o_ref[a, b * hd:(b + 1) * hd, :] = jnp.dot(
                    w_ref[a * per_tile + b].astype(BF16), place, preferred_element_type=F32).astype(BF16)


def _gate_prep(w_r, w_i):
    heads, hd, _ = w_r.shape
    tiles = heads * hd // GATE_TILE
    shp = jax.ShapeDtypeStruct((tiles, GATE_TILE, GATE_TILE), BF16)
    return pl.pallas_call(_gate_prep_kernel, out_shape=(shp, shp), name="gate_prep")(w_r, w_i)


def _ffn_kernel(x_ref, xn_ref, mod_ref, modn_ref, g_ref, wup_ref, wdn_ref, fg_ref, o_ref, u_ref, h0_ref, *,
                mod_base, final_norm):
    d = x_ref.shape[1]
    f = wdn_ref.shape[0]
    c0 = h0_ref.shape[1]

    def modulated(xv, m_ref):
        sh = m_ref[:, (mod_base + 0) * d:(mod_base + 1) * d]
        sc = m_ref[:, (mod_base + 1) * d:(mod_base + 2) * d]
        return (_rms_norm(xv, g_ref[...]) * (1.0 + sc) + sh).astype(BF16)

    @pl.when(pl.program_id(0) == 0)
    def _():
        u0 = modulated(x_ref[...], mod_ref)
        u_ref[...] = u0
        h0_ref[...] = jnp.dot(u0, wup_ref[:, :c0], preferred_element_type=F32)

    x = x_ref[...]
    gt = mod_ref[:, (mod_base + 2) * d:(mod_base + 3) * d]
    h0 = h0_ref[...]
    h_rest = jnp.dot(u_ref[...], wup_ref[:, c0:], preferred_element_type=F32)
    un = modulated(xn_ref[...], modn_ref)
    u_ref[...] = un
    h0_ref[...] = jnp.dot(un, wup_ref[:, :c0], preferred_element_type=F32)
    a = jnp.concatenate([h0, h_rest[:, :f - c0]], axis=1)
    b = h_rest[:, f - c0:]
    act = (a * _sigmoid(a)) * b
    y = jnp.dot(act.astype(BF16), wdn_ref[...], preferred_element_type=F32)
    out = x + (0.5 * gt) * y
    if final_norm:
        out = _rms_norm(out, fg_ref[...])
    o_ref[...] = out


def _ffn(x2d, mod3, norm_g, w_up, w_down, final_g, *, seq, mod_base, final_norm):
    t, d = x2d.shape
    f = w_down.shape[0]
    tm = FFN_TM
    tiles_per_seq = seq // tm
    n_tiles = t // tm
    nxt = lambda i: jnp.minimum(i + 1, n_tiles - 1)
    kern = functools.partial(_ffn_kernel, mod_base=mod_base, final_norm=final_norm)
    return pl.pallas_call(
        kern,
        grid=(n_tiles,),
        in_specs=[pl.BlockSpec((tm, d), lambda i: (i, 0)),
                  pl.BlockSpec((tm, d), lambda i: (nxt(i), 0)),
                  pl.BlockSpec((None, 1, mod3.shape[2]), lambda i: (i // tiles_per_seq, 0, 0)),
                  pl.BlockSpec((None, 1, mod3.shape[2]), lambda i: (nxt(i) // tiles_per_seq, 0, 0)),
                  _const_spec((1, d)),
                  _const_spec((d, 2 * f)),
                  _const_spec((f, d)),
                  _const_spec((1, d))],
        out_specs=pl.BlockSpec((tm, d), lambda i: (i, 0)),
        out_shape=jax.ShapeDtypeStruct((t, d), F32),
        scratch_shapes=[pltpu.VMEM((tm, d), BF16),
                        pltpu.VMEM((tm, FFN_HEAD_COLS), F32)],
        compiler_params=pltpu.CompilerParams(dimension_semantics=("arbitrary",),
                                             vmem_limit_bytes=VMEM_LIMIT),
        name="ffn_final" if final_norm else "ffn",
    )(x2d, x2d, mod3, mod3, norm_g.reshape(1, d), w_up, w_down, final_g.reshape(1, d))


def _lru_scan(a_ref, u_ref, carry_ref):
    ts = u_ref.shape[0]
    h = carry_ref[...]
    for t in range(ts // SUBLANES):
        sl = slice(t * SUBLANES, (t + 1) * SUBLANES)
        h = a_ref[sl, :] * h + u_ref[sl, :]
        u_ref[sl, :] = h
    carry_ref[...] = h


def _s5_scan(u_ref, lam_ref, carry_ref, halves):
    ts = u_ref.shape[0]
    half_states = lam_ref.shape[1] // halves
    w = S5_LANE_CHUNK
    for h in range(halves):
        for j in range(half_states // w):
            cre = slice(2 * half_states * h + j * w, 2 * half_states * h + (j + 1) * w)
            cim = slice(2 * half_states * h + half_states + j * w, 2 * half_states * h + half_states + (j + 1) * w)
            lcs = slice(half_states * h + j * w, half_states * h + (j + 1) * w)
            lr = jnp.broadcast_to(lam_ref[0:1, lcs], (SUBLANES, w))
            li = jnp.broadcast_to(lam_ref[1:2, lcs], (SUBLANES, w))
            hr = carry_ref[:, cre]
            hi = carry_ref[:, cim]
            for t in range(ts // SUBLANES):
                sl = slice(t * SUBLANES, (t + 1) * SUBLANES)
                hr, hi = (u_ref[sl, cre] + (lr * hr - li * hi),
                          u_ref[sl, cim] + (lr * hi + li * hr))
                u_ref[sl, cre] = hr
                u_ref[sl, cim] = hi
            carry_ref[:, cre] = hr
            carry_ref[:, cim] = hi


def _mixer_kernel(x_ref, mod_ref, g_ref, win_ref, bin_ref, cw_ref, cb_ref,
                  wr_ref, br_ref, wi_ref, bi_ref, lamp_ref, pa_ref,
                  bd_ref, cd_ref, lam_ref, sd_ref, gw_ref, gb_ref, pb_ref, wo_ref,
                  o_ref,
                  tm_ref, xpad_ref, a_ref, u_ref, s5_ref, hl_ref, hs_ref):
    nb, tt, d = x_ref.shape
    ts = nb * tt
    lane_chunks = tm_ref.shape[0]
    lanes = tm_ref.shape[2]
    w5 = sd_ref.shape[1]
    halves = bd_ref.shape[0]
    hs2 = bd_ref.shape[2]
    pad = (CONV_WIDTH - 1) * nb

    @pl.when(pl.program_id(0) == 0)
    def _():
        xpad_ref[0:pad, :] = jnp.zeros((pad, d), F32)
        hl_ref[...] = jnp.zeros_like(hl_ref)
        hs_ref[...] = jnp.zeros_like(hs_ref)

    def per_batch(v):
        return v[None, :, :]

    for b in range(nb):
        for c in range(lane_chunks):
            tm_ref[c, pl.ds(b, tt, stride=nb), :] = x_ref[b, :, c * lanes:(c + 1) * lanes]
    x = jnp.concatenate([tm_ref[c] for c in range(lane_chunks)], axis=1)
    x3 = x.reshape(tt, nb, d)
    sh = per_batch(mod_ref[:, 3 * d:4 * d])
    sc = per_batch(mod_ref[:, 4 * d:5 * d])
    gt = per_batch(mod_ref[:, 5 * d:6 * d])
    u_bf = (_rms_norm(x3, g_ref[...]) * (1.0 + sc) + sh).reshape(ts, d).astype(BF16)

    def in_proj(lo, hi):
        return jnp.dot(u_bf, win_ref[:, lo:hi], preferred_element_type=F32) + bin_ref[:, lo:hi]

    xpad_ref[pad:pad + ts, :] = in_proj(0, d)
    xc = cb_ref[...]
    for k in range(CONV_WIDTH):
        xc = xc + xpad_ref[k * nb:k * nb + ts, :] * cw_ref[k:k + 1, :]
    xpad_ref[0:pad, :] = xpad_ref[ts:ts + pad, :]
    xc_bf = xc.astype(BF16)
    gates = []
    for w_ref, b_ref in ((wr_ref, br_ref), (wi_ref, bi_ref)):
        cols = [jnp.dot(xc_bf[:, j * GATE_TILE:(j + 1) * GATE_TILE], w_ref[j], preferred_element_type=F32)
                for j in range(w_ref.shape[0])]
        gates.append(_sigmoid(jnp.concatenate(cols, axis=1) + b_ref[...]))
    r_gate, i_gate = gates
    nl = -lamp_ref[...]
    softplus = jnp.maximum(nl, 0.0) + jnp.log1p(jnp.exp(-jnp.abs(nl)))
    log_a = (-LRU_C * softplus) * r_gate
    a_gate = jnp.exp(log_a)
    a_ref[...] = a_gate
    one_m_a2 = 1.0 - a_gate * a_gate
    mult = jnp.where(one_m_a2 > 0.0, one_m_a2 * lax.rsqrt(one_m_a2), 0.0)
    u_ref[...] = mult * (i_gate * xc)
    _lru_scan(a_ref, u_ref, hl_ref)
    ya = u_ref[...] * _gelu_tanh(in_proj(d, 2 * d))
    proj_a = jnp.dot(ya.astype(BF16), pa_ref[...], preferred_element_type=F32)

    xb = in_proj(2 * d, 2 * d + w5)
    xb_bf = xb.astype(BF16)
    for h in range(halves):
        s5_ref[:, h * hs2:(h + 1) * hs2] = jnp.dot(
            xb_bf[:, h * S5_HALF_CH:(h + 1) * S5_HALF_CH], bd_ref[h], preferred_element_type=F32)
    _s5_scan(s5_ref, lam_ref, hs_ref, halves)
    ys = [jnp.dot(s5_ref[:, h * hs2:(h + 1) * hs2].astype(BF16), cd_ref[h], preferred_element_type=F32)
          for h in range(halves)]
    yb = _gelu_tanh(jnp.concatenate(ys, axis=1) + sd_ref[...] * xb)
    yb = yb * _sigmoid(jnp.dot(yb.astype(BF16), gw_ref[...], preferred_element_type=F32) + gb_ref[...])
    proj_b = jnp.dot(yb.astype(BF16), pb_ref[...], preferred_element_type=F32)

    m = (_sigmoid(in_proj(2 * d + w5, 3 * d + w5)) * proj_a
         + _sigmoid(in_proj(3 * d + w5, 4 * d + w5)) * proj_b)
    y = jnp.dot(m.astype(BF16), wo_ref[...], preferred_element_type=F32)
    out = (x3 + gt * y.reshape(tt, nb, d)).reshape(ts, d)
    for c in range(lane_chunks):
        tm_ref[c] = out[:, c * lanes:(c + 1) * lanes]
    for b in range(nb):
        for c in range(lane_chunks):
            o_ref[b, :, c * lanes:(c + 1) * lanes] = tm_ref[c, pl.ds(b, tt, stride=nb), :]


def _mixer(x3d, mod2, norm_g, w_in, b_in, conv_w, conv_b, wr_bd, b_r, wi_bd, b_i, lam_p, proj_a,
           bd, cd, lam, s5_d, glu_w, glu_b, proj_b, w_out):
    bsz, seq, d = x3d.shape
    tt = MIX_TT
    ts = tt * bsz
    n_states2 = bd.shape[0] * bd.shape[2]
    row = lambda v: v.reshape(1, -1)
    consts = [mod2, row(norm_g), w_in, row(b_in), conv_w, row(conv_b), wr_bd, row(b_r), wi_bd, row(b_i),
              row(lam_p), proj_a, bd, cd, lam, row(s5_d), glu_w, row(glu_b), proj_b, w_out]
    blk = pl.BlockSpec((bsz, tt, d), lambda i: (0, i, 0))
    return pl.pallas_call(
        _mixer_kernel,
        grid=(seq // tt,),
        in_specs=[blk] + [_const_spec(c.shape) for c in consts],
        out_specs=blk,
        out_shape=jax.ShapeDtypeStruct((bsz, seq, d), F32),
        scratch_shapes=[pltpu.VMEM((d // LANES, ts, LANES), F32),
                        pltpu.VMEM((ts + (CONV_WIDTH - 1) * bsz, d), F32),
                        pltpu.VMEM((ts, d), F32),
                        pltpu.VMEM((ts, d), F32),
                        pltpu.VMEM((ts, n_states2), F32),
                        pltpu.VMEM((bsz, d), F32),
                        pltpu.VMEM((bsz, n_states2), F32)],
        compiler_params=pltpu.CompilerParams(dimension_semantics=("arbitrary",),
                                             vmem_limit_bytes=VMEM_LIMIT),
        name="mixer",
    )(x3d, *consts)


def kernel(x, c, mod_w, mod_b, norm1_g, ffn1_w_up, ffn1_w_down, norm2_g, w_in, b_in, conv_w, conv_b, lru_w_r, lru_b_r, lru_w_i, lru_b_i, lru_lambda, proj_a, s5_a_re, s5_a_im, s5_log_dt, s5_b_re, s5_b_im, s5_c_re, s5_c_im, s5_d, glu_w, glu_b, proj_b, w_out, norm3_g, ffn2_w_up, ffn2_w_down, final_g):
    bsz, seq, d = x.shape
    assert bsz == SUBLANES, "time-major layout puts the batch on the 8 sublanes"
    depth = mod_w.shape[0]
    bf = lambda w: w.astype(BF16)
    for l in range(depth):
        mod2 = _mod(c, mod_w[l], mod_b[l])
        mod3 = mod2.reshape(bsz, 1, N_MOD * d)
        lam, bd, cd = _s5_prep(s5_a_re[l], s5_a_im[l], s5_log_dt[l], s5_b_re[l], s5_b_im[l],
                               s5_c_re[l], s5_c_im[l])
        wr_bd, wi_bd = _gate_prep(lru_w_r[l], lru_w_i[l])
        last = l == depth - 1
        x2d = _ffn(x.reshape(bsz * seq, d), mod3, norm1_g[l], bf(ffn1_w_up[l]), bf(ffn1_w_down[l]), final_g,
                   seq=seq, mod_base=0, final_norm=False)
        x3d = _mixer(x2d.reshape(bsz, seq, d), mod2, norm2_g[l], bf(w_in[l]), b_in[l], conv_w[l], conv_b[l],
                     wr_bd, lru_b_r[l], wi_bd, lru_b_i[l], lru_lambda[l], bf(proj_a[l]),
                     bd, cd, lam, s5_d[l], bf(glu_w[l]), glu_b[l], bf(proj_b[l]), bf(w_out[l]))
        x = _ffn(x3d.reshape(bsz * seq, d), mod3, norm3_g[l], bf(ffn2_w_up[l]), bf(ffn2_w_down[l]), final_g,
                 seq=seq, mod_base=6, final_norm=last).reshape(bsz, seq, d)
    return x
```

```python
import functools
import math

import jax
import jax.numpy as jnp
from jax import lax
from jax.experimental import pallas as pl
from jax.experimental.pallas import tpu as pltpu

F32 = jnp.float32
BF16 = jnp.bfloat16

EPS = 1e-6
LRU_C = 8.0
CONV_WIDTH = 4
S5_GROUP = 16
S5_STATE = 64
N_MOD = 9

SUBLANES = 8
LANES = 128
GATE_TILE = 256
S5_HALF_CH = 256
VMEM_LIMIT = 56 * 1024 * 1024

WEIGHT_CHUNK_ROWS = 128
FFN_TM = 512
FFN_HEAD_COLS = 256
MIX_TT = 64
S5_LANE_CHUNK = 512


def _sigmoid(x):
    return 0.5 * jnp.tanh(0.5 * x) + 0.5


def _gelu_tanh(x):
    c = math.sqrt(2.0 / math.pi)
    return 0.5 * x * (1.0 + jnp.tanh(c * (x + 0.044715 * (x * x * x))))


def _rms_norm(x, g):
    ms = jnp.mean(x * x, axis=-1, keepdims=True)
    return x * lax.rsqrt(ms + EPS) * g


def _const_spec(shape):
    nd = len(shape)
    return pl.BlockSpec(shape, lambda *_: (0,) * nd, pipeline_mode=pl.Buffered(1))


def _mod_kernel(c_ref, w_ref, b_ref, o_ref):
    c = c_ref[...]
    ca = c * _sigmoid(c)
    o_ref[...] = jnp.dot(ca.astype(BF16), w_ref[...].astype(BF16),
                         preferred_element_type=F32) + b_ref[...]


def _mod(c, mod_w, mod_b):
    bsz, d = c.shape
    n = mod_w.shape[1]
    tn = d
    return pl.pallas_call(
        _mod_kernel,
        grid=(n // tn,),
        in_specs=[pl.BlockSpec((bsz, d), lambda j: (0, 0)),
                  pl.BlockSpec((d, tn), lambda j: (0, j)),
                  pl.BlockSpec((1, tn), lambda j: (0, j))],
        out_specs=pl.BlockSpec((bsz, tn), lambda j: (0, j)),
        out_shape=jax.ShapeDtypeStruct((bsz, n), F32),
        name="mod",
    )(c, mod_w, mod_b.reshape(1, n))


def _cmul(ar, ai, br, bi):
    return ar * br - ai * bi, ar * bi + ai * br


def _s5_discretise(a_re, a_im, log_dt):
    dt = jnp.exp(log_dt)
    mag = jnp.exp(a_re * dt)
    lr = mag * jnp.cos(a_im * dt)
    li = mag * jnp.sin(a_im * dt)
    den = a_re * a_re + a_im * a_im
    nr = lr - 1.0
    cr = (nr * a_re + li * a_im) / den
    ci = (li * a_re - nr * a_im) / den
    return lr, li, cr, ci


def _s5_prep_kernel(are_g, aim_g, ldt_g, are_r, aim_r, ldt_r, bre_t, bim_t, cre_t, cim_t,
                    lam_ref, bd_ref, cd_ref):
    lr, li, _, _ = _s5_discretise(are_g[...], aim_g[...], ldt_g[...])
    lam_ref[0] = lr
    lam_ref[1] = li

    _, _, cr, ci = _s5_discretise(are_r[...], aim_r[...], ldt_r[...])
    bb_re, bb_im = _cmul(cr, ci, bre_t[...], bim_t[...])
    half_states = (S5_HALF_CH // S5_GROUP) * S5_STATE
    erow = lax.broadcasted_iota(jnp.int32, (S5_STATE, half_states), 0)
    ecol = lax.broadcasted_iota(jnp.int32, (S5_STATE, half_states), 1)
    spread = jnp.where((ecol % S5_STATE) == erow, 1.0, 0.0).astype(BF16)
    brow_g = lax.broadcasted_iota(jnp.int32, (S5_HALF_CH, half_states), 0) // S5_GROUP
    bcol_g = lax.broadcasted_iota(jnp.int32, (S5_HALF_CH, half_states), 1) // S5_STATE
    bmask = brow_g == bcol_g
    crow_g = lax.broadcasted_iota(jnp.int32, (half_states, S5_HALF_CH), 0) // S5_STATE
    ccol_g = lax.broadcasted_iota(jnp.int32, (half_states, S5_HALF_CH), 1) // S5_GROUP
    cmask = crow_g == ccol_g
    reps = half_states // S5_STATE
    for h in range(bd_ref.shape[0]):
        rows = slice(h * S5_HALF_CH, (h + 1) * S5_HALF_CH)
        for k, bb in enumerate((bb_re, bb_im)):
            tiled = jnp.dot(bb[rows, :].astype(BF16), spread, preferred_element_type=F32)
            bd_ref[h, :, k * half_states:(k + 1) * half_states] = jnp.where(bmask, tiled, 0.0).astype(BF16)
        for k, (ct, sign) in enumerate(((cre_t, 1.0), (cim_t, -1.0))):
            tiled = jnp.concatenate([ct[:, rows]] * reps, axis=0)
            cd_ref[h, k * half_states:(k + 1) * half_states, :] = jnp.where(cmask, sign * tiled, 0.0).astype(BF16)


def _s5_prep(a_re, a_im, log_dt, b_re, b_im, c_re, c_im):
    g, p = a_re.shape
    hch = b_re.shape[2]
    assert hch == S5_GROUP and p == S5_STATE
    width = g * hch
    halves = width // S5_HALF_CH
    half_states = (S5_HALF_CH // hch) * p
    rep = lambda a: jnp.repeat(a, hch, axis=0)
    ldt = log_dt.reshape(g, 1)
    bt = lambda b: jnp.transpose(b, (0, 2, 1)).reshape(width, p)
    ct = lambda c: jnp.transpose(c, (2, 0, 1)).reshape(p, width)
    lam, bd, cd = pl.pallas_call(
        _s5_prep_kernel,
        out_shape=(jax.ShapeDtypeStruct((2, g, p), F32),
                   jax.ShapeDtypeStruct((halves, S5_HALF_CH, 2 * half_states), BF16),
                   jax.ShapeDtypeStruct((halves, 2 * half_states, S5_HALF_CH), BF16)),
        name="s5_prep",
    )(a_re, a_im, ldt, rep(a_re), rep(a_im), rep(ldt), bt(b_re), bt(b_im), ct(c_re), ct(c_im))
    return lam.reshape(2, g * p), bd, cd


def _gate_prep_kernel(wr_ref, wi_ref, or_ref, oi_ref):
    hd = wr_ref.shape[1]
    per_tile = GATE_TILE // hd
    erow = lax.broadcasted_iota(jnp.int32, (hd, GATE_TILE), 0)
    ecol = lax.broadcasted_iota(jnp.int32, (hd, GATE_TILE), 1)
    for w_ref, o_ref in ((wr_ref, or_ref), (wi_ref, oi_ref)):
        for a in range(o_ref.shape[0]):
            for b in range(per_tile):
                place = jnp.where(ecol == erow + b * hd, 1.0, 0.0).astype(BF16)
                o_ref[a, b * hd:(b + 1) * hd, :] = jnp.dot(
                    w_ref[a * per_tile + b].astype(BF16), place, preferred_element_type=F32).astype(BF16)


def _gate_prep(w_r, w_i):
    heads, hd, _ = w_r.shape
    tiles = heads * hd // GATE_TILE
    shp = jax.ShapeDtypeStruct((tiles, GATE_TILE, GATE_TILE), BF16)
    return pl.pallas_call(_gate_prep_kernel, out_shape=(shp, shp), name="gate_prep")(w_r, w_i)


def _stream_cast(pairs, stage_ref, sem_ref):
    rows = stage_ref.shape[1]
    chunks = [(src, dst, r0) for src, dst in pairs for r0 in range(0, src.shape[0], rows)]

    def copy(k):
        src, _, r0 = chunks[k]
        slot = k % 2
        return pltpu.make_async_copy(src.at[pl.ds(r0, rows), :],
                                     stage_ref.at[slot, :, pl.ds(0, src.shape[1])],
                                     sem_ref.at[slot])

    copy(0).start()
    for k, (src, dst, r0) in enumerate(chunks):
        if k + 1 < len(chunks):
            copy(k + 1).start()
        copy(k).wait()
        dst[r0:r0 + rows, :] = stage_ref[k % 2, :, 0:src.shape[1]].astype(BF16)


def _weight_scratch(weights):
    assert all(w.shape[0] % WEIGHT_CHUNK_ROWS == 0 for w in weights)
    widest = max(w.shape[1] for w in weights)
    return ([pltpu.VMEM(w.shape, BF16) for w in weights]
            + [pltpu.VMEM((2, WEIGHT_CHUNK_ROWS, widest), F32), pltpu.SemaphoreType.DMA((2,))])


def _ffn_kernel(x_ref, xn_ref, mod_ref, modn_ref, g_ref, fg_ref, wup_hbm, wdn_hbm, o_ref,
                wup_ref, wdn_ref, stage_ref, sem_ref, u_ref, h0_ref, *, mod_base, final_norm):
    d = x_ref.shape[1]
    f = wdn_ref.shape[0]
    c0 = h0_ref.shape[1]

    def modulated(xv, m_ref):
        sh = m_ref[:, (mod_base + 0) * d:(mod_base + 1) * d]
        sc = m_ref[:, (mod_base + 1) * d:(mod_base + 2) * d]
        return (_rms_norm(xv, g_ref[...]) * (1.0 + sc) + sh).astype(BF16)

    @pl.when(pl.program_id(0) == 0)
    def _():
        _stream_cast([(wup_hbm, wup_ref), (wdn_hbm, wdn_ref)], stage_ref, sem_ref)
        u0 = modulated(x_ref[...], mod_ref)
        u_ref[...] = u0
        h0_ref[...] = jnp.dot(u0, wup_ref[:, :c0], preferred_element_type=F32)

    x = x_ref[...]
    gt = mod_ref[:, (mod_base + 2) * d:(mod_base + 3) * d]
    h0 = h0_ref[...]
    h_rest = jnp.dot(u_ref[...], wup_ref[:, c0:], preferred_element_type=F32)
    un = modulated(xn_ref[...], modn_ref)
    u_ref[...] = un
    h0_ref[...] = jnp.dot(un, wup_ref[:, :c0], preferred_element_type=F32)
    a = jnp.concatenate([h0, h_rest[:, :f - c0]], axis=1)
    b = h_rest[:, f - c0:]
    act = (a * _sigmoid(a)) * b
    y = jnp.dot(act.astype(BF16), wdn_ref[...], preferred_element_type=F32)
    out = x + (0.5 * gt) * y
    if final_norm:
        out = _rms_norm(out, fg_ref[...])
    o_ref[...] = out


def _ffn(x2d, mod3, norm_g, w_up, w_down, final_g, *, seq, mod_base, final_norm):
    t, d = x2d.shape
    f = w_down.shape[0]
    tm = FFN_TM
    tiles_per_seq = seq // tm
    n_tiles = t // tm
    nxt = lambda i: jnp.minimum(i + 1, n_tiles - 1)
    kern = functools.partial(_ffn_kernel, mod_base=mod_base, final_norm=final_norm)
    return pl.pallas_call(
        kern,
        grid=(n_tiles,),
        in_specs=[pl.BlockSpec((tm, d), lambda i: (i, 0)),
                  pl.BlockSpec((tm, d), lambda i: (nxt(i), 0)),
                  pl.BlockSpec((None, 1, mod3.shape[2]), lambda i: (i // tiles_per_seq, 0, 0)),
                  pl.BlockSpec((None, 1, mod3.shape[2]), lambda i: (nxt(i) // tiles_per_seq, 0, 0)),
                  _const_spec((1, d)),
                  _const_spec((1, d)),
                  pl.BlockSpec(memory_space=pl.ANY),
                  pl.BlockSpec(memory_space=pl.ANY)],
        out_specs=pl.BlockSpec((tm, d), lambda i: (i, 0)),
        out_shape=jax.ShapeDtypeStruct((t, d), F32),
        scratch_shapes=_weight_scratch([w_up, w_down])
                       + [pltpu.VMEM((tm, d), BF16),
                          pltpu.VMEM((tm, FFN_HEAD_COLS), F32)],
        compiler_params=pltpu.CompilerParams(dimension_semantics=("arbitrary",),
                                             vmem_limit_bytes=VMEM_LIMIT),
        name="ffn_final" if final_norm else "ffn",
    )(x2d, x2d, mod3, mod3, norm_g.reshape(1, d), final_g.reshape(1, d), w_up, w_down)


def _lru_scan(a_ref, u_ref, carry_ref):
    ts = u_ref.shape[0]
    h = carry_ref[...]
    for t in range(ts // SUBLANES):
        sl = slice(t * SUBLANES, (t + 1) * SUBLANES)
        h = a_ref[sl, :] * h + u_ref[sl, :]
        u_ref[sl, :] = h
    carry_ref[...] = h


def _s5_scan(u_ref, lam_ref, carry_ref, halves):
    ts = u_ref.shape[0]
    half_states = lam_ref.shape[1] // halves
    w = S5_LANE_CHUNK
    for h in range(halves):
        for j in range(half_states // w):
            cre = slice(2 * half_states * h + j * w, 2 * half_states * h + (j + 1) * w)
            cim = slice(2 * half_states * h + half_states + j * w, 2 * half_states * h + half_states + (j + 1) * w)
            lcs = slice(half_states * h + j * w, half_states * h + (j + 1) * w)
            lr = jnp.broadcast_to(lam_ref[0:1, lcs], (SUBLANES, w))
            li = jnp.broadcast_to(lam_ref[1:2, lcs], (SUBLANES, w))
            hr = carry_ref[:, cre]
            hi = carry_ref[:, cim]
            for t in range(ts // SUBLANES):
                sl = slice(t * SUBLANES, (t + 1) * SUBLANES)
                hr, hi = (u_ref[sl, cre] + (lr * hr - li * hi),
                          u_ref[sl, cim] + (lr * hi + li * hr))
                u_ref[sl, cre] = hr
                u_ref[sl, cim] = hi
            carry_ref[:, cre] = hr
            carry_ref[:, cim] = hi


def _mixer_kernel(x_ref, mod_ref, g_ref, win_ref, bin_ref, cw_ref, cb_ref,
                  wr_ref, br_ref, wi_ref, bi_ref, lamp_ref, pa_ref,
                  bd_ref, cd_ref, lam_ref, sd_ref, gw_ref, gb_ref, pb_ref, wo_ref,
                  o_ref,
                  tm_ref, xpad_ref, a_ref, u_ref, s5_ref, hl_ref, hs_ref):
    nb, tt, d = x_ref.shape
    ts = nb * tt
    lane_chunks = tm_ref.shape[0]
    lanes = tm_ref.shape[2]
    w5 = sd_ref.shape[1]
    halves = bd_ref.shape[0]
    hs2 = bd_ref.shape[2]
    pad = (CONV_WIDTH - 1) * nb

    @pl.when(pl.program_id(0) == 0)
    def _():
        xpad_ref[0:pad, :] = jnp.zeros((pad, d), F32)
        hl_ref[...] = jnp.zeros_like(hl_ref)
        hs_ref[...] = jnp.zeros_like(hs_ref)

    def per_batch(v):
        return v[None, :, :]

    for b in range(nb):
        for c in range(lane_chunks):
            tm_ref[c, pl.ds(b, tt, stride=nb), :] = x_ref[b, :, c * lanes:(c + 1) * lanes]
    x = jnp.concatenate([tm_ref[c] for c in range(lane_chunks)], axis=1)
    x3 = x.reshape(tt, nb, d)
    sh = per_batch(mod_ref[:, 3 * d:4 * d])
    sc = per_batch(mod_ref[:, 4 * d:5 * d])
    gt = per_batch(mod_ref[:, 5 * d:6 * d])
    u_bf = (_rms_norm(x3, g_ref[...]) * (1.0 + sc) + sh).reshape(ts, d).astype(BF16)

    def in_proj(lo, hi):
        return jnp.dot(u_bf, win_ref[:, lo:hi], preferred_element_type=F32) + bin_ref[:, lo:hi]

    xpad_ref[pad:pad + ts, :] = in_proj(0, d)
    xb = in_proj(2 * d, 2 * d + w5)
    xb_bf = xb.astype(BF16)
    for h in range(halves):
        s5_ref[:, h * hs2:(h + 1) * hs2] = jnp.dot(
            xb_bf[:, h * S5_HALF_CH:(h + 1) * S5_HALF_CH], bd_ref[h], preferred_element_type=F32)
    ga_pre = in_proj(d, 2 * d)

    xc = cb_ref[...]
    for k in range(CONV_WIDTH):
        xc = xc + xpad_ref[k * nb:k * nb + ts, :] * cw_ref[k:k + 1, :]
    xpad_ref[0:pad, :] = xpad_ref[ts:ts + pad, :]
    xc_bf = xc.astype(BF16)
    gates = []
    for w_ref, b_ref in ((wr_ref, br_ref), (wi_ref, bi_ref)):
        cols = [jnp.dot(xc_bf[:, j * GATE_TILE:(j + 1) * GATE_TILE], w_ref[j], preferred_element_type=F32)
                for j in range(w_ref.shape[0])]
        gates.append(_sigmoid(jnp.concatenate(cols, axis=1) + b_ref[...]))
    r_gate, i_gate = gates
    mga_pre = in_proj(2 * d + w5, 3 * d + w5)
    nl = -lamp_ref[...]
    softplus = jnp.maximum(nl, 0.0) + jnp.log1p(jnp.exp(-jnp.abs(nl)))
    log_a = (-LRU_C * softplus) * r_gate
    a_gate = jnp.exp(log_a)
    a_ref[...] = a_gate
    one_m_a2 = 1.0 - a_gate * a_gate
    mult = jnp.where(one_m_a2 > 0.0, one_m_a2 * lax.rsqrt(one_m_a2), 0.0)
    u_ref[...] = mult * (i_gate * xc)
    _lru_scan(a_ref, u_ref, hl_ref)
    ya = u_ref[...] * _gelu_tanh(ga_pre)
    proj_a = jnp.dot(ya.astype(BF16), pa_ref[...], preferred_element_type=F32)
    mgb_pre = in_proj(3 * d + w5, 4 * d + w5)

    _s5_scan(s5_ref, lam_ref, hs_ref, halves)
    ys = [jnp.dot(s5_ref[:, h * hs2:(h + 1) * hs2].astype(BF16), cd_ref[h], preferred_element_type=F32)
          for h in range(halves)]
    yb = _gelu_tanh(jnp.concatenate(ys, axis=1) + sd_ref[...] * xb)
    yb = yb * _sigmoid(jnp.dot(yb.astype(BF16), gw_ref[...], preferred_element_type=F32) + gb_ref[...])
    proj_b = jnp.dot(yb.astype(BF16), pb_ref[...], preferred_element_type=F32)

    m = _sigmoid(mga_pre) * proj_a + _sigmoid(mgb_pre) * proj_b
    y = jnp.dot(m.astype(BF16), wo_ref[...], preferred_element_type=F32)
    out = (x3 + gt * y.reshape(tt, nb, d)).reshape(ts, d)
    for c in range(lane_chunks):
        tm_ref[c] = out[:, c * lanes:(c + 1) * lanes]
    for b in range(nb):
        for c in range(lane_chunks):
            o_ref[b, :, c * lanes:(c + 1) * lanes] = tm_ref[c, pl.ds(b, tt, stride=nb), :]


def _mixer(x3d, mod2, norm_g, w_in, b_in, conv_w, conv_b, wr_bd, b_r, wi_bd, b_i, lam_p, proj_a,
           bd, cd, lam, s5_d, glu_w, glu_b, proj_b, w_out):
    bsz, seq, d = x3d.shape
    tt = MIX_TT
    ts = tt * bsz
    n_states2 = bd.shape[0] * bd.shape[2]
    row = lambda v: v.reshape(1, -1)
    consts = [mod2, row(norm_g), w_in, row(b_in), conv_w, row(conv_b), wr_bd, row(b_r), wi_bd, row(b_i),
              row(lam_p), proj_a, bd, cd, lam, row(s5_d), glu_w, row(glu_b), proj_b, w_out]
    blk = pl.BlockSpec((bsz, tt, d), lambda i: (0, i, 0))
    return pl.pallas_call(
        _mixer_kernel,
        grid=(seq // tt,),
        in_specs=[blk] + [_const_spec(c.shape) for c in consts],
        out_specs=blk,
        out_shape=jax.ShapeDtypeStruct((bsz, seq, d), F32),
        scratch_shapes=[pltpu.VMEM((d // LANES, ts, LANES), F32),
                        pltpu.VMEM((ts + (CONV_WIDTH - 1) * bsz, d), F32),
                        pltpu.VMEM((ts, d), F32),
                        pltpu.VMEM((ts, d), F32),
                        pltpu.VMEM((ts, n_states2), F32),
                        pltpu.VMEM((bsz, d), F32),
                        pltpu.VMEM((bsz, n_states2), F32)],
        compiler_params=pltpu.CompilerParams(dimension_semantics=("arbitrary",),
                                             vmem_limit_bytes=VMEM_LIMIT),
        name="mixer",
    )(x3d, *consts)


def kernel(x, c, mod_w, mod_b, norm1_g, ffn1_w_up, ffn1_w_down, norm2_g, w_in, b_in, conv_w, conv_b, lru_w_r, lru_b_r, lru_w_i, lru_b_i, lru_lambda, proj_a, s5_a_re, s5_a_im, s5_log_dt, s5_b_re, s5_b_im, s5_c_re, s5_c_im, s5_d, glu_w, glu_b, proj_b, w_out, norm3_g, ffn2_w_up, ffn2_w_down, final_g):
    bsz, seq, d = x.shape
    assert bsz == SUBLANES, "time-major layout puts the batch on the 8 sublanes"
    depth = mod_w.shape[0]
    bf = lambda w: w.astype(BF16)
    for l in range(depth):
        mod2 = _mod(c, mod_w[l], mod_b[l])
        mod3 = mod2.reshape(bsz, 1, N_MOD * d)
        lam, bd, cd = _s5_prep(s5_a_re[l], s5_a_im[l], s5_log_dt[l], s5_b_re[l], s5_b_im[l],
                               s5_c_re[l], s5_c_im[l])
        wr_bd, wi_bd = _gate_prep(lru_w_r[l], lru_w_i[l])
        last = l == depth - 1
        x2d = _ffn(x.reshape(bsz * seq, d), mod3, norm1_g[l], ffn1_w_up[l], ffn1_w_down[l], final_g,
                   seq=seq, mod_base=0, final_norm=False)
        x3d = _mixer(x2d.reshape(bsz, seq, d), mod2, norm2_g[l], bf(w_in[l]), b_in[l], conv_w[l], conv_b[l],
                     wr_bd, lru_b_r[l], wi_bd, lru_b_i[l], lru_lambda[l], bf(proj_a[l]),
                     bd, cd, lam, s5_d[l], bf(glu_w[l]), glu_b[l], bf(proj_b[l]), bf(w_out[l]))
        x = _ffn(x3d.reshape(bsz * seq, d), mod3, norm3_g[l], ffn2_w_up[l], ffn2_w_down[l], final_g,
                 seq=seq, mod_base=6, final_norm=last).reshape(bsz, seq, d)
    return x
```

```python
import functools
import math

import jax
import jax.numpy as jnp
from jax import lax
from jax.experimental import pallas as pl
from jax.experimental.pallas import tpu as pltpu

F32 = jnp.float32
BF16 = jnp.bfloat16

EPS = 1e-6
LRU_C = 8.0
CONV_WIDTH = 4
S5_GROUP = 16
S5_STATE = 64
N_MOD = 9

SUBLANES = 8
LANES = 128
GATE_TILE = 256
S5_HALF_CH = 256
VMEM_LIMIT = 56 * 1024 * 1024

WEIGHT_STAGE_SLOTS = 4
FFN_TM = 512
FFN_HEAD_COLS = 256
MIX_TT = 64
S5_LANE_CHUNK = 512


def _sigmoid(x):
    return 0.5 * jnp.tanh(0.5 * x) + 0.5


def _gelu_tanh(x):
    c = math.sqrt(2.0 / math.pi)
    return 0.5 * x * (1.0 + jnp.tanh(c * (x + 0.044715 * (x * x * x))))


def _rms_norm(x, g):
    ms = jnp.mean(x * x, axis=-1, keepdims=True)
    return x * lax.rsqrt(ms + EPS) * g


def _const_spec(shape):
    nd = len(shape)
    return pl.BlockSpec(shape, lambda *_: (0,) * nd, pipeline_mode=pl.Buffered(1))


def _mod_kernel(c_ref, w_ref, b_ref, o_ref):
    c = c_ref[...]
    ca = c * _sigmoid(c)
    o_ref[...] = jnp.dot(ca.astype(BF16), w_ref[...].astype(BF16),
                         preferred_element_type=F32) + b_ref[...]


def _mod(c, mod_w, mod_b):
    bsz, d = c.shape
    n = mod_w.shape[1]
    tn = d
    return pl.pallas_call(
        _mod_kernel,
        grid=(n // tn,),
        in_specs=[pl.BlockSpec((bsz, d), lambda j: (0, 0)),
                  pl.BlockSpec((d, tn), lambda j: (0, j)),
                  pl.BlockSpec((1, tn), lambda j: (0, j))],
        out_specs=pl.BlockSpec((bsz, tn), lambda j: (0, j)),
        out_shape=jax.ShapeDtypeStruct((bsz, n), F32),
        name="mod",
    )(c, mod_w, mod_b.reshape(1, n))


def _cmul(ar, ai, br, bi):
    return ar * br - ai * bi, ar * bi + ai * br


def _s5_discretise(a_re, a_im, log_dt):
    dt = jnp.exp(log_dt)
    mag = jnp.exp(a_re * dt)
    lr = mag * jnp.cos(a_im * dt)
    li = mag * jnp.sin(a_im * dt)
    den = a_re * a_re + a_im * a_im
    nr = lr - 1.0
    cr = (nr * a_re + li * a_im) / den
    ci = (li * a_re - nr * a_im) / den
    return lr, li, cr, ci


def _s5_prep_kernel(are_g, aim_g, ldt_g, are_r, aim_r, ldt_r, bre_t, bim_t, cre_t, cim_t,
                    lam_ref, bd_ref, cd_ref):
    lr, li, _, _ = _s5_discretise(are_g[...], aim_g[...], ldt_g[...])
    lam_ref[0] = lr
    lam_ref[1] = li

    _, _, cr, ci = _s5_discretise(are_r[...], aim_r[...], ldt_r[...])
    bb_re, bb_im = _cmul(cr, ci, bre_t[...], bim_t[...])
    half_states = (S5_HALF_CH // S5_GROUP) * S5_STATE
    erow = lax.broadcasted_iota(jnp.int32, (S5_STATE, half_states), 0)
    ecol = lax.broadcasted_iota(jnp.int32, (S5_STATE, half_states), 1)
    spread = jnp.where((ecol % S5_STATE) == erow, 1.0, 0.0).astype(BF16)
    brow_g = lax.broadcasted_iota(jnp.int32, (S5_HALF_CH, half_states), 0) // S5_GROUP
    bcol_g = lax.broadcasted_iota(jnp.int32, (S5_HALF_CH, half_states), 1) // S5_STATE
    bmask = brow_g == bcol_g
    crow_g = lax.broadcasted_iota(jnp.int32, (half_states, S5_HALF_CH), 0) // S5_STATE
    ccol_g = lax.broadcasted_iota(jnp.int32, (half_states, S5_HALF_CH), 1) // S5_GROUP
    cmask = crow_g == ccol_g
    reps = half_states // S5_STATE
    for h in range(bd_ref.shape[0]):
        rows = slice(h * S5_HALF_CH, (h + 1) * S5_HALF_CH)
        for k, bb in enumerate((bb_re, bb_im)):
            tiled = jnp.dot(bb[rows, :].astype(BF16), spread, preferred_element_type=F32)
            bd_ref[h, :, k * half_states:(k + 1) * half_states] = jnp.where(bmask, tiled, 0.0).astype(BF16)
        for k, (ct, sign) in enumerate(((cre_t, 1.0), (cim_t, -1.0))):
            tiled = jnp.concatenate([ct[:, rows]] * reps, axis=0)
            cd_ref[h, k * half_states:(k + 1) * half_states, :] = jnp.where(cmask, sign * tiled, 0.0).astype(BF16)


def _s5_prep(a_re, a_im, log_dt, b_re, b_im, c_re, c_im):
    g, p = a_re.shape
    hch = b_re.shape[2]
    assert hch == S5_GROUP and p == S5_STATE
    width = g * hch
    halves = width // S5_HALF_CH
    half_states = (S5_HALF_CH // hch) * p
    rep = lambda a: jnp.repeat(a, hch, axis=0)
    ldt = log_dt.reshape(g, 1)
    bt = lambda b: jnp.transpose(b, (0, 2, 1)).reshape(width, p)
    ct = lambda c: jnp.transpose(c, (2, 0, 1)).reshape(p, width)
    lam, bd, cd = pl.pallas_call(
        _s5_prep_kernel,
        out_shape=(jax.ShapeDtypeStruct((2, g, p), F32),
                   jax.ShapeDtypeStruct((halves, S5_HALF_CH, 2 * half_states), BF16),
                   jax.ShapeDtypeStruct((halves, 2 * half_states, S5_HALF_CH), BF16)),
        name="s5_prep",
    )(a_re, a_im, ldt, rep(a_re), rep(a_im), rep(ldt), bt(b_re), bt(b_im), ct(c_re), ct(c_im))
    return lam.reshape(2, g * p), bd, cd


def _gate_prep_kernel(wr_ref, wi_ref, or_ref, oi_ref):
    hd = wr_ref.shape[1]
    per_tile = GATE_TILE // hd
    erow = lax.broadcasted_iota(jnp.int32, (hd, GATE_TILE), 0)
    ecol = lax.broadcasted_iota(jnp.int32, (hd, GATE_TILE), 1)
    for w_ref, o_ref in ((wr_ref, or_ref), (wi_ref, oi_ref)):
        for a in range(o_ref.shape[0]):
            for b in range(per_tile):
                place = jnp.where(ecol == erow + b * hd, 1.0, 0.0).astype(BF16)
                o_ref[a, b * hd:(b + 1) * hd, :] = jnp.dot(
                    w_ref[a * per_tile + b].astype(BF16), place, preferred_element_type=F32).astype(BF16)


def _gate_prep(w_r, w_i):
    heads, hd, _ = w_r.shape
    tiles = heads * hd // GATE_TILE
    shp = jax.ShapeDtypeStruct((tiles, GATE_TILE, GATE_TILE), BF16)
    return pl.pallas_call(_gate_prep_kernel, out_shape=(shp, shp), name="gate_prep")(w_r, w_i)


def _stream_cast(mats, stages, sem_ref):
    chunks = []
    used = [0] * len(stages)
    for src, dst, si in mats:
        rows = stages[si].shape[1]
        assert src.shape[0] % rows == 0 and src.shape[1] <= stages[si].shape[2]
        for r0 in range(0, src.shape[0], rows):
            slot = used[si] % WEIGHT_STAGE_SLOTS
            used[si] += 1
            chunks.append((src, dst, si, r0, rows, slot))

    def copy(k):
        src, _, si, r0, rows, slot = chunks[k]
        return pltpu.make_async_copy(src.at[pl.ds(r0, rows), :],
                                     stages[si].at[slot, :, pl.ds(0, src.shape[1])],
                                     sem_ref.at[si * WEIGHT_STAGE_SLOTS + slot])

    ahead = WEIGHT_STAGE_SLOTS - 1
    for k in range(min(ahead, len(chunks))):
        copy(k).start()
    for k, (src, dst, si, r0, rows, slot) in enumerate(chunks):
        if k + ahead < len(chunks):
            copy(k + ahead).start()
        copy(k).wait()
        dst[r0:r0 + rows, :] = stages[si][slot, :, 0:src.shape[1]].astype(BF16)


def _weight_scratch(weights, stage_shapes):
    return ([pltpu.VMEM(w.shape, BF16) for w in weights]
            + [pltpu.VMEM((WEIGHT_STAGE_SLOTS,) + tuple(s), F32) for s in stage_shapes]
            + [pltpu.SemaphoreType.DMA((WEIGHT_STAGE_SLOTS * len(stage_shapes),))])


def _ffn_kernel(x_ref, xn_ref, mod_ref, modn_ref, g_ref, fg_ref, wup_hbm, wdn_hbm, o_ref,
                wup_ref, wdn_ref, stage_up_ref, stage_dn_ref, sem_ref, u_ref, h0_ref, *, mod_base, final_norm):
    d = x_ref.shape[1]
    f = wdn_ref.shape[0]
    c0 = h0_ref.shape[1]

    def modulated(xv, m_ref):
        sh = m_ref[:, (mod_base + 0) * d:(mod_base + 1) * d]
        sc = m_ref[:, (mod_base + 1) * d:(mod_base + 2) * d]
        return (_rms_norm(xv, g_ref[...]) * (1.0 + sc) + sh).astype(BF16)

    @pl.when(pl.program_id(0) == 0)
    def _():
        _stream_cast([(wup_hbm, wup_ref, 0), (wdn_hbm, wdn_ref, 1)], [stage_up_ref, stage_dn_ref], sem_ref)
        u0 = modulated(x_ref[...], mod_ref)
        u_ref[...] = u0
        h0_ref[...] = jnp.dot(u0, wup_ref[:, :c0], preferred_element_type=F32)

    x = x_ref[...]
    gt = mod_ref[:, (mod_base + 2) * d:(mod_base + 3) * d]
    h0 = h0_ref[...]
    h_rest = jnp.dot(u_ref[...], wup_ref[:, c0:], preferred_element_type=F32)
    un = modulated(xn_ref[...], modn_ref)
    u_ref[...] = un
    h0_ref[...] = jnp.dot(un, wup_ref[:, :c0], preferred_element_type=F32)
    a = jnp.concatenate([h0, h_rest[:, :f - c0]], axis=1)
    b = h_rest[:, f - c0:]
    act = (a * _sigmoid(a)) * b
    y = jnp.dot(act.astype(BF16), wdn_ref[...], preferred_element_type=F32)
    out = x + (0.5 * gt) * y
    if final_norm:
        out = _rms_norm(out, fg_ref[...])
    o_ref[...] = out


def _ffn(x2d, mod3, norm_g, w_up, w_down, final_g, *, seq, mod_base, final_norm):
    t, d = x2d.shape
    f = w_down.shape[0]
    tm = FFN_TM
    tiles_per_seq = seq // tm
    n_tiles = t // tm
    nxt = lambda i: jnp.minimum(i + 1, n_tiles - 1)
    kern = functools.partial(_ffn_kernel, mod_base=mod_base, final_norm=final_norm)
    return pl.pallas_call(
        kern,
        grid=(n_tiles,),
        in_specs=[pl.BlockSpec((tm, d), lambda i: (i, 0)),
                  pl.BlockSpec((tm, d), lambda i: (nxt(i), 0)),
                  pl.BlockSpec((None, 1, mod3.shape[2]), lambda i: (i // tiles_per_seq, 0, 0)),
                  pl.BlockSpec((None, 1, mod3.shape[2]), lambda i: (nxt(i) // tiles_per_seq, 0, 0)),
                  _const_spec((1, d)),
                  _const_spec((1, d)),
                  pl.BlockSpec(memory_space=pl.ANY),
                  pl.BlockSpec(memory_space=pl.ANY)],
        out_specs=pl.BlockSpec((tm, d), lambda i: (i, 0)),
        out_shape=jax.ShapeDtypeStruct((t, d), F32),
        scratch_shapes=_weight_scratch([w_up, w_down], [(d // 16, 2 * f), (f // 8, d)])
                       + [pltpu.VMEM((tm, d), BF16),
                          pltpu.VMEM((tm, FFN_HEAD_COLS), F32)],
        compiler_params=pltpu.CompilerParams(dimension_semantics=("arbitrary",),
                                             vmem_limit_bytes=VMEM_LIMIT),
        name="ffn_final" if final_norm else "ffn",
    )(x2d, x2d, mod3, mod3, norm_g.reshape(1, d), final_g.reshape(1, d), w_up, w_down)


def _lru_scan(a_ref, u_ref, carry_ref):
    ts = u_ref.shape[0]
    h = carry_ref[...]
    for t in range(ts // SUBLANES):
        sl = slice(t * SUBLANES, (t + 1) * SUBLANES)
        h = a_ref[sl, :] * h + u_ref[sl, :]
        u_ref[sl, :] = h
    carry_ref[...] = h


def _s5_scan(u_ref, lam_ref, carry_ref, halves):
    ts = u_ref.shape[0]
    half_states = lam_ref.shape[1] // halves
    w = S5_LANE_CHUNK
    for h in range(halves):
        for j in range(half_states // w):
            cre = slice(2 * half_states * h + j * w, 2 * half_states * h + (j + 1) * w)
            cim = slice(2 * half_states * h + half_states + j * w, 2 * half_states * h + half_states + (j + 1) * w)
            lcs = slice(half_states * h + j * w, half_states * h + (j + 1) * w)
            lr = jnp.broadcast_to(lam_ref[0:1, lcs], (SUBLANES, w))
            li = jnp.broadcast_to(lam_ref[1:2, lcs], (SUBLANES, w))
            hr = carry_ref[:, cre]
            hi = carry_ref[:, cim]
            for t in range(ts // SUBLANES):
                sl = slice(t * SUBLANES, (t + 1) * SUBLANES)
                hr, hi = (u_ref[sl, cre] + (lr * hr - li * hi),
                          u_ref[sl, cim] + (lr * hi + li * hr))
                u_ref[sl, cre] = hr
                u_ref[sl, cim] = hi
            carry_ref[:, cre] = hr
            carry_ref[:, cim] = hi


def _mixer_kernel(x_ref, mod_ref, g_ref, bin_ref, cw_ref, cb_ref,
                  wr_ref, br_ref, wi_ref, bi_ref, lamp_ref,
                  bd_ref, cd_ref, lam_ref, sd_ref, gb_ref,
                  win_hbm, pa_hbm, gw_hbm, pb_hbm, wo_hbm,
                  o_ref,
                  win_ref, pa_ref, gw_ref, pb_ref, wo_ref, stage_in_ref, stage_sq_ref, sem_ref,
                  tm_ref, xpad_ref, a_ref, u_ref, s5_ref, hl_ref, hs_ref):
    nb, tt, d = x_ref.shape
    ts = nb * tt
    lane_chunks = tm_ref.shape[0]
    lanes = tm_ref.shape[2]
    w5 = sd_ref.shape[1]
    halves = bd_ref.shape[0]
    hs2 = bd_ref.shape[2]
    pad = (CONV_WIDTH - 1) * nb

    @pl.when(pl.program_id(0) == 0)
    def _():
        xpad_ref[0:pad, :] = jnp.zeros((pad, d), F32)
        hl_ref[...] = jnp.zeros_like(hl_ref)
        hs_ref[...] = jnp.zeros_like(hs_ref)
        _stream_cast([(win_hbm, win_ref, 0), (pa_hbm, pa_ref, 1), (gw_hbm, gw_ref, 1), (pb_hbm, pb_ref, 1),
                      (wo_hbm, wo_ref, 1)], [stage_in_ref, stage_sq_ref], sem_ref)

    def per_batch(v):
        return v[None, :, :]

    for b in range(nb):
        for c in range(lane_chunks):
            tm_ref[c, pl.ds(b, tt, stride=nb), :] = x_ref[b, :, c * lanes:(c + 1) * lanes]
    x = jnp.concatenate([tm_ref[c] for c in range(lane_chunks)], axis=1)
    x3 = x.reshape(tt, nb, d)
    sh = per_batch(mod_ref[:, 3 * d:4 * d])
    sc = per_batch(mod_ref[:, 4 * d:5 * d])
    gt = per_batch(mod_ref[:, 5 * d:6 * d])
    u_bf = (_rms_norm(x3, g_ref[...]) * (1.0 + sc) + sh).reshape(ts, d).astype(BF16)

    def in_proj(lo, hi):
        return jnp.dot(u_bf, win_ref[:, lo:hi], preferred_element_type=F32) + bin_ref[:, lo:hi]

    xpad_ref[pad:pad + ts, :] = in_proj(0, d)
    xb = in_proj(2 * d, 2 * d + w5)
    xb_bf = xb.astype(BF16)
    for h in range(halves):
        s5_ref[:, h * hs2:(h + 1) * hs2] = jnp.dot(
            xb_bf[:, h * S5_HALF_CH:(h + 1) * S5_HALF_CH], bd_ref[h], preferred_element_type=F32)
    ga_pre = in_proj(d, 2 * d)

    xc = cb_ref[...]
    for k in range(CONV_WIDTH):
        xc = xc + xpad_ref[k * nb:k * nb + ts, :] * cw_ref[k:k + 1, :]
    xpad_ref[0:pad, :] = xpad_ref[ts:ts + pad, :]
    xc_bf = xc.astype(BF16)
    gates = []
    for w_ref, b_ref in ((wr_ref, br_ref), (wi_ref, bi_ref)):
        cols = [jnp.dot(xc_bf[:, j * GATE_TILE:(j + 1) * GATE_TILE], w_ref[j], preferred_element_type=F32)
                for j in range(w_ref.shape[0])]
        gates.append(_sigmoid(jnp.concatenate(cols, axis=1) + b_ref[...]))
    r_gate, i_gate = gates
    mga_pre = in_proj(2 * d + w5, 3 * d + w5)
    nl = -lamp_ref[...]
    softplus = jnp.maximum(nl, 0.0) + jnp.log1p(jnp.exp(-jnp.abs(nl)))
    log_a = (-LRU_C * softplus) * r_gate
    a_gate = jnp.exp(log_a)
    a_ref[...] = a_gate
    one_m_a2 = 1.0 - a_gate * a_gate
    mult = jnp.where(one_m_a2 > 0.0, one_m_a2 * lax.rsqrt(one_m_a2), 0.0)
    u_ref[...] = mult * (i_gate * xc)
    _lru_scan(a_ref, u_ref, hl_ref)
    ya = u_ref[...] * _gelu_tanh(ga_pre)
    proj_a = jnp.dot(ya.astype(BF16), pa_ref[...], preferred_element_type=F32)
    mgb_pre = in_proj(3 * d + w5, 4 * d + w5)

    _s5_scan(s5_ref, lam_ref, hs_ref, halves)
    ys = [jnp.dot(s5_ref[:, h * hs2:(h + 1) * hs2].astype(BF16), cd_ref[h], preferred_element_type=F32)
          for h in range(halves)]
    yb = _gelu_tanh(jnp.concatenate(ys, axis=1) + sd_ref[...] * xb)
    yb = yb * _sigmoid(jnp.dot(yb.astype(BF16), gw_ref[...], preferred_element_type=F32) + gb_ref[...])
    proj_b = jnp.dot(yb.astype(BF16), pb_ref[...], preferred_element_type=F32)

    m = _sigmoid(mga_pre) * proj_a + _sigmoid(mgb_pre) * proj_b
    y = jnp.dot(m.astype(BF16), wo_ref[...], preferred_element_type=F32)
    out = (x3 + gt * y.reshape(tt, nb, d)).reshape(ts, d)
    for c in range(lane_chunks):
        tm_ref[c] = out[:, c * lanes:(c + 1) * lanes]
    for b in range(nb):
        for c in range(lane_chunks):
            o_ref[b, :, c * lanes:(c + 1) * lanes] = tm_ref[c, pl.ds(b, tt, stride=nb), :]


def _mixer(x3d, mod2, norm_g, w_in, b_in, conv_w, conv_b, wr_bd, b_r, wi_bd, b_i, lam_p, proj_a,
           bd, cd, lam, s5_d, glu_w, glu_b, proj_b, w_out):
    bsz, seq, d = x3d.shape
    tt = MIX_TT
    ts = tt * bsz
    n_states2 = bd.shape[0] * bd.shape[2]
    row = lambda v: v.reshape(1, -1)
    consts = [mod2, row(norm_g), row(b_in), conv_w, row(conv_b), wr_bd, row(b_r), wi_bd, row(b_i),
              row(lam_p), bd, cd, lam, row(s5_d), row(glu_b)]
    streamed = [w_in, proj_a, glu_w, proj_b, w_out]
    blk = pl.BlockSpec((bsz, tt, d), lambda i: (0, i, 0))
    return pl.pallas_call(
        _mixer_kernel,
        grid=(seq // tt,),
        in_specs=[blk] + [_const_spec(c.shape) for c in consts]
                 + [pl.BlockSpec(memory_space=pl.ANY)] * len(streamed),
        out_specs=blk,
        out_shape=jax.ShapeDtypeStruct((bsz, seq, d), F32),
        scratch_shapes=_weight_scratch(streamed, [(d // 32, w_in.shape[1]), (d // 8, d)])
                       + [pltpu.VMEM((d // LANES, ts, LANES), F32),
                        pltpu.VMEM((ts + (CONV_WIDTH - 1) * bsz, d), F32),
                        pltpu.VMEM((ts, d), F32),
                        pltpu.VMEM((ts, d), F32),
                        pltpu.VMEM((ts, n_states2), F32),
                        pltpu.VMEM((bsz, d), F32),
                        pltpu.VMEM((bsz, n_states2), F32)],
        compiler_params=pltpu.CompilerParams(dimension_semantics=("arbitrary",),
                                             vmem_limit_bytes=VMEM_LIMIT),
        name="mixer",
    )(x3d, *consts, *streamed)


def kernel(x, c, mod_w, mod_b, norm1_g, ffn1_w_up, ffn1_w_down, norm2_g, w_in, b_in, conv_w, conv_b, lru_w_r, lru_b_r, lru_w_i, lru_b_i, lru_lambda, proj_a, s5_a_re, s5_a_im, s5_log_dt, s5_b_re, s5_b_im, s5_c_re, s5_c_im, s5_d, glu_w, glu_b, proj_b, w_out, norm3_g, ffn2_w_up, ffn2_w_down, final_g):
    bsz, seq, d = x.shape
    assert bsz == SUBLANES, "time-major layout puts the batch on the 8 sublanes"
    depth = mod_w.shape[0]
    for l in range(depth):
        mod2 = _mod(c, mod_w[l], mod_b[l])
        mod3 = mod2.reshape(bsz, 1, N_MOD * d)
        lam, bd, cd = _s5_prep(s5_a_re[l], s5_a_im[l], s5_log_dt[l], s5_b_re[l], s5_b_im[l],
                               s5_c_re[l], s5_c_im[l])
        wr_bd, wi_bd = _gate_prep(lru_w_r[l], lru_w_i[l])
        last = l == depth - 1
        x2d = _ffn(x.reshape(bsz * seq, d), mod3, norm1_g[l], ffn1_w_up[l], ffn1_w_down[l], final_g,
                   seq=seq, mod_base=0, final_norm=False)
        x3d = _mixer(x2d.reshape(bsz, seq, d), mod2, norm2_g[l], w_in[l], b_in[l], conv_w[l], conv_b[l],
                     wr_bd, lru_b_r[l], wi_bd, lru_b_i[l], lru_lambda[l], proj_a[l],
                     bd, cd, lam, s5_d[l], glu_w[l], glu_b[l], proj_b[l], w_out[l])
        x = _ffn(x3d.reshape(bsz * seq, d), mod3, norm3_g[l], ffn2_w_up[l], ffn2_w_down[l], final_g,
                 seq=seq, mod_base=6, final_norm=last).reshape(bsz, seq, d)
    return x
```

```python
import functools
import math

import jax
import jax.numpy as jnp
from jax import lax
from jax.experimental import pallas as pl
from jax.experimental.pallas import tpu as pltpu

F32 = jnp.float32
BF16 = jnp.bfloat16

EPS = 1e-6
LRU_C = 8.0
CONV_WIDTH = 4
S5_GROUP = 16
S5_STATE = 64
N_MOD = 9

SUBLANES = 8
LANES = 128
GATE_TILE = 256
S5_HALF_CH = 256
VMEM_LIMIT = 56 * 1024 * 1024

WEIGHT_STAGE_SLOTS = 4
FFN_TM = 512
FFN_HEAD_COLS = 256
MIX_TT = 64
S5_LANE_CHUNK = 512


def _sigmoid(x):
    return 0.5 * jnp.tanh(0.5 * x) + 0.5


def _gelu_tanh(x):
    c = math.sqrt(2.0 / math.pi)
    return 0.5 * x * (1.0 + jnp.tanh(c * (x + 0.044715 * (x * x * x))))


def _rms_norm(x, g):
    ms = jnp.mean(x * x, axis=-1, keepdims=True)
    return x * lax.rsqrt(ms + EPS) * g


def _const_spec(shape):
    nd = len(shape)
    return pl.BlockSpec(shape, lambda *_: (0,) * nd, pipeline_mode=pl.Buffered(1))


def _mod_kernel(c_ref, w_ref, b_ref, o_ref):
    c = c_ref[...]
    ca = c * _sigmoid(c)
    o_ref[...] = jnp.dot(ca.astype(BF16), w_ref[...].astype(BF16),
                         preferred_element_type=F32) + b_ref[...]


def _mod(c, mod_w, mod_b):
    bsz, d = c.shape
    n = mod_w.shape[1]
    tn = d
    return pl.pallas_call(
        _mod_kernel,
        grid=(n // tn,),
        in_specs=[pl.BlockSpec((bsz, d), lambda j: (0, 0)),
                  pl.BlockSpec((d, tn), lambda j: (0, j)),
                  pl.BlockSpec((1, tn), lambda j: (0, j))],
        out_specs=pl.BlockSpec((bsz, tn), lambda j: (0, j)),
        out_shape=jax.ShapeDtypeStruct((bsz, n), F32),
        name="mod",
    )(c, mod_w, mod_b.reshape(1, n))


def _cmul(ar, ai, br, bi):
    return ar * br - ai * bi, ar * bi + ai * br


def _s5_discretise(a_re, a_im, log_dt):
    dt = jnp.exp(log_dt)
    mag = jnp.exp(a_re * dt)
    lr = mag * jnp.cos(a_im * dt)
    li = mag * jnp.sin(a_im * dt)
    den = a_re * a_re + a_im * a_im
    nr = lr - 1.0
    cr = (nr * a_re + li * a_im) / den
    ci = (li * a_re - nr * a_im) / den
    return lr, li, cr, ci


def _s5_prep_kernel(are_g, aim_g, ldt_g, are_r, aim_r, ldt_r, bre_t, bim_t, cre_t, cim_t,
                    lam_ref, bd_ref, cd_ref):
    lr, li, _, _ = _s5_discretise(are_g[...], aim_g[...], ldt_g[...])
    lam_ref[0] = lr
    lam_ref[1] = li

    _, _, cr, ci = _s5_discretise(are_r[...], aim_r[...], ldt_r[...])
    bb_re, bb_im = _cmul(cr, ci, bre_t[...], bim_t[...])
    half_states = (S5_HALF_CH // S5_GROUP) * S5_STATE
    erow = lax.broadcasted_iota(jnp.int32, (S5_STATE, half_states), 0)
    ecol = lax.broadcasted_iota(jnp.int32, (S5_STATE, half_states), 1)
    spread = jnp.where((ecol % S5_STATE) == erow, 1.0, 0.0).astype(BF16)
    brow_g = lax.broadcasted_iota(jnp.int32, (S5_HALF_CH, half_states), 0) // S5_GROUP
    bcol_g = lax.broadcasted_iota(jnp.int32, (S5_HALF_CH, half_states), 1) // S5_STATE
    bmask = brow_g == bcol_g
    crow_g = lax.broadcasted_iota(jnp.int32, (half_states, S5_HALF_CH), 0) // S5_STATE
    ccol_g = lax.broadcasted_iota(jnp.int32, (half_states, S5_HALF_CH), 1) // S5_GROUP
    cmask = crow_g == ccol_g
    reps = half_states // S5_STATE
    for h in range(bd_ref.shape[0]):
        rows = slice(h * S5_HALF_CH, (h + 1) * S5_HALF_CH)
        for k, bb in enumerate((bb_re, bb_im)):
            tiled = jnp.dot(bb[rows, :].astype(BF16), spread, preferred_element_type=F32)
            bd_ref[h, :, k * half_states:(k + 1) * half_states] = jnp.where(bmask, tiled, 0.0).astype(BF16)
        for k, (ct, sign) in enumerate(((cre_t, 1.0), (cim_t, -1.0))):
            tiled = jnp.concatenate([ct[:, rows]] * reps, axis=0)
            cd_ref[h, k * half_states:(k + 1) * half_states, :] = jnp.where(cmask, sign * tiled, 0.0).astype(BF16)


def _s5_prep(a_re, a_im, log_dt, b_re, b_im, c_re, c_im):
    g, p = a_re.shape
    hch = b_re.shape[2]
    assert hch == S5_GROUP and p == S5_STATE
    width = g * hch
    halves = width // S5_HALF_CH
    half_states = (S5_HALF_CH // hch) * p
    rep = lambda a: jnp.repeat(a, hch, axis=0)
    ldt = log_dt.reshape(g, 1)
    bt = lambda b: jnp.transpose(b, (0, 2, 1)).reshape(width, p)
    ct = lambda c: jnp.transpose(c, (2, 0, 1)).reshape(p, width)
    lam, bd, cd = pl.pallas_call(
        _s5_prep_kernel,
        out_shape=(jax.ShapeDtypeStruct((2, g, p), F32),
                   jax.ShapeDtypeStruct((halves, S5_HALF_CH, 2 * half_states), BF16),
                   jax.ShapeDtypeStruct((halves, 2 * half_states, S5_HALF_CH), BF16)),
        name="s5_prep",
    )(a_re, a_im, ldt, rep(a_re), rep(a_im), rep(ldt), bt(b_re), bt(b_im), ct(c_re), ct(c_im))
    return lam.reshape(2, g * p), bd, cd


def _gate_prep_kernel(wr_ref, wi_ref, or_ref, oi_ref):
    hd = wr_ref.shape[1]
    per_tile = GATE_TILE // hd
    erow = lax.broadcasted_iota(jnp.int32, (hd, GATE_TILE), 0)
    ecol = lax.broadcasted_iota(jnp.int32, (hd, GATE_TILE), 1)
    for w_ref, o_ref in ((wr_ref, or_ref), (wi_ref, oi_ref)):
        for a in range(o_ref.shape[0]):
            for b in range(per_tile):
                place = jnp.where(ecol == erow + b * hd, 1.0, 0.0).astype(BF16)
                o_ref[a, b * hd:(b + 1) * hd, :] = jnp.dot(
                    w_ref[a * per_tile + b].astype(BF16), place, preferred_element_type=F32).astype(BF16)


def _gate_prep(w_r, w_i):
    heads, hd, _ = w_r.shape
    tiles = heads * hd // GATE_TILE
    shp = jax.ShapeDtypeStruct((tiles, GATE_TILE, GATE_TILE), BF16)
    return pl.pallas_call(_gate_prep_kernel, out_shape=(shp, shp), name="gate_prep")(w_r, w_i)


def _stream_cast(mats, stages, sem_ref):
    chunks = []
    used = [0] * len(stages)
    for src, dst, si in mats:
        rows = stages[si].shape[1]
        assert src.shape[0] % rows == 0 and src.shape[1] <= stages[si].shape[2]
        for r0 in range(0, src.shape[0], rows):
            slot = used[si] % WEIGHT_STAGE_SLOTS
            used[si] += 1
            chunks.append((src, dst, si, r0, rows, slot))

    def copy(k):
        src, _, si, r0, rows, slot = chunks[k]
        return pltpu.make_async_copy(src.at[pl.ds(r0, rows), :],
                                     stages[si].at[slot, :, pl.ds(0, src.shape[1])],
                                     sem_ref.at[si * WEIGHT_STAGE_SLOTS + slot])

    ahead = WEIGHT_STAGE_SLOTS - 1
    for k in range(min(ahead, len(chunks))):
        copy(k).start()
    for k, (src, dst, si, r0, rows, slot) in enumerate(chunks):
        if k + ahead < len(chunks):
            copy(k + ahead).start()
        copy(k).wait()
        dst[r0:r0 + rows, :] = stages[si][slot, :, 0:src.shape[1]].astype(BF16)


def _weight_scratch(weights, stage_shapes):
    return ([pltpu.VMEM(w.shape, BF16) for w in weights]
            + [pltpu.VMEM((WEIGHT_STAGE_SLOTS,) + tuple(s), F32) for s in stage_shapes]
            + [pltpu.SemaphoreType.DMA((WEIGHT_STAGE_SLOTS * len(stage_shapes),))])


def _ffn_kernel(x_ref, xn_ref, mod_ref, modn_ref, g_ref, fg_ref, wup_hbm, wdn_hbm, o_ref,
                wup_ref, wdn_ref, stage_up_ref, stage_dn_ref, sem_ref, u_ref, h0_ref, *, mod_base, final_norm):
    d = x_ref.shape[1]
    f = wdn_ref.shape[0]
    c0 = h0_ref.shape[1]

    def modulated(xv, m_ref):
        sh = m_ref[:, (mod_base + 0) * d:(mod_base + 1) * d]
        sc = m_ref[:, (mod_base + 1) * d:(mod_base + 2) * d]
        return (_rms_norm(xv, g_ref[...]) * (1.0 + sc) + sh).astype(BF16)

    @pl.when(pl.program_id(0) == 0)
    def _():
        _stream_cast([(wup_hbm, wup_ref, 0), (wdn_hbm, wdn_ref, 1)], [stage_up_ref, stage_dn_ref], sem_ref)
        u0 = modulated(x_ref[...], mod_ref)
        u_ref[...] = u0
        h0_ref[...] = jnp.dot(u0, wup_ref[:, :c0], preferred_element_type=F32)

    x = x_ref[...]
    gt = mod_ref[:, (mod_base + 2) * d:(mod_base + 3) * d]
    h0 = h0_ref[...]
    h_rest = jnp.dot(u_ref[...], wup_ref[:, c0:], preferred_element_type=F32)
    un = modulated(xn_ref[...], modn_ref)
    u_ref[...] = un
    h0_ref[...] = jnp.dot(un, wup_ref[:, :c0], preferred_element_type=F32)
    a = jnp.concatenate([h0, h_rest[:, :f - c0]], axis=1)
    b = h_rest[:, f - c0:]
    act = (a * _sigmoid(a)) * b
    y = jnp.dot(act.astype(BF16), wdn_ref[...], preferred_element_type=F32)
    out = x + (0.5 * gt) * y
    if final_norm:
        out = _rms_norm(out, fg_ref[...])
    o_ref[...] = out


def _ffn(x2d, mod3, norm_g, w_up, w_down, final_g, *, seq, mod_base, final_norm):
    t, d = x2d.shape
    f = w_down.shape[0]
    tm = FFN_TM
    tiles_per_seq = seq // tm
    n_tiles = t // tm
    nxt = lambda i: jnp.minimum(i + 1, n_tiles - 1)
    kern = functools.partial(_ffn_kernel, mod_base=mod_base, final_norm=final_norm)
    return pl.pallas_call(
        kern,
        grid=(n_tiles,),
        in_specs=[pl.BlockSpec((tm, d), lambda i: (i, 0)),
                  pl.BlockSpec((tm, d), lambda i: (nxt(i), 0)),
                  pl.BlockSpec((None, 1, mod3.shape[2]), lambda i: (i // tiles_per_seq, 0, 0)),
                  pl.BlockSpec((None, 1, mod3.shape[2]), lambda i: (nxt(i) // tiles_per_seq, 0, 0)),
                  _const_spec((1, d)),
                  _const_spec((1, d)),
                  pl.BlockSpec(memory_space=pl.ANY),
                  pl.BlockSpec(memory_space=pl.ANY)],
        out_specs=pl.BlockSpec((tm, d), lambda i: (i, 0)),
        out_shape=jax.ShapeDtypeStruct((t, d), F32),
        scratch_shapes=_weight_scratch([w_up, w_down], [(d // 16, 2 * f), (f // 8, d)])
                       + [pltpu.VMEM((tm, d), BF16),
                          pltpu.VMEM((tm, FFN_HEAD_COLS), F32)],
        compiler_params=pltpu.CompilerParams(dimension_semantics=("arbitrary",),
                                             vmem_limit_bytes=VMEM_LIMIT),
        name="ffn_final" if final_norm else "ffn",
    )(x2d, x2d, mod3, mod3, norm_g.reshape(1, d), final_g.reshape(1, d), w_up, w_down)


def _lru_scan(a_ref, u_ref, carry_ref):
    ts = u_ref.shape[0]
    h = carry_ref[...]
    for t in range(ts // SUBLANES):
        sl = slice(t * SUBLANES, (t + 1) * SUBLANES)
        h = a_ref[sl, :] * h + u_ref[sl, :]
        u_ref[sl, :] = h
    carry_ref[...] = h


def _s5_scan(u_ref, lam_ref, carry_ref, halves):
    ts = u_ref.shape[0]
    half_states = lam_ref.shape[1] // halves
    w = S5_LANE_CHUNK
    for h in range(halves):
        for j in range(half_states // w):
            cre = slice(2 * half_states * h + j * w, 2 * half_states * h + (j + 1) * w)
            cim = slice(2 * half_states * h + half_states + j * w, 2 * half_states * h + half_states + (j + 1) * w)
            lcs = slice(half_states * h + j * w, half_states * h + (j + 1) * w)
            lr = jnp.broadcast_to(lam_ref[0:1, lcs], (SUBLANES, w))
            li = jnp.broadcast_to(lam_ref[1:2, lcs], (SUBLANES, w))
            hr = carry_ref[:, cre]
            hi = carry_ref[:, cim]
            for t in range(ts // SUBLANES):
                sl = slice(t * SUBLANES, (t + 1) * SUBLANES)
                hr, hi = (u_ref[sl, cre] + (lr * hr - li * hi),
                          u_ref[sl, cim] + (lr * hi + li * hr))
                u_ref[sl, cre] = hr
                u_ref[sl, cim] = hi
            carry_ref[:, cre] = hr
            carry_ref[:, cim] = hi


def _mixer_kernel(mod_ref, g_ref, bin_ref, cw_ref, cb_ref,
                  wr_ref, br_ref, wi_ref, bi_ref, lamp_ref,
                  bd_ref, cd_ref, lam_ref, sd_ref, gb_ref,
                  win_hbm, pa_hbm, gw_hbm, pb_hbm, wo_hbm, x_hbm,
                  o_hbm,
                  win_ref, pa_ref, gw_ref, pb_ref, wo_ref, stage_in_ref, stage_sq_ref, sem_ref,
                  xt_ref, ot_ref, sem_x_ref, sem_o_ref,
                  xpad_ref, a_ref, u_ref, s5_ref, hl_ref, hs_ref):
    _, tt, nb, d = xt_ref.shape
    ts = nb * tt
    w5 = sd_ref.shape[1]
    halves = bd_ref.shape[0]
    hs2 = bd_ref.shape[2]
    pad = (CONV_WIDTH - 1) * nb
    step = pl.program_id(0)
    n_steps = pl.num_programs(0)
    slot = step % 2

    def x_copies(tile, sl):
        return [pltpu.make_async_copy(x_hbm.at[b, pl.ds(tile * tt, tt), :], xt_ref.at[sl, :, b, :],
                                      sem_x_ref.at[sl, b]) for b in range(nb)]

    def o_copies(tile, sl):
        return [pltpu.make_async_copy(ot_ref.at[sl, :, b, :], o_hbm.at[b, pl.ds(tile * tt, tt), :],
                                      sem_o_ref.at[sl, b]) for b in range(nb)]

    @pl.when(step == 0)
    def _():
        for cp in x_copies(0, 0):
            cp.start()
        xpad_ref[0:pad, :] = jnp.zeros((pad, d), F32)
        hl_ref[...] = jnp.zeros_like(hl_ref)
        hs_ref[...] = jnp.zeros_like(hs_ref)
        _stream_cast([(win_hbm, win_ref, 0), (pa_hbm, pa_ref, 1), (gw_hbm, gw_ref, 1), (pb_hbm, pb_ref, 1),
                      (wo_hbm, wo_ref, 1)], [stage_in_ref, stage_sq_ref], sem_ref)

    @pl.when(step + 1 < n_steps)
    def _():
        for cp in x_copies(step + 1, 1 - slot):
            cp.start()

    for cp in x_copies(step, slot):
        cp.wait()

    def per_batch(v):
        return v[None, :, :]

    x3 = xt_ref[slot]
    sh = per_batch(mod_ref[:, 3 * d:4 * d])
    sc = per_batch(mod_ref[:, 4 * d:5 * d])
    gt = per_batch(mod_ref[:, 5 * d:6 * d])
    u_bf = (_rms_norm(x3, g_ref[...]) * (1.0 + sc) + sh).reshape(ts, d).astype(BF16)

    def in_proj(lo, hi):
        return jnp.dot(u_bf, win_ref[:, lo:hi], preferred_element_type=F32) + bin_ref[:, lo:hi]

    xpad_ref[pad:pad + ts, :] = in_proj(0, d)
    xb = in_proj(2 * d, 2 * d + w5)
    xb_bf = xb.astype(BF16)
    for h in range(halves):
        s5_ref[:, h * hs2:(h + 1) * hs2] = jnp.dot(
            xb_bf[:, h * S5_HALF_CH:(h + 1) * S5_HALF_CH], bd_ref[h], preferred_element_type=F32)
    ga_pre = in_proj(d, 2 * d)

    xc = cb_ref[...]
    for k in range(CONV_WIDTH):
        xc = xc + xpad_ref[k * nb:k * nb + ts, :] * cw_ref[k:k + 1, :]
    xpad_ref[0:pad, :] = xpad_ref[ts:ts + pad, :]
    xc_bf = xc.astype(BF16)
    gates = []
    for w_ref, b_ref in ((wr_ref, br_ref), (wi_ref, bi_ref)):
        cols = [jnp.dot(xc_bf[:, j * GATE_TILE:(j + 1) * GATE_TILE], w_ref[j], preferred_element_type=F32)
                for j in range(w_ref.shape[0])]
        gates.append(_sigmoid(jnp.concatenate(cols, axis=1) + b_ref[...]))
    r_gate, i_gate = gates
    mga_pre = in_proj(2 * d + w5, 3 * d + w5)
    nl = -lamp_ref[...]
    softplus = jnp.maximum(nl, 0.0) + jnp.log1p(jnp.exp(-jnp.abs(nl)))
    log_a = (-LRU_C * softplus) * r_gate
    a_gate = jnp.exp(log_a)
    a_ref[...] = a_gate
    one_m_a2 = 1.0 - a_gate * a_gate
    mult = jnp.where(one_m_a2 > 0.0, one_m_a2 * lax.rsqrt(one_m_a2), 0.0)
    u_ref[...] = mult * (i_gate * xc)
    _lru_scan(a_ref, u_ref, hl_ref)
    ya = u_ref[...] * _gelu_tanh(ga_pre)
    proj_a = jnp.dot(ya.astype(BF16), pa_ref[...], preferred_element_type=F32)
    mgb_pre = in_proj(3 * d + w5, 4 * d + w5)

    _s5_scan(s5_ref, lam_ref, hs_ref, halves)
    ys = [jnp.dot(s5_ref[:, h * hs2:(h + 1) * hs2].astype(BF16), cd_ref[h], preferred_element_type=F32)
          for h in range(halves)]
    yb = _gelu_tanh(jnp.concatenate(ys, axis=1) + sd_ref[...] * xb)
    yb = yb * _sigmoid(jnp.dot(yb.astype(BF16), gw_ref[...], preferred_element_type=F32) + gb_ref[...])
    proj_b = jnp.dot(yb.astype(BF16), pb_ref[...], preferred_element_type=F32)

    m = _sigmoid(mga_pre) * proj_a + _sigmoid(mgb_pre) * proj_b
    y = jnp.dot(m.astype(BF16), wo_ref[...], preferred_element_type=F32)
    out = x3 + gt * y.reshape(tt, nb, d)

    @pl.when(step >= 2)
    def _():
        for cp in o_copies(step - 2, slot):
            cp.wait()

    ot_ref[slot] = out
    for cp in o_copies(step, slot):
        cp.start()

    @pl.when(step == n_steps - 1)
    def _():
        for cp in o_copies(step - 1, 1 - slot):
            cp.wait()
        for cp in o_copies(step, slot):
            cp.wait()


def _mixer(x3d, mod2, norm_g, w_in, b_in, conv_w, conv_b, wr_bd, b_r, wi_bd, b_i, lam_p, proj_a,
           bd, cd, lam, s5_d, glu_w, glu_b, proj_b, w_out):
    bsz, seq, d = x3d.shape
    tt = MIX_TT
    ts = tt * bsz
    n_states2 = bd.shape[0] * bd.shape[2]
    row = lambda v: v.reshape(1, -1)
    consts = [mod2, row(norm_g), row(b_in), conv_w, row(conv_b), wr_bd, row(b_r), wi_bd, row(b_i),
              row(lam_p), bd, cd, lam, row(s5_d), row(glu_b)]
    streamed = [w_in, proj_a, glu_w, proj_b, w_out]
    assert seq // tt >= 2, "the write-back drain expects at least two sequence tiles"
    return pl.pallas_call(
        _mixer_kernel,
        grid=(seq // tt,),
        in_specs=[_const_spec(c.shape) for c in consts]
                 + [pl.BlockSpec(memory_space=pl.ANY)] * (len(streamed) + 1),
        out_specs=pl.BlockSpec(memory_space=pl.ANY),
        out_shape=jax.ShapeDtypeStruct((bsz, seq, d), F32),
        scratch_shapes=_weight_scratch(streamed, [(d // 32, w_in.shape[1]), (d // 8, d)])
                       + [pltpu.VMEM((2, tt, bsz, d), F32),
                          pltpu.VMEM((2, tt, bsz, d), F32),
                          pltpu.SemaphoreType.DMA((2, bsz)),
                          pltpu.SemaphoreType.DMA((2, bsz)),
                        pltpu.VMEM((ts + (CONV_WIDTH - 1) * bsz, d), F32),
                        pltpu.VMEM((ts, d), F32),
                        pltpu.VMEM((ts, d), F32),
                        pltpu.VMEM((ts, n_states2), F32),
                        pltpu.VMEM((bsz, d), F32),
                        pltpu.VMEM((bsz, n_states2), F32)],
        compiler_params=pltpu.CompilerParams(dimension_semantics=("arbitrary",),
                                             vmem_limit_bytes=VMEM_LIMIT),
        name="mixer",
    )(*consts, *streamed, x3d)


def kernel(x, c, mod_w, mod_b, norm1_g, ffn1_w_up, ffn1_w_down, norm2_g, w_in, b_in, conv_w, conv_b, lru_w_r, lru_b_r, lru_w_i, lru_b_i, lru_lambda, proj_a, s5_a_re, s5_a_im, s5_log_dt, s5_b_re, s5_b_im, s5_c_re, s5_c_im, s5_d, glu_w, glu_b, proj_b, w_out, norm3_g, ffn2_w_up, ffn2_w_down, final_g):
    bsz, seq, d = x.shape
    assert bsz == SUBLANES, "time-major layout puts the batch on the 8 sublanes"
    depth = mod_w.shape[0]
    for l in range(depth):
        mod2 = _mod(c, mod_w[l], mod_b[l])
        mod3 = mod2.reshape(bsz, 1, N_MOD * d)
        lam, bd, cd = _s5_prep(s5_a_re[l], s5_a_im[l], s5_log_dt[l], s5_b_re[l], s5_b_im[l],
                               s5_c_re[l], s5_c_im[l])
        wr_bd, wi_bd = _gate_prep(lru_w_r[l], lru_w_i[l])
        last = l == depth - 1
        x2d = _ffn(x.reshape(bsz * seq, d), mod3, norm1_g[l], ffn1_w_up[l], ffn1_w_down[l], final_g,
                   seq=seq, mod_base=0, final_norm=False)
        x3d = _mixer(x2d.reshape(bsz, seq, d), mod2, norm2_g[l], w_in[l], b_in[l], conv_w[l], conv_b[l],
                     wr_bd, lru_b_r[l], wi_bd, lru_b_i[l], lru_lambda[l], proj_a[l],
                     bd, cd, lam, s5_d[l], glu_w[l], glu_b[l], proj_b[l], w_out[l])
        x = _ffn(x3d.reshape(bsz * seq, d), mod3, norm3_g[l], ffn2_w_up[l], ffn2_w_down[l], final_g,
                 seq=seq, mod_base=6, final_norm=last).reshape(bsz, seq, d)
    return x
```

```python
import functools
import math

import jax
import jax.numpy as jnp
from jax import lax
from jax.experimental import pallas as pl
from jax.experimental.pallas import tpu as pltpu

F32 = jnp.float32
BF16 = jnp.bfloat16

EPS = 1e-6
LRU_C = 8.0
CONV_WIDTH = 4
S5_GROUP = 16
S5_STATE = 64
N_MOD = 9

SUBLANES = 8
LANES = 128
GATE_TILE = 256
S5_HALF_CH = 256
VMEM_LIMIT = 56 * 1024 * 1024

WEIGHT_STAGE_SLOTS = 4
FFN_TM = 512
FFN_HEAD_COLS = 256
MIX_TT = 64
S5_LANE_CHUNK = 512


def _sigmoid(x):
    return 0.5 * jnp.tanh(0.5 * x) + 0.5


def _gelu_tanh(x):
    c = math.sqrt(2.0 / math.pi)
    return 0.5 * x * (1.0 + jnp.tanh(c * (x + 0.044715 * (x * x * x))))


def _rms_norm(x, g):
    ms = jnp.mean(x * x, axis=-1, keepdims=True)
    return x * lax.rsqrt(ms + EPS) * g


def _const_spec(shape):
    nd = len(shape)
    return pl.BlockSpec(shape, lambda *_: (0,) * nd, pipeline_mode=pl.Buffered(1))


def _mod_kernel(c_ref, w_ref, b_ref, o_ref):
    c = c_ref[...]
    ca = c * _sigmoid(c)
    o_ref[...] = jnp.dot(ca.astype(BF16), w_ref[...].astype(BF16),
                         preferred_element_type=F32) + b_ref[...]


def _mod(c, mod_w, mod_b):
    bsz, d = c.shape
    n = mod_w.shape[1]
    tn = d
    return pl.pallas_call(
        _mod_kernel,
        grid=(n // tn,),
        in_specs=[pl.BlockSpec((bsz, d), lambda j: (0, 0)),
                  pl.BlockSpec((d, tn), lambda j: (0, j)),
                  pl.BlockSpec((1, tn), lambda j: (0, j))],
        out_specs=pl.BlockSpec((bsz, tn), lambda j: (0, j)),
        out_shape=jax.ShapeDtypeStruct((bsz, n), F32),
        name="mod",
    )(c, mod_w, mod_b.reshape(1, n))


def _cmul(ar, ai, br, bi):
    return ar * br - ai * bi, ar * bi + ai * br


def _s5_discretise(a_re, a_im, log_dt):
    dt = jnp.exp(log_dt)
    mag = jnp.exp(a_re * dt)
    lr = mag * jnp.cos(a_im * dt)
    li = mag * jnp.sin(a_im * dt)
    den = a_re * a_re + a_im * a_im
    nr = lr - 1.0
    cr = (nr * a_re + li * a_im) / den
    ci = (li * a_re - nr * a_im) / den
    return lr, li, cr, ci


def _s5_prep_kernel(are_g, aim_g, ldt_g, are_r, aim_r, ldt_r, bre_t, bim_t, cre_t, cim_t,
                    lam_ref, bd_ref, cd_ref):
    lr, li, _, _ = _s5_discretise(are_g[...], aim_g[...], ldt_g[...])
    lam_ref[0] = lr
    lam_ref[1] = li

    _, _, cr, ci = _s5_discretise(are_r[...], aim_r[...], ldt_r[...])
    bb_re, bb_im = _cmul(cr, ci, bre_t[...], bim_t[...])
    half_states = (S5_HALF_CH // S5_GROUP) * S5_STATE
    erow = lax.broadcasted_iota(jnp.int32, (S5_STATE, half_states), 0)
    ecol = lax.broadcasted_iota(jnp.int32, (S5_STATE, half_states), 1)
    spread = jnp.where((ecol % S5_STATE) == erow, 1.0, 0.0).astype(BF16)
    brow_g = lax.broadcasted_iota(jnp.int32, (S5_HALF_CH, half_states), 0) // S5_GROUP
    bcol_g = lax.broadcasted_iota(jnp.int32, (S5_HALF_CH, half_states), 1) // S5_STATE
    bmask = brow_g == bcol_g
    crow_g = lax.broadcasted_iota(jnp.int32, (half_states, S5_HALF_CH), 0) // S5_STATE
    ccol_g = lax.broadcasted_iota(jnp.int32, (half_states, S5_HALF_CH), 1) // S5_GROUP
    cmask = crow_g == ccol_g
    reps = half_states // S5_STATE
    for h in range(bd_ref.shape[0]):
        rows = slice(h * S5_HALF_CH, (h + 1) * S5_HALF_CH)
        for k, bb in enumerate((bb_re, bb_im)):
            tiled = jnp.dot(bb[rows, :].astype(BF16), spread, preferred_element_type=F32)
            bd_ref[h, :, k * half_states:(k + 1) * half_states] = jnp.where(bmask, tiled, 0.0).astype(BF16)
        for k, (ct, sign) in enumerate(((cre_t, 1.0), (cim_t, -1.0))):
            tiled = jnp.concatenate([ct[:, rows]] * reps, axis=0)
            cd_ref[h, k * half_states:(k + 1) * half_states, :] = jnp.where(cmask, sign * tiled, 0.0).astype(BF16)


def _s5_prep(a_re, a_im, log_dt, b_re, b_im, c_re, c_im):
    g, p = a_re.shape
    hch = b_re.shape[2]
    assert hch == S5_GROUP and p == S5_STATE
    width = g * hch
    halves = width // S5_HALF_CH
    half_states = (S5_HALF_CH // hch) * p
    rep = lambda a: jnp.repeat(a, hch, axis=0)
    ldt = log_dt.reshape(g, 1)
    bt = lambda b: jnp.transpose(b, (0, 2, 1)).reshape(width, p)
    ct = lambda c: jnp.transpose(c, (2, 0, 1)).reshape(p, width)
    lam, bd, cd = pl.pallas_call(
        _s5_prep_kernel,
        out_shape=(jax.ShapeDtypeStruct((2, g, p), F32),
                   jax.ShapeDtypeStruct((halves, S5_HALF_CH, 2 * half_states), BF16),
                   jax.ShapeDtypeStruct((halves, 2 * half_states, S5_HALF_CH), BF16)),
        name="s5_prep",
    )(a_re, a_im, ldt, rep(a_re), rep(a_im), rep(ldt), bt(b_re), bt(b_im), ct(c_re), ct(c_im))
    return lam.reshape(2, g * p), bd, cd


def _gate_prep_kernel(wr_ref, wi_ref, or_ref, oi_ref):
    hd = wr_ref.shape[1]
    per_tile = GATE_TILE // hd
    erow = lax.broadcasted_iota(jnp.int32, (hd, GATE_TILE), 0)
    ecol = lax.broadcasted_iota(jnp.int32, (hd, GATE_TILE), 1)
    for w_ref, o_ref in ((wr_ref, or_ref), (wi_ref, oi_ref)):
        for a in range(o_ref.shape[0]):
            for b in range(per_tile):
                place = jnp.where(ecol == erow + b * hd, 1.0, 0.0).astype(BF16)
                o_ref[a, b * hd:(b + 1) * hd, :] = jnp.dot(
                    w_ref[a * per_tile + b].astype(BF16), place, preferred_element_type=F32).astype(BF16)


def _gate_prep(w_r, w_i):
    heads, hd, _ = w_r.shape
    tiles = heads * hd // GATE_TILE
    shp = jax.ShapeDtypeStruct((tiles, GATE_TILE, GATE_TILE), BF16)
    return pl.pallas_call(_gate_prep_kernel, out_shape=(shp, shp), name="gate_prep")(w_r, w_i)


def _stream_cast(mats, stages, sem_ref):
    chunks = []
    used = [0] * len(stages)
    for src, dst, si in mats:
        rows = stages[si].shape[1]
        assert src.shape[0] % rows == 0 and src.shape[1] <= stages[si].shape[2]
        for r0 in range(0, src.shape[0], rows):
            slot = used[si] % WEIGHT_STAGE_SLOTS
            used[si] += 1
            chunks.append((src, dst, si, r0, rows, slot))

    def copy(k):
        src, _, si, r0, rows, slot = chunks[k]
        return pltpu.make_async_copy(src.at[pl.ds(r0, rows), :],
                                     stages[si].at[slot, :, pl.ds(0, src.shape[1])],
                                     sem_ref.at[si * WEIGHT_STAGE_SLOTS + slot])

    ahead = WEIGHT_STAGE_SLOTS - 1
    for k in range(min(ahead, len(chunks))):
        copy(k).start()
    for k, (src, dst, si, r0, rows, slot) in enumerate(chunks):
        if k + ahead < len(chunks):
            copy(k + ahead).start()
        copy(k).wait()
        dst[r0:r0 + rows, :] = stages[si][slot, :, 0:src.shape[1]].astype(BF16)


def _weight_scratch(weights, stage_shapes):
    return ([pltpu.VMEM(w.shape, BF16) for w in weights]
            + [pltpu.VMEM((WEIGHT_STAGE_SLOTS,) + tuple(s), F32) for s in stage_shapes]
            + [pltpu.SemaphoreType.DMA((WEIGHT_STAGE_SLOTS * len(stage_shapes),))])


def _ffn_kernel(x_ref, xn_ref, mod_ref, modn_ref, g_ref, fg_ref, wup_hbm, wdn_hbm, o_ref,
                wup_ref, wdn_ref, stage_up_ref, stage_dn_ref, sem_ref, u_ref, h0_ref, *, mod_base, final_norm):
    d = x_ref.shape[1]
    f = wdn_ref.shape[0]
    c0 = h0_ref.shape[1]

    def modulated(xv, m_ref):
        sh = m_ref[:, (mod_base + 0) * d:(mod_base + 1) * d]
        sc = m_ref[:, (mod_base + 1) * d:(mod_base + 2) * d]
        return (_rms_norm(xv, g_ref[...]) * (1.0 + sc) + sh).astype(BF16)

    @pl.when(pl.program_id(0) == 0)
    def _():
        _stream_cast([(wup_hbm, wup_ref, 0), (wdn_hbm, wdn_ref, 1)], [stage_up_ref, stage_dn_ref], sem_ref)
        u0 = modulated(x_ref[...], mod_ref)
        u_ref[...] = u0
        h0_ref[...] = jnp.dot(u0, wup_ref[:, :c0], preferred_element_type=F32)

    x = x_ref[...]
    gt = mod_ref[:, (mod_base + 2) * d:(mod_base + 3) * d]
    h0 = h0_ref[...]
    h_rest = jnp.dot(u_ref[...], wup_ref[:, c0:], preferred_element_type=F32)
    un = modulated(xn_ref[...], modn_ref)
    u_ref[...] = un
    h0_ref[...] = jnp.dot(un, wup_ref[:, :c0], preferred_element_type=F32)
    a = jnp.concatenate([h0, h_rest[:, :f - c0]], axis=1)
    b = h_rest[:, f - c0:]
    act = (a * _sigmoid(a)) * b
    y = jnp.dot(act.astype(BF16), wdn_ref[...], preferred_element_type=F32)
    out = x + (0.5 * gt) * y
    if final_norm:
        out = _rms_norm(out, fg_ref[...])
    o_ref[...] = out


def _ffn(x2d, mod3, norm_g, w_up, w_down, final_g, *, seq, mod_base, final_norm):
    t, d = x2d.shape
    f = w_down.shape[0]
    tm = FFN_TM
    tiles_per_seq = seq // tm
    n_tiles = t // tm
    nxt = lambda i: jnp.minimum(i + 1, n_tiles - 1)
    kern = functools.partial(_ffn_kernel, mod_base=mod_base, final_norm=final_norm)
    return pl.pallas_call(
        kern,
        grid=(n_tiles,),
        in_specs=[pl.BlockSpec((tm, d), lambda i: (i, 0)),
                  pl.BlockSpec((tm, d), lambda i: (nxt(i), 0)),
                  pl.BlockSpec((None, 1, mod3.shape[2]), lambda i: (i // tiles_per_seq, 0, 0)),
                  pl.BlockSpec((None, 1, mod3.shape[2]), lambda i: (nxt(i) // tiles_per_seq, 0, 0)),
                  _const_spec((1, d)),
                  _const_spec((1, d)),
                  pl.BlockSpec(memory_space=pl.ANY),
                  pl.BlockSpec(memory_space=pl.ANY)],
        out_specs=pl.BlockSpec((tm, d), lambda i: (i, 0)),
        out_shape=jax.ShapeDtypeStruct((t, d), F32),
        scratch_shapes=_weight_scratch([w_up, w_down], [(d // 16, 2 * f), (f // 8, d)])
                       + [pltpu.VMEM((tm, d), BF16),
                          pltpu.VMEM((tm, FFN_HEAD_COLS), F32)],
        compiler_params=pltpu.CompilerParams(dimension_semantics=("arbitrary",),
                                             vmem_limit_bytes=VMEM_LIMIT),
        name="ffn_final" if final_norm else "ffn",
    )(x2d, x2d, mod3, mod3, norm_g.reshape(1, d), final_g.reshape(1, d), w_up, w_down)


def _lru_scan(a_ref, u_ref, carry_ref):
    ts = u_ref.shape[0]
    h = carry_ref[...]
    for t in range(ts // SUBLANES):
        sl = slice(t * SUBLANES, (t + 1) * SUBLANES)
        h = a_ref[sl, :] * h + u_ref[sl, :]
        u_ref[sl, :] = h
    carry_ref[...] = h


def _s5_scan(u_ref, lam_ref, carry_ref, halves):
    ts = u_ref.shape[0]
    half_states = lam_ref.shape[1] // halves
    w = S5_LANE_CHUNK
    for h in range(halves):
        for j in range(half_states // w):
            cre = slice(2 * half_states * h + j * w, 2 * half_states * h + (j + 1) * w)
            cim = slice(2 * half_states * h + half_states + j * w, 2 * half_states * h + half_states + (j + 1) * w)
            lcs = slice(half_states * h + j * w, half_states * h + (j + 1) * w)
            lr = jnp.broadcast_to(lam_ref[0:1, lcs], (SUBLANES, w))
            li = jnp.broadcast_to(lam_ref[1:2, lcs], (SUBLANES, w))
            hr = carry_ref[:, cre]
            hi = carry_ref[:, cim]
            for t in range(ts // SUBLANES):
                sl = slice(t * SUBLANES, (t + 1) * SUBLANES)
                hr, hi = (u_ref[sl, cre] + (lr * hr - li * hi),
                          u_ref[sl, cim] + (lr * hi + li * hr))
                u_ref[sl, cre] = hr
                u_ref[sl, cim] = hi
            carry_ref[:, cre] = hr
            carry_ref[:, cim] = hi


def _mixer_kernel(mod_ref, g_ref, bin_ref, cw_ref, cb_ref,
                  wr_ref, br_ref, wi_ref, bi_ref, lamp_ref,
                  bd_ref, cd_ref, lam_ref, sd_ref, gb_ref,
                  win_hbm, pa_hbm, gw_hbm, pb_hbm, wo_hbm, x_hbm,
                  o_hbm,
                  win_ref, pa_ref, gw_ref, pb_ref, wo_ref, stage_in_ref, stage_sq_ref, sem_ref,
                  xt_ref, ot_ref, sem_x_ref, sem_o_ref,
                  xpad_ref, a_ref, u_ref, s5_ref, hl_ref, hs_ref):
    _, tt, nb, d = xt_ref.shape
    ts = nb * tt
    w5 = sd_ref.shape[1]
    halves = bd_ref.shape[0]
    hs2 = bd_ref.shape[2]
    pad = (CONV_WIDTH - 1) * nb
    step = pl.program_id(0)
    n_steps = pl.num_programs(0)
    slot = step % 2

    def x_copies(tile, sl):
        return [pltpu.make_async_copy(x_hbm.at[b, pl.ds(tile * tt, tt), :], xt_ref.at[sl, :, b, :],
                                      sem_x_ref.at[sl]) for b in range(nb)]

    def o_copies(tile, sl):
        return [pltpu.make_async_copy(ot_ref.at[sl, :, b, :], o_hbm.at[b, pl.ds(tile * tt, tt), :],
                                      sem_o_ref.at[sl]) for b in range(nb)]

    @pl.when(step == 0)
    def _():
        for cp in x_copies(0, 0):
            cp.start()
        xpad_ref[0:pad, :] = jnp.zeros((pad, d), F32)
        hl_ref[...] = jnp.zeros_like(hl_ref)
        hs_ref[...] = jnp.zeros_like(hs_ref)
        _stream_cast([(win_hbm, win_ref, 0), (pa_hbm, pa_ref, 1), (gw_hbm, gw_ref, 1), (pb_hbm, pb_ref, 1),
                      (wo_hbm, wo_ref, 1)], [stage_in_ref, stage_sq_ref], sem_ref)

    @pl.when(step + 1 < n_steps)
    def _():
        for cp in x_copies(step + 1, 1 - slot):
            cp.start()

    for cp in x_copies(step, slot):
        cp.wait()

    def per_batch(v):
        return v[None, :, :]

    x3 = xt_ref[slot]
    sh = per_batch(mod_ref[:, 3 * d:4 * d])
    sc = per_batch(mod_ref[:, 4 * d:5 * d])
    gt = per_batch(mod_ref[:, 5 * d:6 * d])
    u_bf = (_rms_norm(x3, g_ref[...]) * (1.0 + sc) + sh).reshape(ts, d).astype(BF16)

    def in_proj(lo, hi):
        return jnp.dot(u_bf, win_ref[:, lo:hi], preferred_element_type=F32) + bin_ref[:, lo:hi]

    xpad_ref[pad:pad + ts, :] = in_proj(0, d)
    xb = in_proj(2 * d, 2 * d + w5)
    xb_bf = xb.astype(BF16)
    for h in range(halves):
        s5_ref[:, h * hs2:(h + 1) * hs2] = jnp.dot(
            xb_bf[:, h * S5_HALF_CH:(h + 1) * S5_HALF_CH], bd_ref[h], preferred_element_type=F32)
    ga_pre = in_proj(d, 2 * d)

    xc = cb_ref[...]
    for k in range(CONV_WIDTH):
        xc = xc + xpad_ref[k * nb:k * nb + ts, :] * cw_ref[k:k + 1, :]
    xpad_ref[0:pad, :] = xpad_ref[ts:ts + pad, :]
    xc_bf = xc.astype(BF16)
    gates = []
    for w_ref, b_ref in ((wr_ref, br_ref), (wi_ref, bi_ref)):
        cols = [jnp.dot(xc_bf[:, j * GATE_TILE:(j + 1) * GATE_TILE], w_ref[j], preferred_element_type=F32)
                for j in range(w_ref.shape[0])]
        gates.append(_sigmoid(jnp.concatenate(cols, axis=1) + b_ref[...]))
    r_gate, i_gate = gates
    mga_pre = in_proj(2 * d + w5, 3 * d + w5)
    nl = -lamp_ref[...]
    softplus = jnp.maximum(nl, 0.0) + jnp.log1p(jnp.exp(-jnp.abs(nl)))
    log_a = (-LRU_C * softplus) * r_gate
    a_gate = jnp.exp(log_a)
    a_ref[...] = a_gate
    one_m_a2 = 1.0 - a_gate * a_gate
    mult = jnp.where(one_m_a2 > 0.0, one_m_a2 * lax.rsqrt(one_m_a2), 0.0)
    u_ref[...] = mult * (i_gate * xc)
    _lru_scan(a_ref, u_ref, hl_ref)
    ya = u_ref[...] * _gelu_tanh(ga_pre)
    proj_a = jnp.dot(ya.astype(BF16), pa_ref[...], preferred_element_type=F32)
    mgb_pre = in_proj(3 * d + w5, 4 * d + w5)

    _s5_scan(s5_ref, lam_ref, hs_ref, halves)
    ys = [jnp.dot(s5_ref[:, h * hs2:(h + 1) * hs2].astype(BF16), cd_ref[h], preferred_element_type=F32)
          for h in range(halves)]
    yb = _gelu_tanh(jnp.concatenate(ys, axis=1) + sd_ref[...] * xb)
    yb = yb * _sigmoid(jnp.dot(yb.astype(BF16), gw_ref[...], preferred_element_type=F32) + gb_ref[...])
    proj_b = jnp.dot(yb.astype(BF16), pb_ref[...], preferred_element_type=F32)

    m = _sigmoid(mga_pre) * proj_a + _sigmoid(mgb_pre) * proj_b
    y = jnp.dot(m.astype(BF16), wo_ref[...], preferred_element_type=F32)
    out = x3 + gt * y.reshape(tt, nb, d)

    @pl.when(step >= 2)
    def _():
        for cp in o_copies(step - 2, slot):
            cp.wait()

    ot_ref[slot] = out
    for cp in o_copies(step, slot):
        cp.start()

    @pl.when(step == n_steps - 1)
    def _():
        for cp in o_copies(step - 1, 1 - slot):
            cp.wait()
        for cp in o_copies(step, slot):
            cp.wait()


def _mixer(x3d, mod2, norm_g, w_in, b_in, conv_w, conv_b, wr_bd, b_r, wi_bd, b_i, lam_p, proj_a,
           bd, cd, lam, s5_d, glu_w, glu_b, proj_b, w_out):
    bsz, seq, d = x3d.shape
    tt = MIX_TT
    ts = tt * bsz
    n_states2 = bd.shape[0] * bd.shape[2]
    row = lambda v: v.reshape(1, -1)
    consts = [mod2, row(norm_g), row(b_in), conv_w, row(conv_b), wr_bd, row(b_r), wi_bd, row(b_i),
              row(lam_p), bd, cd, lam, row(s5_d), row(glu_b)]
    streamed = [w_in, proj_a, glu_w, proj_b, w_out]
    assert seq // tt >= 2, "the write-back drain expects at least two sequence tiles"
    return pl.pallas_call(
        _mixer_kernel,
        grid=(seq // tt,),
        in_specs=[_const_spec(c.shape) for c in consts]
                 + [pl.BlockSpec(memory_space=pl.ANY)] * (len(streamed) + 1),
        out_specs=pl.BlockSpec(memory_space=pl.ANY),
        out_shape=jax.ShapeDtypeStruct((bsz, seq, d), F32),
        scratch_shapes=_weight_scratch(streamed, [(d // 32, w_in.shape[1]), (d // 8, d)])
                       + [pltpu.VMEM((2, tt, bsz, d), F32),
                          pltpu.VMEM((2, tt, bsz, d), F32),
                          pltpu.SemaphoreType.DMA((2,)),
                          pltpu.SemaphoreType.DMA((2,)),
                        pltpu.VMEM((ts + (CONV_WIDTH - 1) * bsz, d), F32),
                        pltpu.VMEM((ts, d), F32),
                        pltpu.VMEM((ts, d), F32),
                        pltpu.VMEM((ts, n_states2), F32),
                        pltpu.VMEM((bsz, d), F32),
                        pltpu.VMEM((bsz, n_states2), F32)],
        compiler_params=pltpu.CompilerParams(dimension_semantics=("arbitrary",),
                                             vmem_limit_bytes=VMEM_LIMIT),
        name="mixer",
    )(*consts, *streamed, x3d)


def kernel(x, c, mod_w, mod_b, norm1_g, ffn1_w_up, ffn1_w_down, norm2_g, w_in, b_in, conv_w, conv_b, lru_w_r, lru_b_r, lru_w_i, lru_b_i, lru_lambda, proj_a, s5_a_re, s5_a_im, s5_log_dt, s5_b_re, s5_b_im, s5_c_re, s5_c_im, s5_d, glu_w, glu_b, proj_b, w_out, norm3_g, ffn2_w_up, ffn2_w_down, final_g):
    bsz, seq, d = x.shape
    assert bsz == SUBLANES, "time-major layout puts the batch on the 8 sublanes"
    depth = mod_w.shape[0]
    for l in range(depth):
        mod2 = _mod(c, mod_w[l], mod_b[l])
        mod3 = mod2.reshape(bsz, 1, N_MOD * d)
        lam, bd, cd = _s5_prep(s5_a_re[l], s5_a_im[l], s5_log_dt[l], s5_b_re[l], s5_b_im[l],
                               s5_c_re[l], s5_c_im[l])
        wr_bd, wi_bd = _gate_prep(lru_w_r[l], lru_w_i[l])
        last = l == depth - 1
        x2d = _ffn(x.reshape(bsz * seq, d), mod3, norm1_g[l], ffn1_w_up[l], ffn1_w_down[l], final_g,
                   seq=seq, mod_base=0, final_norm=False)
        x3d = _mixer(x2d.reshape(bsz, seq, d), mod2, norm2_g[l], w_in[l], b_in[l], conv_w[l], conv_b[l],
                     wr_bd, lru_b_r[l], wi_bd, lru_b_i[l], lru_lambda[l], proj_a[l],
                     bd, cd, lam, s5_d[l], glu_w[l], glu_b[l], proj_b[l], w_out[l])
        x = _ffn(x3d.reshape(bsz * seq, d), mod3, norm3_g[l], ffn2_w_up[l], ffn2_w_down[l], final_g,
                 seq=seq, mod_base=6, final_norm=last).reshape(bsz, seq, d)
    return x
```

```python
import functools
import math

import jax
import jax.numpy as jnp
from jax import lax
from jax.experimental import pallas as pl
from jax.experimental.pallas import tpu as pltpu

F32 = jnp.float32
BF16 = jnp.bfloat16

EPS = 1e-6
LRU_C = 8.0
CONV_WIDTH = 4
S5_GROUP = 16
S5_STATE = 64
N_MOD = 9

SUBLANES = 8
LANES = 128
GATE_TILE = 256
S5_HALF_CH = 256
VMEM_LIMIT = 56 * 1024 * 1024

WEIGHT_STAGE_SLOTS = 4
MOD_STEPS = 4
FFN_TM = 1024
FFN_HEAD_COLS = 256
MIX_TT = 64
S5_LANE_CHUNK = 512


def _sigmoid(x):
    return 0.5 * jnp.tanh(0.5 * x) + 0.5


def _gelu_tanh(x):
    c = math.sqrt(2.0 / math.pi)
    return 0.5 * x * (1.0 + jnp.tanh(c * (x + 0.044715 * (x * x * x))))


def _rms_norm(x, g):
    ms = jnp.mean(x * x, axis=-1, keepdims=True)
    return x * lax.rsqrt(ms + EPS) * g


def _const_spec(shape):
    nd = len(shape)
    return pl.BlockSpec(shape, lambda *_: (0,) * nd, pipeline_mode=pl.Buffered(1))


def _mod_kernel(c_ref, w_ref, b_ref, o_ref):
    c = c_ref[...]
    ca = c * _sigmoid(c)
    o_ref[...] = jnp.dot(ca.astype(BF16), w_ref[...].astype(BF16),
                         preferred_element_type=F32) + b_ref[...]


def _mod(c, mod_w, mod_b):
    bsz, d = c.shape
    n = mod_w.shape[1]
    tn = n // MOD_STEPS
    return pl.pallas_call(
        _mod_kernel,
        grid=(MOD_STEPS,),
        in_specs=[pl.BlockSpec((bsz, d), lambda j: (0, 0)),
                  pl.BlockSpec((d, tn), lambda j: (0, j)),
                  pl.BlockSpec((1, tn), lambda j: (0, j))],
        out_specs=pl.BlockSpec((bsz, tn), lambda j: (0, j)),
        out_shape=jax.ShapeDtypeStruct((bsz, n), F32),
        compiler_params=pltpu.CompilerParams(dimension_semantics=("arbitrary",),
                                             vmem_limit_bytes=VMEM_LIMIT),
        name="mod",
    )(c, mod_w, mod_b.reshape(1, n))


def _cmul(ar, ai, br, bi):
    return ar * br - ai * bi, ar * bi + ai * br


def _s5_discretise(a_re, a_im, log_dt):
    dt = jnp.exp(log_dt)
    mag = jnp.exp(a_re * dt)
    lr = mag * jnp.cos(a_im * dt)
    li = mag * jnp.sin(a_im * dt)
    den = a_re * a_re + a_im * a_im
    nr = lr - 1.0
    cr = (nr * a_re + li * a_im) / den
    ci = (li * a_re - nr * a_im) / den
    return lr, li, cr, ci


def _s5_prep_kernel(are_g, aim_g, ldt_g, are_r, aim_r, ldt_r, bre_t, bim_t, cre_t, cim_t,
                    lam_ref, bd_ref, cd_ref):
    lr, li, _, _ = _s5_discretise(are_g[...], aim_g[...], ldt_g[...])
    lam_ref[0] = lr
    lam_ref[1] = li

    _, _, cr, ci = _s5_discretise(are_r[...], aim_r[...], ldt_r[...])
    bb_re, bb_im = _cmul(cr, ci, bre_t[...], bim_t[...])
    half_states = (S5_HALF_CH // S5_GROUP) * S5_STATE
    erow = lax.broadcasted_iota(jnp.int32, (S5_STATE, half_states), 0)
    ecol = lax.broadcasted_iota(jnp.int32, (S5_STATE, half_states), 1)
    spread = jnp.where((ecol % S5_STATE) == erow, 1.0, 0.0).astype(BF16)
    brow_g = lax.broadcasted_iota(jnp.int32, (S5_HALF_CH, half_states), 0) // S5_GROUP
    bcol_g = lax.broadcasted_iota(jnp.int32, (S5_HALF_CH, half_states), 1) // S5_STATE
    bmask = brow_g == bcol_g
    crow_g = lax.broadcasted_iota(jnp.int32, (half_states, S5_HALF_CH), 0) // S5_STATE
    ccol_g = lax.broadcasted_iota(jnp.int32, (half_states, S5_HALF_CH), 1) // S5_GROUP
    cmask = crow_g == ccol_g
    reps = half_states // S5_STATE
    for h in range(bd_ref.shape[0]):
        rows = slice(h * S5_HALF_CH, (h + 1) * S5_HALF_CH)
        for k, bb in enumerate((bb_re, bb_im)):
            tiled = jnp.dot(bb[rows, :].astype(BF16), spread, preferred_element_type=F32)
            bd_ref[h, :, k * half_states:(k + 1) * half_states] = jnp.where(bmask, tiled, 0.0).astype(BF16)
        for k, (ct, sign) in enumerate(((cre_t, 1.0), (cim_t, -1.0))):
            tiled = jnp.concatenate([ct[:, rows]] * reps, axis=0)
            cd_ref[h, k * half_states:(k + 1) * half_states, :] = jnp.where(cmask, sign * tiled, 0.0).astype(BF16)


def _s5_prep(a_re, a_im, log_dt, b_re, b_im, c_re, c_im):
    g, p = a_re.shape
    hch = b_re.shape[2]
    assert hch == S5_GROUP and p == S5_STATE
    width = g * hch
    halves = width // S5_HALF_CH
    half_states = (S5_HALF_CH // hch) * p
    rep = lambda a: jnp.repeat(a, hch, axis=0)
    ldt = log_dt.reshape(g, 1)
    bt = lambda b: jnp.transpose(b, (0, 2, 1)).reshape(width, p)
    ct = lambda c: jnp.transpose(c, (2, 0, 1)).reshape(p, width)
    lam, bd, cd = pl.pallas_call(
        _s5_prep_kernel,
        out_shape=(jax.ShapeDtypeStruct((2, g, p), F32),
                   jax.ShapeDtypeStruct((halves, S5_HALF_CH, 2 * half_states), BF16),
                   jax.ShapeDtypeStruct((halves, 2 * half_states, S5_HALF_CH), BF16)),
        name="s5_prep",
    )(a_re, a_im, ldt, rep(a_re), rep(a_im), rep(ldt), bt(b_re), bt(b_im), ct(c_re), ct(c_im))
    return lam.reshape(2, g * p), bd, cd


def _gate_prep_kernel(wr_ref, wi_ref, or_ref, oi_ref):
    hd = wr_ref.shape[1]
    per_tile = GATE_TILE // hd
    erow = lax.broadcasted_iota(jnp.int32, (hd, GATE_TILE), 0)
    ecol = lax.broadcasted_iota(jnp.int32, (hd, GATE_TILE), 1)
    for w_ref, o_ref in ((wr_ref, or_ref), (wi_ref, oi_ref)):
        for a in range(o_ref.shape[0]):
            for b in range(per_tile):
                place = jnp.where(ecol == erow + b * hd, 1.0, 0.0).astype(BF16)
                o_ref[a, b * hd:(b + 1) * hd, :] = jnp.dot(
                    w_ref[a * per_tile + b].astype(BF16), place, preferred_element_type=F32).astype(BF16)


def _gate_prep(w_r, w_i):
    heads, hd, _ = w_r.shape
    tiles = heads * hd // GATE_TILE
    shp = jax.ShapeDtypeStruct((tiles, GATE_TILE, GATE_TILE), BF16)
    return pl.pallas_call(_gate_prep_kernel, out_shape=(shp, shp), name="gate_prep")(w_r, w_i)


def _stream_cast(mats, stages, sem_ref):
    chunks = []
    used = [0] * len(stages)
    for src, dst, si in mats:
        rows = stages[si].shape[1]
        assert src.shape[0] % rows == 0 and src.shape[1] <= stages[si].shape[2]
        for r0 in range(0, src.shape[0], rows):
            slot = used[si] % WEIGHT_STAGE_SLOTS
            used[si] += 1
            chunks.append((src, dst, si, r0, rows, slot))

    def copy(k):
        src, _, si, r0, rows, slot = chunks[k]
        return pltpu.make_async_copy(src.at[pl.ds(r0, rows), :],
                                     stages[si].at[slot, :, pl.ds(0, src.shape[1])],
                                     sem_ref.at[si * WEIGHT_STAGE_SLOTS + slot])

    ahead = WEIGHT_STAGE_SLOTS - 1
    for k in range(min(ahead, len(chunks))):
        copy(k).start()
    for k, (src, dst, si, r0, rows, slot) in enumerate(chunks):
        if k + ahead < len(chunks):
            copy(k + ahead).start()
        copy(k).wait()
        dst[r0:r0 + rows, :] = stages[si][slot, :, 0:src.shape[1]].astype(BF16)


def _weight_scratch(weights, stage_shapes):
    return ([pltpu.VMEM(w.shape, BF16) for w in weights]
            + [pltpu.VMEM((WEIGHT_STAGE_SLOTS,) + tuple(s), F32) for s in stage_shapes]
            + [pltpu.SemaphoreType.DMA((WEIGHT_STAGE_SLOTS * len(stage_shapes),))])


def _ffn_kernel(x_ref, xn_ref, mod_ref, modn_ref, g_ref, fg_ref, wup_hbm, wdn_hbm, o_ref,
                wup_ref, wdn_ref, stage_up_ref, stage_dn_ref, sem_ref, u_ref, h0_ref, *, mod_base, final_norm):
    d = x_ref.shape[1]
    f = wdn_ref.shape[0]
    c0 = h0_ref.shape[1]

    def modulated(xv, m_ref):
        sh = m_ref[:, (mod_base + 0) * d:(mod_base + 1) * d]
        sc = m_ref[:, (mod_base + 1) * d:(mod_base + 2) * d]
        return (_rms_norm(xv, g_ref[...]) * (1.0 + sc) + sh).astype(BF16)

    @pl.when(pl.program_id(0) == 0)
    def _():
        _stream_cast([(wup_hbm, wup_ref, 0), (wdn_hbm, wdn_ref, 1)], [stage_up_ref, stage_dn_ref], sem_ref)
        u0 = modulated(x_ref[...], mod_ref)
        u_ref[...] = u0
        h0_ref[...] = jnp.dot(u0, wup_ref[:, :c0], preferred_element_type=F32)

    x = x_ref[...]
    gt = mod_ref[:, (mod_base + 2) * d:(mod_base + 3) * d]
    h0 = h0_ref[...]
    h_rest = jnp.dot(u_ref[...], wup_ref[:, c0:], preferred_element_type=F32)
    un = modulated(xn_ref[...], modn_ref)
    u_ref[...] = un
    h0_ref[...] = jnp.dot(un, wup_ref[:, :c0], preferred_element_type=F32)
    a = jnp.concatenate([h0, h_rest[:, :f - c0]], axis=1)
    b = h_rest[:, f - c0:]
    act = (a * _sigmoid(a)) * b
    y = jnp.dot(act.astype(BF16), wdn_ref[...], preferred_element_type=F32)
    out = x + (0.5 * gt) * y
    if final_norm:
        out = _rms_norm(out, fg_ref[...])
    o_ref[...] = out


def _ffn(x2d, mod3, norm_g, w_up, w_down, final_g, *, seq, mod_base, final_norm):
    t, d = x2d.shape
    f = w_down.shape[0]
    tm = FFN_TM
    tiles_per_seq = seq // tm
    n_tiles = t // tm
    nxt = lambda i: jnp.minimum(i + 1, n_tiles - 1)
    kern = functools.partial(_ffn_kernel, mod_base=mod_base, final_norm=final_norm)
    return pl.pallas_call(
        kern,
        grid=(n_tiles,),
        in_specs=[pl.BlockSpec((tm, d), lambda i: (i, 0)),
                  pl.BlockSpec((tm, d), lambda i: (nxt(i), 0), pipeline_mode=pl.Buffered(1)),
                  pl.BlockSpec((None, 1, mod3.shape[2]), lambda i: (i // tiles_per_seq, 0, 0)),
                  pl.BlockSpec((None, 1, mod3.shape[2]), lambda i: (nxt(i) // tiles_per_seq, 0, 0)),
                  _const_spec((1, d)),
                  _const_spec((1, d)),
                  pl.BlockSpec(memory_space=pl.ANY),
                  pl.BlockSpec(memory_space=pl.ANY)],
        out_specs=pl.BlockSpec((tm, d), lambda i: (i, 0)),
        out_shape=jax.ShapeDtypeStruct((t, d), F32),
        scratch_shapes=_weight_scratch([w_up, w_down], [(d // 64, 2 * f), (f // 32, d)])
                       + [pltpu.VMEM((tm, d), BF16),
                          pltpu.VMEM((tm, FFN_HEAD_COLS), F32)],
        compiler_params=pltpu.CompilerParams(dimension_semantics=("arbitrary",),
                                             vmem_limit_bytes=VMEM_LIMIT),
        name="ffn_final" if final_norm else "ffn",
    )(x2d, x2d, mod3, mod3, norm_g.reshape(1, d), final_g.reshape(1, d), w_up, w_down)


def _lru_scan(a_ref, u_ref, carry_ref):
    ts = u_ref.shape[0]
    h = carry_ref[...]
    for t in range(ts // SUBLANES):
        sl = slice(t * SUBLANES, (t + 1) * SUBLANES)
        h = a_ref[sl, :] * h + u_ref[sl, :]
        u_ref[sl, :] = h
    carry_ref[...] = h


def _s5_scan(u_ref, lam_ref, carry_ref, halves):
    ts = u_ref.shape[0]
    half_states = lam_ref.shape[1] // halves
    w = S5_LANE_CHUNK
    for h in range(halves):
        for j in range(half_states // w):
            cre = slice(2 * half_states * h + j * w, 2 * half_states * h + (j + 1) * w)
            cim = slice(2 * half_states * h + half_states + j * w, 2 * half_states * h + half_states + (j + 1) * w)
            lcs = slice(half_states * h + j * w, half_states * h + (j + 1) * w)
            lr = jnp.broadcast_to(lam_ref[0:1, lcs], (SUBLANES, w))
            li = jnp.broadcast_to(lam_ref[1:2, lcs], (SUBLANES, w))
            hr = carry_ref[:, cre]
            hi = carry_ref[:, cim]
            for t in range(ts // SUBLANES):
                sl = slice(t * SUBLANES, (t + 1) * SUBLANES)
                hr, hi = (u_ref[sl, cre] + (lr * hr - li * hi),
                          u_ref[sl, cim] + (lr * hi + li * hr))
                u_ref[sl, cre] = hr
                u_ref[sl, cim] = hi
            carry_ref[:, cre] = hr
            carry_ref[:, cim] = hi


def _mixer_kernel(mod_ref, g_ref, bin_ref, cw_ref, cb_ref,
                  wr_ref, br_ref, wi_ref, bi_ref, lamp_ref,
                  bd_ref, cd_ref, lam_ref, sd_ref, gb_ref,
                  win_hbm, pa_hbm, gw_hbm, pb_hbm, wo_hbm, x_hbm,
                  o_hbm,
                  win_ref, pa_ref, gw_ref, pb_ref, wo_ref, stage_in_ref, stage_sq_ref, sem_ref,
                  xt_ref, ot_ref, sem_x_ref, sem_o_ref,
                  xpad_ref, a_ref, u_ref, s5_ref, hl_ref, hs_ref):
    _, tt, nb, d = xt_ref.shape
    ts = nb * tt
    w5 = sd_ref.shape[1]
    halves = bd_ref.shape[0]
    hs2 = bd_ref.shape[2]
    pad = (CONV_WIDTH - 1) * nb
    step = pl.program_id(0)
    n_steps = pl.num_programs(0)
    slot = step % 2

    def x_copies(tile, sl):
        return [pltpu.make_async_copy(x_hbm.at[b, pl.ds(tile * tt, tt), :], xt_ref.at[sl, :, b, :],
                                      sem_x_ref.at[sl]) for b in range(nb)]

    def o_copies(tile, sl):
        return [pltpu.make_async_copy(ot_ref.at[sl, :, b, :], o_hbm.at[b, pl.ds(tile * tt, tt), :],
                                      sem_o_ref.at[sl]) for b in range(nb)]

    @pl.when(step == 0)
    def _():
        for cp in x_copies(0, 0):
            cp.start()
        xpad_ref[0:pad, :] = jnp.zeros((pad, d), F32)
        hl_ref[...] = jnp.zeros_like(hl_ref)
        hs_ref[...] = jnp.zeros_like(hs_ref)
        _stream_cast([(win_hbm, win_ref, 0), (pa_hbm, pa_ref, 1), (gw_hbm, gw_ref, 1), (pb_hbm, pb_ref, 1),
                      (wo_hbm, wo_ref, 1)], [stage_in_ref, stage_sq_ref], sem_ref)

    @pl.when(step + 1 < n_steps)
    def _():
        for cp in x_copies(step + 1, 1 - slot):
            cp.start()

    for cp in x_copies(step, slot):
        cp.wait()

    def per_batch(v):
        return v[None, :, :]

    x3 = xt_ref[slot]
    sh = per_batch(mod_ref[:, 3 * d:4 * d])
    sc = per_batch(mod_ref[:, 4 * d:5 * d])
    gt = per_batch(mod_ref[:, 5 * d:6 * d])
    u_bf = (_rms_norm(x3, g_ref[...]) * (1.0 + sc) + sh).reshape(ts, d).astype(BF16)

    def in_proj(lo, hi):
        return jnp.dot(u_bf, win_ref[:, lo:hi], preferred_element_type=F32) + bin_ref[:, lo:hi]

    xpad_ref[pad:pad + ts, :] = in_proj(0, d)
    xb = in_proj(2 * d, 2 * d + w5)
    xb_bf = xb.astype(BF16)
    for h in range(halves):
        s5_ref[:, h * hs2:(h + 1) * hs2] = jnp.dot(
            xb_bf[:, h * S5_HALF_CH:(h + 1) * S5_HALF_CH], bd_ref[h], preferred_element_type=F32)
    ga_pre = in_proj(d, 2 * d)

    xc = cb_ref[...]
    for k in range(CONV_WIDTH):
        xc = xc + xpad_ref[k * nb:k * nb + ts, :] * cw_ref[k:k + 1, :]
    xpad_ref[0:pad, :] = xpad_ref[ts:ts + pad, :]
    xc_bf = xc.astype(BF16)
    gates = []
    for w_ref, b_ref in ((wr_ref, br_ref), (wi_ref, bi_ref)):
        cols = [jnp.dot(xc_bf[:, j * GATE_TILE:(j + 1) * GATE_TILE], w_ref[j], preferred_element_type=F32)
                for j in range(w_ref.shape[0])]
        gates.append(_sigmoid(jnp.concatenate(cols, axis=1) + b_ref[...]))
    r_gate, i_gate = gates
    mga_pre = in_proj(2 * d + w5, 3 * d + w5)
    nl = -lamp_ref[...]
    softplus = jnp.maximum(nl, 0.0) + jnp.log1p(jnp.exp(-jnp.abs(nl)))
    log_a = (-LRU_C * softplus) * r_gate
    a_gate = jnp.exp(log_a)
    a_ref[...] = a_gate
    one_m_a2 = 1.0 - a_gate * a_gate
    mult = jnp.where(one_m_a2 > 0.0, one_m_a2 * lax.rsqrt(one_m_a2), 0.0)
    u_ref[...] = mult * (i_gate * xc)
    _lru_scan(a_ref, u_ref, hl_ref)
    ya = u_ref[...] * _gelu_tanh(ga_pre)
    proj_a = jnp.dot(ya.astype(BF16), pa_ref[...], preferred_element_type=F32)
    mgb_pre = in_proj(3 * d + w5, 4 * d + w5)

    _s5_scan(s5_ref, lam_ref, hs_ref, halves)
    ys = [jnp.dot(s5_ref[:, h * hs2:(h + 1) * hs2].astype(BF16), cd_ref[h], preferred_element_type=F32)
          for h in range(halves)]
    yb = _gelu_tanh(jnp.concatenate(ys, axis=1) + sd_ref[...] * xb)
    yb = yb * _sigmoid(jnp.dot(yb.astype(BF16), gw_ref[...], preferred_element_type=F32) + gb_ref[...])
    proj_b = jnp.dot(yb.astype(BF16), pb_ref[...], preferred_element_type=F32)

    m = _sigmoid(mga_pre) * proj_a + _sigmoid(mgb_pre) * proj_b
    y = jnp.dot(m.astype(BF16), wo_ref[...], preferred_element_type=F32)
    out = x3 + gt * y.reshape(tt, nb, d)

    @pl.when(step >= 2)
    def _():
        for cp in o_copies(step - 2, slot):
            cp.wait()

    ot_ref[slot] = out
    for cp in o_copies(step, slot):
        cp.start()

    @pl.when(step == n_steps - 1)
    def _():
        for cp in o_copies(step - 1, 1 - slot):
            cp.wait()
        for cp in o_copies(step, slot):
            cp.wait()


def _mixer(x3d, mod2, norm_g, w_in, b_in, conv_w, conv_b, wr_bd, b_r, wi_bd, b_i, lam_p, proj_a,
           bd, cd, lam, s5_d, glu_w, glu_b, proj_b, w_out):
    bsz, seq, d = x3d.shape
    tt = MIX_TT
    ts = tt * bsz
    n_states2 = bd.shape[0] * bd.shape[2]
    row = lambda v: v.reshape(1, -1)
    consts = [mod2, row(norm_g), row(b_in), conv_w, row(conv_b), wr_bd, row(b_r), wi_bd, row(b_i),
              row(lam_p), bd, cd, lam, row(s5_d), row(glu_b)]
    streamed = [w_in, proj_a, glu_w, proj_b, w_out]
    assert seq // tt >= 2, "the write-back drain expects at least two sequence tiles"
    return pl.pallas_call(
        _mixer_kernel,
        grid=(seq // tt,),
        in_specs=[_const_spec(c.shape) for c in consts]
                 + [pl.BlockSpec(memory_space=pl.ANY)] * (len(streamed) + 1),
        out_specs=pl.BlockSpec(memory_space=pl.ANY),
        out_shape=jax.ShapeDtypeStruct((bsz, seq, d), F32),
        scratch_shapes=_weight_scratch(streamed, [(d // 32, w_in.shape[1]), (d // 8, d)])
                       + [pltpu.VMEM((2, tt, bsz, d), F32),
                          pltpu.VMEM((2, tt, bsz, d), F32),
                          pltpu.SemaphoreType.DMA((2,)),
                          pltpu.SemaphoreType.DMA((2,)),
                        pltpu.VMEM((ts + (CONV_WIDTH - 1) * bsz, d), F32),
                        pltpu.VMEM((ts, d), F32),
                        pltpu.VMEM((ts, d), F32),
                        pltpu.VMEM((ts, n_states2), F32),
                        pltpu.VMEM((bsz, d), F32),
                        pltpu.VMEM((bsz, n_states2), F32)],
        compiler_params=pltpu.CompilerParams(dimension_semantics=("arbitrary",),
                                             vmem_limit_bytes=VMEM_LIMIT),
        name="mixer",
    )(*consts, *streamed, x3d)


def kernel(x, c, mod_w, mod_b, norm1_g, ffn1_w_up, ffn1_w_down, norm2_g, w_in, b_in, conv_w, conv_b, lru_w_r, lru_b_r, lru_w_i, lru_b_i, lru_lambda, proj_a, s5_a_re, s5_a_im, s5_log_dt, s5_b_re, s5_b_im, s5_c_re, s5_c_im, s5_d, glu_w, glu_b, proj_b, w_out, norm3_g, ffn2_w_up, ffn2_w_down, final_g):
    bsz, seq, d = x.shape
    assert bsz == SUBLANES, "time-major layout puts the batch on the 8 sublanes"
    depth = mod_w.shape[0]
    for l in range(depth):
        mod2 = _mod(c, mod_w[l], mod_b[l])
        mod3 = mod2.reshape(bsz, 1, N_MOD * d)
        lam, bd, cd = _s5_prep(s5_a_re[l], s5_a_im[l], s5_log_dt[l], s5_b_re[l], s5_b_im[l],
                               s5_c_re[l], s5_c_im[l])
        wr_bd, wi_bd = _gate_prep(lru_w_r[l], lru_w_i[l])
        last = l == depth - 1
        x2d = _ffn(x.reshape(bsz * seq, d), mod3, norm1_g[l], ffn1_w_up[l], ffn1_w_down[l], final_g,
                   seq=seq, mod_base=0, final_norm=False)
        x3d = _mixer(x2d.reshape(bsz, seq, d), mod2, norm2_g[l], w_in[l], b_in[l], conv_w[l], conv_b[l],
                     wr_bd, lru_b_r[l], wi_bd, lru_b_i[l], lru_lambda[l], proj_a[l],
                     bd, cd, lam, s5_d[l], glu_w[l], glu_b[l], proj_b[l], w_out[l])
        x = _ffn(x3d.reshape(bsz * seq, d), mod3, norm3_g[l], ffn2_w_up[l], ffn2_w_down[l], final_g,
                 seq=seq, mod_base=6, final_norm=last).reshape(bsz, seq, d)
    return x
```

```python
import functools
import math

import jax
import jax.numpy as jnp
from jax import lax
from jax.experimental import pallas as pl
from jax.experimental.pallas import tpu as pltpu

F32 = jnp.float32
BF16 = jnp.bfloat16

EPS = 1e-6
LRU_C = 8.0
CONV_WIDTH = 4
S5_GROUP = 16
S5_STATE = 64
MOD_PER_SUB = 3

SUBLANES = 8
LANES = 128
GATE_TILE = 256
S5_HALF_CH = 256
VMEM_LIMIT = 56 * 1024 * 1024

WEIGHT_STAGE_SLOTS = 4
FFN_TM = 512
FFN_HEAD_COLS = 256
MIX_TT = 64
S5_LANE_CHUNK = 512


def _sigmoid(x):
    return 0.5 * jnp.tanh(0.5 * x) + 0.5


def _gelu_tanh(x):
    c = math.sqrt(2.0 / math.pi)
    return 0.5 * x * (1.0 + jnp.tanh(c * (x + 0.044715 * (x * x * x))))


def _rms_norm(x, g):
    ms = jnp.mean(x * x, axis=-1, keepdims=True)
    return x * lax.rsqrt(ms + EPS) * g


def _const_spec(shape):
    nd = len(shape)
    return pl.BlockSpec(shape, lambda *_: (0,) * nd, pipeline_mode=pl.Buffered(1))


def _mod_kernel(c_ref, *refs):
    n_sub = (len(refs) - 1) // 2
    o_ref = refs[-1]
    c = c_ref[...]
    ca = (c * _sigmoid(c)).astype(BF16)
    for s in range(n_sub):
        o_ref[s] = jnp.dot(ca, refs[s][...].astype(BF16), preferred_element_type=F32) + refs[n_sub + s][...]


def _mod(c, mod_w, mod_b):
    bsz, d = c.shape
    n = mod_w.shape[1]
    n_sub = n // (MOD_PER_SUB * d)
    w_spec = lambda s: pl.BlockSpec((d, d), lambda j: (0, s * MOD_PER_SUB + j))
    b_spec = lambda s: pl.BlockSpec((1, d), lambda j: (0, s * MOD_PER_SUB + j))
    return pl.pallas_call(
        _mod_kernel,
        grid=(MOD_PER_SUB,),
        in_specs=[pl.BlockSpec((bsz, d), lambda j: (0, 0))]
                 + [w_spec(s) for s in range(n_sub)] + [b_spec(s) for s in range(n_sub)],
        out_specs=pl.BlockSpec((n_sub, bsz, d), lambda j: (0, 0, j)),
        out_shape=jax.ShapeDtypeStruct((n_sub, bsz, MOD_PER_SUB * d), F32),
        compiler_params=pltpu.CompilerParams(dimension_semantics=("arbitrary",),
                                             vmem_limit_bytes=VMEM_LIMIT),
        name="mod",
    )(c, *([mod_w] * n_sub), *([mod_b.reshape(1, n)] * n_sub))


def _cmul(ar, ai, br, bi):
    return ar * br - ai * bi, ar * bi + ai * br


def _s5_discretise(a_re, a_im, log_dt):
    dt = jnp.exp(log_dt)
    mag = jnp.exp(a_re * dt)
    lr = mag * jnp.cos(a_im * dt)
    li = mag * jnp.sin(a_im * dt)
    den = a_re * a_re + a_im * a_im
    nr = lr - 1.0
    cr = (nr * a_re + li * a_im) / den
    ci = (li * a_re - nr * a_im) / den
    return lr, li, cr, ci


def _s5_prep_kernel(are_g, aim_g, ldt_g, are_r, aim_r, ldt_r, bre_t, bim_t, cre_t, cim_t,
                    lam_ref, bd_ref, cd_ref):
    lr, li, _, _ = _s5_discretise(are_g[...], aim_g[...], ldt_g[...])
    lam_ref[0] = lr
    lam_ref[1] = li

    _, _, cr, ci = _s5_discretise(are_r[...], aim_r[...], ldt_r[...])
    bb_re, bb_im = _cmul(cr, ci, bre_t[...], bim_t[...])
    half_states = (S5_HALF_CH // S5_GROUP) * S5_STATE
    erow = lax.broadcasted_iota(jnp.int32, (S5_STATE, half_states), 0)
    ecol = lax.broadcasted_iota(jnp.int32, (S5_STATE, half_states), 1)
    spread = jnp.where((ecol % S5_STATE) == erow, 1.0, 0.0).astype(BF16)
    brow_g = lax.broadcasted_iota(jnp.int32, (S5_HALF_CH, half_states), 0) // S5_GROUP
    bcol_g = lax.broadcasted_iota(jnp.int32, (S5_HALF_CH, half_states), 1) // S5_STATE
    bmask = brow_g == bcol_g
    crow_g = lax.broadcasted_iota(jnp.int32, (half_states, S5_HALF_CH), 0) // S5_STATE
    ccol_g = lax.broadcasted_iota(jnp.int32, (half_states, S5_HALF_CH), 1) // S5_GROUP
    cmask = crow_g == ccol_g
    reps = half_states // S5_STATE
    for h in range(bd_ref.shape[0]):
        rows = slice(h * S5_HALF_CH, (h + 1) * S5_HALF_CH)
        for k, bb in enumerate((bb_re, bb_im)):
            tiled = jnp.dot(bb[rows, :].astype(BF16), spread, preferred_element_type=F32)
            bd_ref[h, :, k * half_states:(k + 1) * half_states] = jnp.where(bmask, tiled, 0.0).astype(BF16)
        for k, (ct, sign) in enumerate(((cre_t, 1.0), (cim_t, -1.0))):
            tiled = jnp.concatenate([ct[:, rows]] * reps, axis=0)
            cd_ref[h, k * half_states:(k + 1) * half_states, :] = jnp.where(cmask, sign * tiled, 0.0).astype(BF16)


def _s5_prep(a_re, a_im, log_dt, b_re, b_im, c_re, c_im):
    g, p = a_re.shape
    hch = b_re.shape[2]
    assert hch == S5_GROUP and p == S5_STATE
    width = g * hch
    halves = width // S5_HALF_CH
    half_states = (S5_HALF_CH // hch) * p
    rep = lambda a: jnp.repeat(a, hch, axis=0)
    ldt = log_dt.reshape(g, 1)
    bt = lambda b: jnp.transpose(b, (0, 2, 1)).reshape(width, p)
    ct = lambda c: jnp.transpose(c, (2, 0, 1)).reshape(p, width)
    lam, bd, cd = pl.pallas_call(
        _s5_prep_kernel,
        out_shape=(jax.ShapeDtypeStruct((2, g, p), F32),
                   jax.ShapeDtypeStruct((halves, S5_HALF_CH, 2 * half_states), BF16),
                   jax.ShapeDtypeStruct((halves, 2 * half_states, S5_HALF_CH), BF16)),
        name="s5_prep",
    )(a_re, a_im, ldt, rep(a_re), rep(a_im), rep(ldt), bt(b_re), bt(b_im), ct(c_re), ct(c_im))
    return lam.reshape(2, g * p), bd, cd


def _gate_prep_kernel(wr_ref, wi_ref, or_ref, oi_ref):
    hd = wr_ref.shape[1]
    per_tile = GATE_TILE // hd
    erow = lax.broadcasted_iota(jnp.int32, (hd, GATE_TILE), 0)
    ecol = lax.broadcasted_iota(jnp.int32, (hd, GATE_TILE), 1)
    for w_ref, o_ref in ((wr_ref, or_ref), (wi_ref, oi_ref)):
        for a in range(o_ref.shape[0]):
            for b in range(per_tile):
                place = jnp.where(ecol == erow + b * hd, 1.0, 0.0).astype(BF16)
                o_ref[a, b * hd:(b + 1) * hd, :] = jnp.dot(
                    w_ref[a * per_tile + b].astype(BF16), place, preferred_element_type=F32).astype(BF16)


def _gate_prep(w_r, w_i):
    heads, hd, _ = w_r.shape
    tiles = heads * hd // GATE_TILE
    shp = jax.ShapeDtypeStruct((tiles, GATE_TILE, GATE_TILE), BF16)
    return pl.pallas_call(_gate_prep_kernel, out_shape=(shp, shp), name="gate_prep")(w_r, w_i)


def _stream_cast(mats, stages, sem_ref):
    chunks = []
    used = [0] * len(stages)
    for src, dst, si in mats:
        rows = stages[si].shape[1]
        assert src.shape[0] % rows == 0 and src.shape[1] <= stages[si].shape[2]
        for r0 in range(0, src.shape[0], rows):
            slot = used[si] % WEIGHT_STAGE_SLOTS
            used[si] += 1
            chunks.append((src, dst, si, r0, rows, slot))

    def copy(k):
        src, _, si, r0, rows, slot = chunks[k]
        return pltpu.make_async_copy(src.at[pl.ds(r0, rows), :],
                                     stages[si].at[slot, :, pl.ds(0, src.shape[1])],
                                     sem_ref.at[si * WEIGHT_STAGE_SLOTS + slot])

    ahead = WEIGHT_STAGE_SLOTS - 1
    for k in range(min(ahead, len(chunks))):
        copy(k).start()
    for k, (src, dst, si, r0, rows, slot) in enumerate(chunks):
        if k + ahead < len(chunks):
            copy(k + ahead).start()
        copy(k).wait()
        dst[r0:r0 + rows, :] = stages[si][slot, :, 0:src.shape[1]].astype(BF16)


def _weight_scratch(weights, stage_shapes):
    return ([pltpu.VMEM(w.shape, BF16) for w in weights]
            + [pltpu.VMEM((WEIGHT_STAGE_SLOTS,) + tuple(s), F32) for s in stage_shapes]
            + [pltpu.SemaphoreType.DMA((WEIGHT_STAGE_SLOTS * len(stage_shapes),))])


def _ffn_kernel(x_ref, xn_ref, mod_ref, g_ref, fg_ref, wup_hbm, wdn_hbm, o_ref,
                wup_ref, wdn_ref, stage_up_ref, stage_dn_ref, sem_ref, u_ref, h0_ref, *,
                tiles_per_seq, final_norm):
    d = x_ref.shape[1]
    f = wdn_ref.shape[0]
    c0 = h0_ref.shape[1]
    step = pl.program_id(0)
    batch = step // tiles_per_seq
    batch_next = jnp.minimum(step + 1, pl.num_programs(0) - 1) // tiles_per_seq

    def mod_row(b, k):
        return mod_ref[pl.ds(b, 1), k * d:(k + 1) * d]

    def modulated(xv, b):
        return (_rms_norm(xv, g_ref[...]) * (1.0 + mod_row(b, 1)) + mod_row(b, 0)).astype(BF16)

    @pl.when(step == 0)
    def _():
        _stream_cast([(wup_hbm, wup_ref, 0), (wdn_hbm, wdn_ref, 1)], [stage_up_ref, stage_dn_ref], sem_ref)
        u0 = modulated(x_ref[...], batch)
        u_ref[...] = u0
        h0_ref[...] = jnp.dot(u0, wup_ref[:, :c0], preferred_element_type=F32)

    x = x_ref[...]
    gt = mod_row(batch, 2)
    h0 = h0_ref[...]
    h_rest = jnp.dot(u_ref[...], wup_ref[:, c0:], preferred_element_type=F32)
    un = modulated(xn_ref[...], batch_next)
    u_ref[...] = un
    h0_ref[...] = jnp.dot(un, wup_ref[:, :c0], preferred_element_type=F32)
    a = jnp.concatenate([h0, h_rest[:, :f - c0]], axis=1)
    b = h_rest[:, f - c0:]
    act = (a * _sigmoid(a)) * b
    y = jnp.dot(act.astype(BF16), wdn_ref[...], preferred_element_type=F32)
    out = x + (0.5 * gt) * y
    if final_norm:
        out = _rms_norm(out, fg_ref[...])
    o_ref[...] = out


def _ffn(x2d, mod, sub, norm_g, w_up, w_down, final_g, *, seq, final_norm):
    t, d = x2d.shape
    f = w_down.shape[0]
    tm = FFN_TM
    tiles_per_seq = seq // tm
    n_tiles = t // tm
    nxt = lambda i: jnp.minimum(i + 1, n_tiles - 1)
    kern = functools.partial(_ffn_kernel, tiles_per_seq=tiles_per_seq, final_norm=final_norm)
    return pl.pallas_call(
        kern,
        grid=(n_tiles,),
        in_specs=[pl.BlockSpec((tm, d), lambda i: (i, 0)),
                  pl.BlockSpec((tm, d), lambda i: (nxt(i), 0)),
                  pl.BlockSpec((None,) + mod.shape[1:], lambda i: (sub, 0, 0), pipeline_mode=pl.Buffered(1)),
                  _const_spec((1, d)),
                  _const_spec((1, d)),
                  pl.BlockSpec(memory_space=pl.ANY),
                  pl.BlockSpec(memory_space=pl.ANY)],
        out_specs=pl.BlockSpec((tm, d), lambda i: (i, 0)),
        out_shape=jax.ShapeDtypeStruct((t, d), F32),
        scratch_shapes=_weight_scratch([w_up, w_down], [(d // 16, 2 * f), (f // 8, d)])
                       + [pltpu.VMEM((tm, d), BF16),
                          pltpu.VMEM((tm, FFN_HEAD_COLS), F32)],
        compiler_params=pltpu.CompilerParams(dimension_semantics=("arbitrary",),
                                             vmem_limit_bytes=VMEM_LIMIT),
        name="ffn_final" if final_norm else "ffn",
    )(x2d, x2d, mod, norm_g.reshape(1, d), final_g.reshape(1, d), w_up, w_down)


def _lru_scan(a_ref, u_ref, carry_ref):
    ts = u_ref.shape[0]
    h = carry_ref[...]
    for t in range(ts // SUBLANES):
        sl = slice(t * SUBLANES, (t + 1) * SUBLANES)
        h = a_ref[sl, :] * h + u_ref[sl, :]
        u_ref[sl, :] = h
    carry_ref[...] = h


def _s5_scan(u_ref, lam_ref, carry_ref, halves):
    ts = u_ref.shape[0]
    half_states = lam_ref.shape[1] // halves
    w = S5_LANE_CHUNK
    for h in range(halves):
        for j in range(half_states // w):
            cre = slice(2 * half_states * h + j * w, 2 * half_states * h + (j + 1) * w)
            cim = slice(2 * half_states * h + half_states + j * w, 2 * half_states * h + half_states + (j + 1) * w)
            lcs = slice(half_states * h + j * w, half_states * h + (j + 1) * w)
            lr = jnp.broadcast_to(lam_ref[0:1, lcs], (SUBLANES, w))
            li = jnp.broadcast_to(lam_ref[1:2, lcs], (SUBLANES, w))
            hr = carry_ref[:, cre]
            hi = carry_ref[:, cim]
            for t in range(ts // SUBLANES):
                sl = slice(t * SUBLANES, (t + 1) * SUBLANES)
                hr, hi = (u_ref[sl, cre] + (lr * hr - li * hi),
                          u_ref[sl, cim] + (lr * hi + li * hr))
                u_ref[sl, cre] = hr
                u_ref[sl, cim] = hi
            carry_ref[:, cre] = hr
            carry_ref[:, cim] = hi


def _mixer_kernel(mod_ref, g_ref, bin_ref, cw_ref, cb_ref,
                  wr_ref, br_ref, wi_ref, bi_ref, lamp_ref,
                  bd_ref, cd_ref, lam_ref, sd_ref, gb_ref,
                  win_hbm, pa_hbm, gw_hbm, pb_hbm, wo_hbm, x_hbm,
                  o_hbm,
                  win_ref, pa_ref, gw_ref, pb_ref, wo_ref, stage_in_ref, stage_sq_ref, sem_ref,
                  xt_ref, ot_ref, sem_x_ref, sem_o_ref,
                  xpad_ref, a_ref, u_ref, s5_ref, hl_ref, hs_ref):
    _, tt, nb, d = xt_ref.shape
    ts = nb * tt
    w5 = sd_ref.shape[1]
    halves = bd_ref.shape[0]
    hs2 = bd_ref.shape[2]
    pad = (CONV_WIDTH - 1) * nb
    step = pl.program_id(0)
    n_steps = pl.num_programs(0)
    slot = step % 2

    def x_copies(tile, sl):
        return [pltpu.make_async_copy(x_hbm.at[b, pl.ds(tile * tt, tt), :], xt_ref.at[sl, :, b, :],
                                      sem_x_ref.at[sl]) for b in range(nb)]

    def o_copies(tile, sl):
        return [pltpu.make_async_copy(ot_ref.at[sl, :, b, :], o_hbm.at[b, pl.ds(tile * tt, tt), :],
                                      sem_o_ref.at[sl]) for b in range(nb)]

    @pl.when(step == 0)
    def _():
        for cp in x_copies(0, 0):
            cp.start()
        xpad_ref[0:pad, :] = jnp.zeros((pad, d), F32)
        hl_ref[...] = jnp.zeros_like(hl_ref)
        hs_ref[...] = jnp.zeros_like(hs_ref)
        _stream_cast([(win_hbm, win_ref, 0), (pa_hbm, pa_ref, 1), (gw_hbm, gw_ref, 1), (pb_hbm, pb_ref, 1),
                      (wo_hbm, wo_ref, 1)], [stage_in_ref, stage_sq_ref], sem_ref)

    @pl.when(step + 1 < n_steps)
    def _():
        for cp in x_copies(step + 1, 1 - slot):
            cp.start()

    for cp in x_copies(step, slot):
        cp.wait()

    def per_batch(v):
        return v[None, :, :]

    x3 = xt_ref[slot]
    sh = per_batch(mod_ref[:, 0:d])
    sc = per_batch(mod_ref[:, d:2 * d])
    gt = per_batch(mod_ref[:, 2 * d:3 * d])
    u_bf = (_rms_norm(x3, g_ref[...]) * (1.0 + sc) + sh).reshape(ts, d).astype(BF16)

    def in_proj(lo, hi):
        return jnp.dot(u_bf, win_ref[:, lo:hi], preferred_element_type=F32) + bin_ref[:, lo:hi]

    xpad_ref[pad:pad + ts, :] = in_proj(0, d)
    xb = in_proj(2 * d, 2 * d + w5)
    xb_bf = xb.astype(BF16)
    for h in range(halves):
        s5_ref[:, h * hs2:(h + 1) * hs2] = jnp.dot(
            xb_bf[:, h * S5_HALF_CH:(h + 1) * S5_HALF_CH], bd_ref[h], preferred_element_type=F32)
    ga_pre = in_proj(d, 2 * d)

    xc = cb_ref[...]
    for k in range(CONV_WIDTH):
        xc = xc + xpad_ref[k * nb:k * nb + ts, :] * cw_ref[k:k + 1, :]
    xpad_ref[0:pad, :] = xpad_ref[ts:ts + pad, :]
    xc_bf = xc.astype(BF16)
    gates = []
    for w_ref, b_ref in ((wr_ref, br_ref), (wi_ref, bi_ref)):
        cols = [jnp.dot(xc_bf[:, j * GATE_TILE:(j + 1) * GATE_TILE], w_ref[j], preferred_element_type=F32)
                for j in range(w_ref.shape[0])]
        gates.append(_sigmoid(jnp.concatenate(cols, axis=1) + b_ref[...]))
    r_gate, i_gate = gates
    mga_pre = in_proj(2 * d + w5, 3 * d + w5)
    nl = -lamp_ref[...]
    softplus = jnp.maximum(nl, 0.0) + jnp.log1p(jnp.exp(-jnp.abs(nl)))
    neg_log_a = (LRU_C * softplus) * r_gate
    a_gate = jnp.exp(-neg_log_a)
    a_ref[...] = a_gate
    one_m_a2 = jnp.tanh(neg_log_a) * (1.0 + a_gate * a_gate)
    mult = jnp.where(one_m_a2 > 0.0, one_m_a2 * lax.rsqrt(one_m_a2), 0.0)
    u_ref[...] = mult * (i_gate * xc)
    _lru_scan(a_ref, u_ref, hl_ref)
    ya = u_ref[...] * _gelu_tanh(ga_pre)
    proj_a = jnp.dot(ya.astype(BF16), pa_ref[...], preferred_element_type=F32)
    mgb_pre = in_proj(3 * d + w5, 4 * d + w5)

    _s5_scan(s5_ref, lam_ref, hs_ref, halves)
    ys = [jnp.dot(s5_ref[:, h * hs2:(h + 1) * hs2].astype(BF16), cd_ref[h], preferred_element_type=F32)
          for h in range(halves)]
    yb = _gelu_tanh(jnp.concatenate(ys, axis=1) + sd_ref[...] * xb)
    yb = yb * _sigmoid(jnp.dot(yb.astype(BF16), gw_ref[...], preferred_element_type=F32) + gb_ref[...])
    proj_b = jnp.dot(yb.astype(BF16), pb_ref[...], preferred_element_type=F32)

    m = _sigmoid(mga_pre) * proj_a + _sigmoid(mgb_pre) * proj_b
    y = jnp.dot(m.astype(BF16), wo_ref[...], preferred_element_type=F32)
    out = x3 + gt * y.reshape(tt, nb, d)

    @pl.when(step >= 2)
    def _():
        for cp in o_copies(step - 2, slot):
            cp.wait()

    ot_ref[slot] = out
    for cp in o_copies(step, slot):
        cp.start()

    @pl.when(step == n_steps - 1)
    def _():
        for cp in o_copies(step - 1, 1 - slot):
            cp.wait()
        for cp in o_copies(step, slot):
            cp.wait()


def _mixer(x3d, mod, sub, norm_g, w_in, b_in, conv_w, conv_b, wr_bd, b_r, wi_bd, b_i, lam_p, proj_a,
           bd, cd, lam, s5_d, glu_w, glu_b, proj_b, w_out):
    bsz, seq, d = x3d.shape
    tt = MIX_TT
    ts = tt * bsz
    n_states2 = bd.shape[0] * bd.shape[2]
    row = lambda v: v.reshape(1, -1)
    consts = [row(norm_g), row(b_in), conv_w, row(conv_b), wr_bd, row(b_r), wi_bd, row(b_i),
              row(lam_p), bd, cd, lam, row(s5_d), row(glu_b)]
    streamed = [w_in, proj_a, glu_w, proj_b, w_out]
    assert seq // tt >= 2, "the write-back drain expects at least two sequence tiles"
    return pl.pallas_call(
        _mixer_kernel,
        grid=(seq // tt,),
        in_specs=[pl.BlockSpec((None,) + mod.shape[1:], lambda i: (sub, 0, 0), pipeline_mode=pl.Buffered(1))]
                 + [_const_spec(c.shape) for c in consts]
                 + [pl.BlockSpec(memory_space=pl.ANY)] * (len(streamed) + 1),
        out_specs=pl.BlockSpec(memory_space=pl.ANY),
        out_shape=jax.ShapeDtypeStruct((bsz, seq, d), F32),
        scratch_shapes=_weight_scratch(streamed, [(d // 32, w_in.shape[1]), (d // 8, d)])
                       + [pltpu.VMEM((2, tt, bsz, d), F32),
                          pltpu.VMEM((2, tt, bsz, d), F32),
                          pltpu.SemaphoreType.DMA((2,)),
                          pltpu.SemaphoreType.DMA((2,)),
                        pltpu.VMEM((ts + (CONV_WIDTH - 1) * bsz, d), F32),
                        pltpu.VMEM((ts, d), F32),
                        pltpu.VMEM((ts, d), F32),
                        pltpu.VMEM((ts, n_states2), F32),
                        pltpu.VMEM((bsz, d), F32),
                        pltpu.VMEM((bsz, n_states2), F32)],
        compiler_params=pltpu.CompilerParams(dimension_semantics=("arbitrary",),
                                             vmem_limit_bytes=VMEM_LIMIT),
        name="mixer",
    )(mod, *consts, *streamed, x3d)


def kernel(x, c, mod_w, mod_b, norm1_g, ffn1_w_up, ffn1_w_down, norm2_g, w_in, b_in, conv_w, conv_b, lru_w_r, lru_b_r, lru_w_i, lru_b_i, lru_lambda, proj_a, s5_a_re, s5_a_im, s5_log_dt, s5_b_re, s5_b_im, s5_c_re, s5_c_im, s5_d, glu_w, glu_b, proj_b, w_out, norm3_g, ffn2_w_up, ffn2_w_down, final_g):
    bsz, seq, d = x.shape
    assert bsz == SUBLANES, "time-major layout puts the batch on the 8 sublanes"
    depth = mod_w.shape[0]
    for l in range(depth):
        mod = _mod(c, mod_w[l], mod_b[l])
        lam, bd, cd = _s5_prep(s5_a_re[l], s5_a_im[l], s5_log_dt[l], s5_b_re[l], s5_b_im[l],
                               s5_c_re[l], s5_c_im[l])
        wr_bd, wi_bd = _gate_prep(lru_w_r[l], lru_w_i[l])
        last = l == depth - 1
        x2d = _ffn(x.reshape(bsz * seq, d), mod, 0, norm1_g[l], ffn1_w_up[l], ffn1_w_down[l], final_g,
                   seq=seq, final_norm=False)
        x3d = _mixer(x2d.reshape(bsz, seq, d), mod, 1, norm2_g[l], w_in[l], b_in[l], conv_w[l], conv_b[l],
                     wr_bd, lru_b_r[l], wi_bd, lru_b_i[l], lru_lambda[l], proj_a[l],
                     bd, cd, lam, s5_d[l], glu_w[l], glu_b[l], proj_b[l], w_out[l])
        x = _ffn(x3d.reshape(bsz * seq, d), mod, 2, norm3_g[l], ffn2_w_up[l], ffn2_w_down[l], final_g,
                 seq=seq, final_norm=last).reshape(bsz, seq, d)
    return x
```

```python
import functools
import math

import jax
import jax.numpy as jnp
from jax import lax
from jax.experimental import pallas as pl
from jax.experimental.pallas import tpu as pltpu

F32 = jnp.float32
BF16 = jnp.bfloat16

EPS = 1e-6
LRU_C = 8.0
CONV_WIDTH = 4
S5_GROUP = 16
S5_STATE = 64
MOD_PER_SUB = 3

SUBLANES = 8
LANES = 128
GATE_TILE = 256
S5_HALF_CH = 256
VMEM_LIMIT = 56 * 1024 * 1024

WEIGHT_STAGE_SLOTS = 4
FFN_TM = 512
FFN_HEAD_COLS = 256
MIX_TT = 64
S5_LANE_CHUNK = 512


def _sigmoid(x):
    return 0.5 * jnp.tanh(0.5 * x) + 0.5


def _gelu_tanh(x):
    c = math.sqrt(2.0 / math.pi)
    return 0.5 * x * (1.0 + jnp.tanh(c * (x + 0.044715 * (x * x * x))))


def _rms_norm(x, g):
    ms = jnp.mean(x * x, axis=-1, keepdims=True)
    return x * lax.rsqrt(ms + EPS) * g


def _const_spec(shape):
    nd = len(shape)
    return pl.BlockSpec(shape, lambda *_: (0,) * nd, pipeline_mode=pl.Buffered(1))


def _mod_kernel(c_ref, *refs):
    n_sub = (len(refs) - 1) // 2
    o_ref = refs[-1]
    c = c_ref[...]
    ca = (c * _sigmoid(c)).astype(BF16)
    for s in range(n_sub):
        o_ref[s] = jnp.dot(ca, refs[s][...].astype(BF16), preferred_element_type=F32) + refs[n_sub + s][...]


def _mod(c, mod_w, mod_b):
    bsz, d = c.shape
    n = mod_w.shape[1]
    n_sub = n // (MOD_PER_SUB * d)
    w_spec = lambda s: pl.BlockSpec((d, d), lambda j: (0, s * MOD_PER_SUB + j))
    b_spec = lambda s: pl.BlockSpec((1, d), lambda j: (0, s * MOD_PER_SUB + j))
    return pl.pallas_call(
        _mod_kernel,
        grid=(MOD_PER_SUB,),
        in_specs=[pl.BlockSpec((bsz, d), lambda j: (0, 0))]
                 + [w_spec(s) for s in range(n_sub)] + [b_spec(s) for s in range(n_sub)],
        out_specs=pl.BlockSpec((n_sub, bsz, d), lambda j: (0, 0, j)),
        out_shape=jax.ShapeDtypeStruct((n_sub, bsz, MOD_PER_SUB * d), F32),
        compiler_params=pltpu.CompilerParams(dimension_semantics=("arbitrary",),
                                             vmem_limit_bytes=VMEM_LIMIT),
        name="mod",
    )(c, *([mod_w] * n_sub), *([mod_b.reshape(1, n)] * n_sub))


def _cmul(ar, ai, br, bi):
    return ar * br - ai * bi, ar * bi + ai * br


def _s5_discretise(a_re, a_im, log_dt):
    dt = jnp.exp(log_dt)
    mag = jnp.exp(a_re * dt)
    lr = mag * jnp.cos(a_im * dt)
    li = mag * jnp.sin(a_im * dt)
    den = a_re * a_re + a_im * a_im
    nr = lr - 1.0
    cr = (nr * a_re + li * a_im) / den
    ci = (li * a_re - nr * a_im) / den
    return lr, li, cr, ci


def _s5_prep_kernel(are_f, aim_f, ldt_f, bre_n, bim_n, cre_n, cim_n, lam_ref, bd_ref, cd_ref):
    lr, li, cr, ci = _s5_discretise(are_f[...], aim_f[...], ldt_f[...])
    lam_ref[0:1, :] = lr
    lam_ref[1:2, :] = li

    half_states = (S5_HALF_CH // S5_GROUP) * S5_STATE
    shape_sc = (half_states, S5_HALF_CH)
    shape_cs = (S5_HALF_CH, half_states)
    iota = lambda shape, axis: lax.broadcasted_iota(jnp.int32, shape, axis)
    spread_ch = jnp.where(iota((S5_GROUP, S5_HALF_CH), 1) % S5_GROUP == iota((S5_GROUP, S5_HALF_CH), 0), 1.0, 0.0)
    spread_st = jnp.where(iota((S5_STATE, half_states), 1) % S5_STATE == iota((S5_STATE, half_states), 0), 1.0, 0.0)
    same_group_sc = iota(shape_sc, 0) // S5_STATE == iota(shape_sc, 1) // S5_GROUP
    same_group_cs = iota(shape_cs, 0) // S5_GROUP == iota(shape_cs, 1) // S5_STATE
    exact_dot = functools.partial(jnp.dot, precision=lax.Precision.HIGHEST, preferred_element_type=F32)

    for h in range(bd_ref.shape[0]):
        st = slice(h * half_states, (h + 1) * half_states)
        ch = slice(h * S5_HALF_CH, (h + 1) * S5_HALF_CH)
        b_re, b_im = [jnp.where(same_group_sc, exact_dot(b[st, :], spread_ch), 0.0).T for b in (bre_n, bim_n)]
        bb_re, bb_im = _cmul(cr[:, st], ci[:, st], b_re, b_im)
        bd_ref[h, :, 0:half_states] = bb_re.astype(BF16)
        bd_ref[h, :, half_states:2 * half_states] = bb_im.astype(BF16)
        c_re, c_im = [jnp.where(same_group_cs, exact_dot(c[ch, :], spread_st), 0.0).T for c in (cre_n, cim_n)]
        cd_ref[h, 0:half_states, :] = c_re.astype(BF16)
        cd_ref[h, half_states:2 * half_states, :] = (-c_im).astype(BF16)


def _s5_prep(a_re, a_im, log_dt, b_re, b_im, c_re, c_im):
    g, p = a_re.shape
    hch = b_re.shape[2]
    assert hch == S5_GROUP and p == S5_STATE
    width = g * hch
    halves = width // S5_HALF_CH
    half_states = (S5_HALF_CH // hch) * p
    flat = lambda a: a.reshape(1, g * p)
    return pl.pallas_call(
        _s5_prep_kernel,
        out_shape=(jax.ShapeDtypeStruct((2, g * p), F32),
                   jax.ShapeDtypeStruct((halves, S5_HALF_CH, 2 * half_states), BF16),
                   jax.ShapeDtypeStruct((halves, 2 * half_states, S5_HALF_CH), BF16)),
        name="s5_prep",
    )(flat(a_re), flat(a_im), flat(jnp.broadcast_to(log_dt[:, None], (g, p))),
      b_re.reshape(g * p, hch), b_im.reshape(g * p, hch), c_re.reshape(width, p), c_im.reshape(width, p))


def _gate_prep_kernel(wr_ref, wi_ref, or_ref, oi_ref):
    hd = wr_ref.shape[1]
    per_tile = GATE_TILE // hd
    erow = lax.broadcasted_iota(jnp.int32, (hd, GATE_TILE), 0)
    ecol = lax.broadcasted_iota(jnp.int32, (hd, GATE_TILE), 1)
    for w_ref, o_ref in ((wr_ref, or_ref), (wi_ref, oi_ref)):
        for a in range(o_ref.shape[0]):
            for b in range(per_tile):
                place = jnp.where(ecol == erow + b * hd, 1.0, 0.0).astype(BF16)
                o_ref[a, b * hd:(b + 1) * hd, :] = jnp.dot(
                    w_ref[a * per_tile + b].astype(BF16), place, preferred_element_type=F32).astype(BF16)


def _gate_prep(w_r, w_i):
    heads, hd, _ = w_r.shape
    tiles = heads * hd // GATE_TILE
    shp = jax.ShapeDtypeStruct((tiles, GATE_TILE, GATE_TILE), BF16)
    return pl.pallas_call(_gate_prep_kernel, out_shape=(shp, shp), name="gate_prep")(w_r, w_i)


def _stream_cast(mats, stages, sem_ref):
    chunks = []
    used = [0] * len(stages)
    for src, dst, si in mats:
        rows = stages[si].shape[1]
        assert src.shape[0] % rows == 0 and src.shape[1] <= stages[si].shape[2]
        for r0 in range(0, src.shape[0], rows):
            slot = used[si] % WEIGHT_STAGE_SLOTS
            used[si] += 1
            chunks.append((src, dst, si, r0, rows, slot))

    def copy(k):
        src, _, si, r0, rows, slot = chunks[k]
        return pltpu.make_async_copy(src.at[pl.ds(r0, rows), :],
                                     stages[si].at[slot, :, pl.ds(0, src.shape[1])],
                                     sem_ref.at[si * WEIGHT_STAGE_SLOTS + slot])

    ahead = WEIGHT_STAGE_SLOTS - 1
    for k in range(min(ahead, len(chunks))):
        copy(k).start()
    for k, (src, dst, si, r0, rows, slot) in enumerate(chunks):
        if k + ahead < len(chunks):
            copy(k + ahead).start()
        copy(k).wait()
        dst[r0:r0 + rows, :] = stages[si][slot, :, 0:src.shape[1]].astype(BF16)


def _weight_scratch(weights, stage_shapes):
    return ([pltpu.VMEM(w.shape, BF16) for w in weights]
            + [pltpu.VMEM((WEIGHT_STAGE_SLOTS,) + tuple(s), F32) for s in stage_shapes]
            + [pltpu.SemaphoreType.DMA((WEIGHT_STAGE_SLOTS * len(stage_shapes),))])


def _ffn_kernel(x_ref, xn_ref, mod_ref, g_ref, fg_ref, wup_hbm, wdn_hbm, o_ref,
                wup_ref, wdn_ref, stage_up_ref, stage_dn_ref, sem_ref, u_ref, h0_ref, *,
                tiles_per_seq, final_norm):
    d = x_ref.shape[1]
    f = wdn_ref.shape[0]
    c0 = h0_ref.shape[1]
    step = pl.program_id(0)
    batch = step // tiles_per_seq
    batch_next = jnp.minimum(step + 1, pl.num_programs(0) - 1) // tiles_per_seq

    def mod_row(b, k):
        return mod_ref[pl.ds(b, 1), k * d:(k + 1) * d]

    def modulated(xv, b):
        return (_rms_norm(xv, g_ref[...]) * (1.0 + mod_row(b, 1)) + mod_row(b, 0)).astype(BF16)

    @pl.when(step == 0)
    def _():
        _stream_cast([(wup_hbm, wup_ref, 0), (wdn_hbm, wdn_ref, 1)], [stage_up_ref, stage_dn_ref], sem_ref)
        u0 = modulated(x_ref[...], batch)
        u_ref[...] = u0
        h0_ref[...] = jnp.dot(u0, wup_ref[:, :c0], preferred_element_type=F32)

    x = x_ref[...]
    gt = mod_row(batch, 2)
    h0 = h0_ref[...]
    h_rest = jnp.dot(u_ref[...], wup_ref[:, c0:], preferred_element_type=F32)
    un = modulated(xn_ref[...], batch_next)
    u_ref[...] = un
    h0_ref[...] = jnp.dot(un, wup_ref[:, :c0], preferred_element_type=F32)
    a = jnp.concatenate([h0, h_rest[:, :f - c0]], axis=1)
    b = h_rest[:, f - c0:]
    act = (a * _sigmoid(a)) * b
    y = jnp.dot(act.astype(BF16), wdn_ref[...], preferred_element_type=F32)
    out = x + (0.5 * gt) * y
    if final_norm:
        out = _rms_norm(out, fg_ref[...])
    o_ref[...] = out


def _ffn(x2d, mod, sub, norm_g, w_up, w_down, final_g, *, seq, final_norm):
    t, d = x2d.shape
    f = w_down.shape[0]
    tm = FFN_TM
    tiles_per_seq = seq // tm
    n_tiles = t // tm
    nxt = lambda i: jnp.minimum(i + 1, n_tiles - 1)
    kern = functools.partial(_ffn_kernel, tiles_per_seq=tiles_per_seq, final_norm=final_norm)
    return pl.pallas_call(
        kern,
        grid=(n_tiles,),
        in_specs=[pl.BlockSpec((tm, d), lambda i: (i, 0)),
                  pl.BlockSpec((tm, d), lambda i: (nxt(i), 0)),
                  pl.BlockSpec((None,) + mod.shape[1:], lambda i: (sub, 0, 0), pipeline_mode=pl.Buffered(1)),
                  _const_spec((1, d)),
                  _const_spec((1, d)),
                  pl.BlockSpec(memory_space=pl.ANY),
                  pl.BlockSpec(memory_space=pl.ANY)],
        out_specs=pl.BlockSpec((tm, d), lambda i: (i, 0)),
        out_shape=jax.ShapeDtypeStruct((t, d), F32),
        scratch_shapes=_weight_scratch([w_up, w_down], [(d // 16, 2 * f), (f // 8, d)])
                       + [pltpu.VMEM((tm, d), BF16),
                          pltpu.VMEM((tm, FFN_HEAD_COLS), F32)],
        compiler_params=pltpu.CompilerParams(dimension_semantics=("arbitrary",),
                                             vmem_limit_bytes=VMEM_LIMIT),
        name="ffn_final" if final_norm else "ffn",
    )(x2d, x2d, mod, norm_g.reshape(1, d), final_g.reshape(1, d), w_up, w_down)


def _lru_scan(a_ref, u_ref, carry_ref):
    ts = u_ref.shape[0]
    h = carry_ref[...]
    for t in range(ts // SUBLANES):
        sl = slice(t * SUBLANES, (t + 1) * SUBLANES)
        h = a_ref[sl, :] * h + u_ref[sl, :]
        u_ref[sl, :] = h
    carry_ref[...] = h


def _s5_scan(u_ref, lam_ref, carry_ref, halves):
    ts = u_ref.shape[0]
    half_states = lam_ref.shape[1] // halves
    w = S5_LANE_CHUNK
    for h in range(halves):
        for j in range(half_states // w):
            cre = slice(2 * half_states * h + j * w, 2 * half_states * h + (j + 1) * w)
            cim = slice(2 * half_states * h + half_states + j * w, 2 * half_states * h + half_states + (j + 1) * w)
            lcs = slice(half_states * h + j * w, half_states * h + (j + 1) * w)
            lr = jnp.broadcast_to(lam_ref[0:1, lcs], (SUBLANES, w))
            li = jnp.broadcast_to(lam_ref[1:2, lcs], (SUBLANES, w))
            hr = carry_ref[:, cre]
            hi = carry_ref[:, cim]
            for t in range(ts // SUBLANES):
                sl = slice(t * SUBLANES, (t + 1) * SUBLANES)
                hr, hi = (u_ref[sl, cre] + (lr * hr - li * hi),
                          u_ref[sl, cim] + (lr * hi + li * hr))
                u_ref[sl, cre] = hr
                u_ref[sl, cim] = hi
            carry_ref[:, cre] = hr
            carry_ref[:, cim] = hi


def _mixer_kernel(mod_ref, g_ref, bin_ref, cw_ref, cb_ref,
                  wr_ref, br_ref, wi_ref, bi_ref, lamp_ref,
                  bd_ref, cd_ref, lam_ref, sd_ref, gb_ref,
                  win_hbm, pa_hbm, gw_hbm, pb_hbm, wo_hbm, x_hbm,
                  o_hbm,
                  win_ref, pa_ref, gw_ref, pb_ref, wo_ref, stage_in_ref, stage_sq_ref, sem_ref,
                  xt_ref, ot_ref, sem_x_ref, sem_o_ref,
                  xpad_ref, a_ref, u_ref, s5_ref, hl_ref, hs_ref):
    _, tt, nb, d = xt_ref.shape
    ts = nb * tt
    w5 = sd_ref.shape[1]
    halves = bd_ref.shape[0]
    hs2 = bd_ref.shape[2]
    pad = (CONV_WIDTH - 1) * nb
    step = pl.program_id(0)
    n_steps = pl.num_programs(0)
    slot = step % 2

    def x_copies(tile, sl):
        return [pltpu.make_async_copy(x_hbm.at[b, pl.ds(tile * tt, tt), :], xt_ref.at[sl, :, b, :],
                                      sem_x_ref.at[sl]) for b in range(nb)]

    def o_copies(tile, sl):
        return [pltpu.make_async_copy(ot_ref.at[sl, :, b, :], o_hbm.at[b, pl.ds(tile * tt, tt), :],
                                      sem_o_ref.at[sl]) for b in range(nb)]

    @pl.when(step == 0)
    def _():
        for cp in x_copies(0, 0):
            cp.start()
        xpad_ref[0:pad, :] = jnp.zeros((pad, d), F32)
        hl_ref[...] = jnp.zeros_like(hl_ref)
        hs_ref[...] = jnp.zeros_like(hs_ref)
        _stream_cast([(win_hbm, win_ref, 0), (pa_hbm, pa_ref, 1), (gw_hbm, gw_ref, 1), (pb_hbm, pb_ref, 1),
                      (wo_hbm, wo_ref, 1)], [stage_in_ref, stage_sq_ref], sem_ref)

    @pl.when(step + 1 < n_steps)
    def _():
        for cp in x_copies(step + 1, 1 - slot):
            cp.start()

    for cp in x_copies(step, slot):
        cp.wait()

    def per_batch(v):
        return v[None, :, :]

    x3 = xt_ref[slot]
    sh = per_batch(mod_ref[:, 0:d])
    sc = per_batch(mod_ref[:, d:2 * d])
    gt = per_batch(mod_ref[:, 2 * d:3 * d])
    u_bf = (_rms_norm(x3, g_ref[...]) * (1.0 + sc) + sh).reshape(ts, d).astype(BF16)

    def in_proj(lo, hi):
        return jnp.dot(u_bf, win_ref[:, lo:hi], preferred_element_type=F32) + bin_ref[:, lo:hi]

    xpad_ref[pad:pad + ts, :] = in_proj(0, d)
    xb = in_proj(2 * d, 2 * d + w5)
    xb_bf = xb.astype(BF16)
    for h in range(halves):
        s5_ref[:, h * hs2:(h + 1) * hs2] = jnp.dot(
            xb_bf[:, h * S5_HALF_CH:(h + 1) * S5_HALF_CH], bd_ref[h], preferred_element_type=F32)
    ga_pre = in_proj(d, 2 * d)

    xc = cb_ref[...]
    for k in range(CONV_WIDTH):
        xc = xc + xpad_ref[k * nb:k * nb + ts, :] * cw_ref[k:k + 1, :]
    xpad_ref[0:pad, :] = xpad_ref[ts:ts + pad, :]
    xc_bf = xc.astype(BF16)
    gates = []
    for w_ref, b_ref in ((wr_ref, br_ref), (wi_ref, bi_ref)):
        cols = [jnp.dot(xc_bf[:, j * GATE_TILE:(j + 1) * GATE_TILE], w_ref[j], preferred_element_type=F32)
                for j in range(w_ref.shape[0])]
        gates.append(_sigmoid(jnp.concatenate(cols, axis=1) + b_ref[...]))
    r_gate, i_gate = gates
    mga_pre = in_proj(2 * d + w5, 3 * d + w5)
    nl = -lamp_ref[...]
    softplus = jnp.maximum(nl, 0.0) + jnp.log1p(jnp.exp(-jnp.abs(nl)))
    neg_log_a = (LRU_C * softplus) * r_gate
    a_gate = jnp.exp(-neg_log_a)
    a_ref[...] = a_gate
    one_m_a2 = jnp.tanh(neg_log_a) * (1.0 + a_gate * a_gate)
    mult = jnp.where(one_m_a2 > 0.0, one_m_a2 * lax.rsqrt(one_m_a2), 0.0)
    u_ref[...] = mult * (i_gate * xc)
    _lru_scan(a_ref, u_ref, hl_ref)
    ya = u_ref[...] * _gelu_tanh(ga_pre)
    proj_a = jnp.dot(ya.astype(BF16), pa_ref[...], preferred_element_type=F32)
    mgb_pre = in_proj(3 * d + w5, 4 * d + w5)

    _s5_scan(s5_ref, lam_ref, hs_ref, halves)
    ys = [jnp.dot(s5_ref[:, h * hs2:(h + 1) * hs2].astype(BF16), cd_ref[h], preferred_element_type=F32)
          for h in range(halves)]
    yb = _gelu_tanh(jnp.concatenate(ys, axis=1) + sd_ref[...] * xb)
    yb = yb * _sigmoid(jnp.dot(yb.astype(BF16), gw_ref[...], preferred_element_type=F32) + gb_ref[...])
    proj_b = jnp.dot(yb.astype(BF16), pb_ref[...], preferred_element_type=F32)

    m = _sigmoid(mga_pre) * proj_a + _sigmoid(mgb_pre) * proj_b
    y = jnp.dot(m.astype(BF16), wo_ref[...], preferred_element_type=F32)
    out = x3 + gt * y.reshape(tt, nb, d)

    @pl.when(step >= 2)
    def _():
        for cp in o_copies(step - 2, slot):
            cp.wait()

    ot_ref[slot] = out
    for cp in o_copies(step, slot):
        cp.start()

    @pl.when(step == n_steps - 1)
    def _():
        for cp in o_copies(step - 1, 1 - slot):
            cp.wait()
        for cp in o_copies(step, slot):
            cp.wait()


def _mixer(x3d, mod, sub, norm_g, w_in, b_in, conv_w, conv_b, wr_bd, b_r, wi_bd, b_i, lam_p, proj_a,
           bd, cd, lam, s5_d, glu_w, glu_b, proj_b, w_out):
    bsz, seq, d = x3d.shape
    tt = MIX_TT
    ts = tt * bsz
    n_states2 = bd.shape[0] * bd.shape[2]
    row = lambda v: v.reshape(1, -1)
    consts = [row(norm_g), row(b_in), conv_w, row(conv_b), wr_bd, row(b_r), wi_bd, row(b_i),
              row(lam_p), bd, cd, lam, row(s5_d), row(glu_b)]
    streamed = [w_in, proj_a, glu_w, proj_b, w_out]
    assert seq // tt >= 2, "the write-back drain expects at least two sequence tiles"
    return pl.pallas_call(
        _mixer_kernel,
        grid=(seq // tt,),
        in_specs=[pl.BlockSpec((None,) + mod.shape[1:], lambda i: (sub, 0, 0), pipeline_mode=pl.Buffered(1))]
                 + [_const_spec(c.shape) for c in consts]
                 + [pl.BlockSpec(memory_space=pl.ANY)] * (len(streamed) + 1),
        out_specs=pl.BlockSpec(memory_space=pl.ANY),
        out_shape=jax.ShapeDtypeStruct((bsz, seq, d), F32),
        scratch_shapes=_weight_scratch(streamed, [(d // 32, w_in.shape[1]), (d // 8, d)])
                       + [pltpu.VMEM((2, tt, bsz, d), F32),
                          pltpu.VMEM((2, tt, bsz, d), F32),
                          pltpu.SemaphoreType.DMA((2,)),
                          pltpu.SemaphoreType.DMA((2,)),
                        pltpu.VMEM((ts + (CONV_WIDTH - 1) * bsz, d), F32),
                        pltpu.VMEM((ts, d), F32),
                        pltpu.VMEM((ts, d), F32),
                        pltpu.VMEM((ts, n_states2), F32),
                        pltpu.VMEM((bsz, d), F32),
                        pltpu.VMEM((bsz, n_states2), F32)],
        compiler_params=pltpu.CompilerParams(dimension_semantics=("arbitrary",),
                                             vmem_limit_bytes=VMEM_LIMIT),
        name="mixer",
    )(mod, *consts, *streamed, x3d)


def kernel(x, c, mod_w, mod_b, norm1_g, ffn1_w_up, ffn1_w_down, norm2_g, w_in, b_in, conv_w, conv_b, lru_w_r, lru_b_r, lru_w_i, lru_b_i, lru_lambda, proj_a, s5_a_re, s5_a_im, s5_log_dt, s5_b_re, s5_b_im, s5_c_re, s5_c_im, s5_d, glu_w, glu_b, proj_b, w_out, norm3_g, ffn2_w_up, ffn2_w_down, final_g):
    bsz, seq, d = x.shape
    assert bsz == SUBLANES, "time-major layout puts the batch on the 8 sublanes"
    depth = mod_w.shape[0]
    for l in range(depth):
        mod = _mod(c, mod_w[l], mod_b[l])
        lam, bd, cd = _s5_prep(s5_a_re[l], s5_a_im[l], s5_log_dt[l], s5_b_re[l], s5_b_im[l],
                               s5_c_re[l], s5_c_im[l])
        wr_bd, wi_bd = _gate_prep(lru_w_r[l], lru_w_i[l])
        last = l == depth - 1
        x2d = _ffn(x.reshape(bsz * seq, d), mod, 0, norm1_g[l], ffn1_w_up[l], ffn1_w_down[l], final_g,
                   seq=seq, final_norm=False)
        x3d = _mixer(x2d.reshape(bsz, seq, d), mod, 1, norm2_g[l], w_in[l], b_in[l], conv_w[l], conv_b[l],
                     wr_bd, lru_b_r[l], wi_bd, lru_b_i[l], lru_lambda[l], proj_a[l],
                     bd, cd, lam, s5_d[l], glu_w[l], glu_b[l], proj_b[l], w_out[l])
        x = _ffn(x3d.reshape(bsz * seq, d), mod, 2, norm3_g[l], ffn2_w_up[l], ffn2_w_down[l], final_g,
                 seq=seq, final_norm=last).reshape(bsz, seq, d)
    return x
```

```python
import functools
import math

import jax
import jax.numpy as jnp
from jax import lax
from jax.experimental import pallas as pl
from jax.experimental.pallas import tpu as pltpu

F32 = jnp.float32
BF16 = jnp.bfloat16

EPS = 1e-6
LRU_C = 8.0
CONV_WIDTH = 4
S5_GROUP = 16
S5_STATE = 64
MOD_PER_SUB = 3

SUBLANES = 8
LANES = 128
GATE_TILE = 256
S5_HALF_CH = 256
VMEM_LIMIT = 60 * 1024 * 1024

WEIGHT_STAGE_SLOTS = 4
FFN_TM = 512
FFN_TILES_PER_STEP = 2
FFN_HEAD_COLS = 256
MIX_TT = 64
S5_LANE_CHUNK = 512


def _sigmoid(x):
    return 0.5 * jnp.tanh(0.5 * x) + 0.5


def _gelu_tanh(x):
    c = math.sqrt(2.0 / math.pi)
    return 0.5 * x * (1.0 + jnp.tanh(c * (x + 0.044715 * (x * x * x))))


def _rms_norm(x, g):
    ms = jnp.mean(x * x, axis=-1, keepdims=True)
    return x * lax.rsqrt(ms + EPS) * g


def _const_spec(shape):
    nd = len(shape)
    return pl.BlockSpec(shape, lambda *_: (0,) * nd, pipeline_mode=pl.Buffered(1))


def _mod_kernel(c_ref, *refs):
    n_sub = (len(refs) - 1) // 2
    o_ref = refs[-1]
    c = c_ref[...]
    ca = (c * _sigmoid(c)).astype(BF16)
    for s in range(n_sub):
        o_ref[s] = jnp.dot(ca, refs[s][...].astype(BF16), preferred_element_type=F32) + refs[n_sub + s][...]


def _mod(c, mod_w, mod_b):
    bsz, d = c.shape
    n = mod_w.shape[1]
    n_sub = n // (MOD_PER_SUB * d)
    w_spec = lambda s: pl.BlockSpec((d, d), lambda j: (0, s * MOD_PER_SUB + j))
    b_spec = lambda s: pl.BlockSpec((1, d), lambda j: (0, s * MOD_PER_SUB + j))
    return pl.pallas_call(
        _mod_kernel,
        grid=(MOD_PER_SUB,),
        in_specs=[pl.BlockSpec((bsz, d), lambda j: (0, 0))]
                 + [w_spec(s) for s in range(n_sub)] + [b_spec(s) for s in range(n_sub)],
        out_specs=pl.BlockSpec((n_sub, bsz, d), lambda j: (0, 0, j)),
        out_shape=jax.ShapeDtypeStruct((n_sub, bsz, MOD_PER_SUB * d), F32),
        compiler_params=pltpu.CompilerParams(dimension_semantics=("arbitrary",),
                                             vmem_limit_bytes=VMEM_LIMIT),
        name="mod",
    )(c, *([mod_w] * n_sub), *([mod_b.reshape(1, n)] * n_sub))


def _cmul(ar, ai, br, bi):
    return ar * br - ai * bi, ar * bi + ai * br


def _s5_discretise(a_re, a_im, log_dt):
    dt = jnp.exp(log_dt)
    mag = jnp.exp(a_re * dt)
    lr = mag * jnp.cos(a_im * dt)
    li = mag * jnp.sin(a_im * dt)
    den = a_re * a_re + a_im * a_im
    nr = lr - 1.0
    cr = (nr * a_re + li * a_im) / den
    ci = (li * a_re - nr * a_im) / den
    return lr, li, cr, ci


def _s5_prep_kernel(are_g, aim_g, ldt_g, are_r, aim_r, ldt_r, bre_t, bim_t, cre_t, cim_t,
                    lam_ref, bd_ref, cd_ref):
    lr, li, _, _ = _s5_discretise(are_g[...], aim_g[...], ldt_g[...])
    lam_ref[0] = lr
    lam_ref[1] = li

    _, _, cr, ci = _s5_discretise(are_r[...], aim_r[...], ldt_r[...])
    bb_re, bb_im = _cmul(cr, ci, bre_t[...], bim_t[...])
    half_states = (S5_HALF_CH // S5_GROUP) * S5_STATE
    erow = lax.broadcasted_iota(jnp.int32, (S5_STATE, half_states), 0)
    ecol = lax.broadcasted_iota(jnp.int32, (S5_STATE, half_states), 1)
    spread = jnp.where((ecol % S5_STATE) == erow, 1.0, 0.0).astype(BF16)
    brow_g = lax.broadcasted_iota(jnp.int32, (S5_HALF_CH, half_states), 0) // S5_GROUP
    bcol_g = lax.broadcasted_iota(jnp.int32, (S5_HALF_CH, half_states), 1) // S5_STATE
    bmask = brow_g == bcol_g
    crow_g = lax.broadcasted_iota(jnp.int32, (half_states, S5_HALF_CH), 0) // S5_STATE
    ccol_g = lax.broadcasted_iota(jnp.int32, (half_states, S5_HALF_CH), 1) // S5_GROUP
    cmask = crow_g == ccol_g
    reps = half_states // S5_STATE
    for h in range(bd_ref.shape[0]):
        rows = slice(h * S5_HALF_CH, (h + 1) * S5_HALF_CH)
        for k, bb in enumerate((bb_re, bb_im)):
            tiled = jnp.dot(bb[rows, :].astype(BF16), spread, preferred_element_type=F32)
            bd_ref[h, :, k * half_states:(k + 1) * half_states] = jnp.where(bmask, tiled, 0.0).astype(BF16)
        for k, (ct, sign) in enumerate(((cre_t, 1.0), (cim_t, -1.0))):
            tiled = jnp.concatenate([ct[:, rows]] * reps, axis=0)
            cd_ref[h, k * half_states:(k + 1) * half_states, :] = jnp.where(cmask, sign * tiled, 0.0).astype(BF16)


def _s5_prep(a_re, a_im, log_dt, b_re, b_im, c_re, c_im):
    g, p = a_re.shape
    hch = b_re.shape[2]
    assert hch == S5_GROUP and p == S5_STATE
    width = g * hch
    halves = width // S5_HALF_CH
    half_states = (S5_HALF_CH // hch) * p
    rep = lambda a: jnp.repeat(a, hch, axis=0)
    ldt = log_dt.reshape(g, 1)
    bt = lambda b: jnp.transpose(b, (0, 2, 1)).reshape(width, p)
    ct = lambda c: jnp.transpose(c, (2, 0, 1)).reshape(p, width)
    lam, bd, cd = pl.pallas_call(
        _s5_prep_kernel,
        out_shape=(jax.ShapeDtypeStruct((2, g, p), F32),
                   jax.ShapeDtypeStruct((halves, S5_HALF_CH, 2 * half_states), BF16),
                   jax.ShapeDtypeStruct((halves, 2 * half_states, S5_HALF_CH), BF16)),
        name="s5_prep",
    )(a_re, a_im, ldt, rep(a_re), rep(a_im), rep(ldt), bt(b_re), bt(b_im), ct(c_re), ct(c_im))
    return lam.reshape(2, g * p), bd, cd


def _gate_prep_kernel(wr_ref, wi_ref, or_ref, oi_ref):
    hd = wr_ref.shape[1]
    per_tile = GATE_TILE // hd
    erow = lax.broadcasted_iota(jnp.int32, (hd, GATE_TILE), 0)
    ecol = lax.broadcasted_iota(jnp.int32, (hd, GATE_TILE), 1)
    for w_ref, o_ref in ((wr_ref, or_ref), (wi_ref, oi_ref)):
        for a in range(o_ref.shape[0]):
            for b in range(per_tile):
                place = jnp.where(ecol == erow + b * hd, 1.0, 0.0).astype(BF16)
                o_ref[a, b * hd:(b + 1) * hd, :] = jnp.dot(
                    w_ref[a * per_tile + b].astype(BF16), place, preferred_element_type=F32).astype(BF16)


def _gate_prep(w_r, w_i):
    heads, hd, _ = w_r.shape
    tiles = heads * hd // GATE_TILE
    shp = jax.ShapeDtypeStruct((tiles, GATE_TILE, GATE_TILE), BF16)
    return pl.pallas_call(_gate_prep_kernel, out_shape=(shp, shp), name="gate_prep")(w_r, w_i)


def _stream_cast(mats, stages, sem_ref):
    chunks = []
    used = [0] * len(stages)
    for src, dst, si in mats:
        rows = stages[si].shape[1]
        assert src.shape[0] % rows == 0 and src.shape[1] <= stages[si].shape[2]
        for r0 in range(0, src.shape[0], rows):
            slot = used[si] % WEIGHT_STAGE_SLOTS
            used[si] += 1
            chunks.append((src, dst, si, r0, rows, slot))

    def copy(k):
        src, _, si, r0, rows, slot = chunks[k]
        return pltpu.make_async_copy(src.at[pl.ds(r0, rows), :],
                                     stages[si].at[slot, :, pl.ds(0, src.shape[1])],
                                     sem_ref.at[si * WEIGHT_STAGE_SLOTS + slot])

    ahead = WEIGHT_STAGE_SLOTS - 1
    for k in range(min(ahead, len(chunks))):
        copy(k).start()
    for k, (src, dst, si, r0, rows, slot) in enumerate(chunks):
        if k + ahead < len(chunks):
            copy(k + ahead).start()
        copy(k).wait()
        dst[r0:r0 + rows, :] = stages[si][slot, :, 0:src.shape[1]].astype(BF16)


def _weight_scratch(weights, stage_shapes):
    return ([pltpu.VMEM(w.shape, BF16) for w in weights]
            + [pltpu.VMEM((WEIGHT_STAGE_SLOTS,) + tuple(s), F32) for s in stage_shapes]
            + [pltpu.SemaphoreType.DMA((WEIGHT_STAGE_SLOTS * len(stage_shapes),))])


def _ffn_kernel(x_ref, xn_ref, mod_ref, g_ref, fg_ref, wup_hbm, wdn_hbm, o_ref,
                wup_ref, wdn_ref, stage_up_ref, stage_dn_ref, sem_ref, u_ref, h0_ref, *,
                steps_per_seq, final_norm):
    d = x_ref.shape[1]
    tm = xn_ref.shape[0]
    n_sub = x_ref.shape[0] // tm
    f = wdn_ref.shape[0]
    c0 = h0_ref.shape[1]
    step = pl.program_id(0)
    batch = step // steps_per_seq
    batch_next = jnp.minimum(step + 1, pl.num_programs(0) - 1) // steps_per_seq

    def mod_row(b, k):
        return mod_ref[pl.ds(b, 1), k * d:(k + 1) * d]

    def modulated(xv, b):
        return (_rms_norm(xv, g_ref[...]) * (1.0 + mod_row(b, 1)) + mod_row(b, 0)).astype(BF16)

    @pl.when(step == 0)
    def _():
        _stream_cast([(wup_hbm, wup_ref, 0), (wdn_hbm, wdn_ref, 1)], [stage_up_ref, stage_dn_ref], sem_ref)
        u0 = modulated(x_ref[0:tm, :], batch)
        u_ref[...] = u0
        h0_ref[...] = jnp.dot(u0, wup_ref[:, :c0], preferred_element_type=F32)

    gt = mod_row(batch, 2)
    for k in range(n_sub):
        rows = slice(k * tm, (k + 1) * tm)
        x = x_ref[rows, :]
        h0 = h0_ref[...]
        h_rest = jnp.dot(u_ref[...], wup_ref[:, c0:], preferred_element_type=F32)
        if k + 1 < n_sub:
            un = modulated(x_ref[(k + 1) * tm:(k + 2) * tm, :], batch)
        else:
            un = modulated(xn_ref[...], batch_next)
        u_ref[...] = un
        h0_ref[...] = jnp.dot(un, wup_ref[:, :c0], preferred_element_type=F32)
        a = jnp.concatenate([h0, h_rest[:, :f - c0]], axis=1)
        b = h_rest[:, f - c0:]
        act = (a * _sigmoid(a)) * b
        y = jnp.dot(act.astype(BF16), wdn_ref[...], preferred_element_type=F32)
        out = x + (0.5 * gt) * y
        if final_norm:
            out = _rms_norm(out, fg_ref[...])
        o_ref[rows, :] = out


def _ffn(x2d, mod, sub, norm_g, w_up, w_down, final_g, *, seq, final_norm):
    t, d = x2d.shape
    f = w_down.shape[0]
    tm = FFN_TM
    rows_per_step = tm * FFN_TILES_PER_STEP
    steps_per_seq = seq // rows_per_step
    n_steps = t // rows_per_step
    n_tiles = t // tm
    nxt = lambda i: jnp.minimum(FFN_TILES_PER_STEP * (i + 1), n_tiles - 1)
    kern = functools.partial(_ffn_kernel, steps_per_seq=steps_per_seq, final_norm=final_norm)
    return pl.pallas_call(
        kern,
        grid=(n_steps,),
        in_specs=[pl.BlockSpec((rows_per_step, d), lambda i: (i, 0)),
                  pl.BlockSpec((tm, d), lambda i: (nxt(i), 0)),
                  pl.BlockSpec((None,) + mod.shape[1:], lambda i: (sub, 0, 0), pipeline_mode=pl.Buffered(1)),
                  _const_spec((1, d)),
                  _const_spec((1, d)),
                  pl.BlockSpec(memory_space=pl.ANY),
                  pl.BlockSpec(memory_space=pl.ANY)],
        out_specs=pl.BlockSpec((rows_per_step, d), lambda i: (i, 0)),
        out_shape=jax.ShapeDtypeStruct((t, d), F32),
        scratch_shapes=_weight_scratch([w_up, w_down], [(d // 16, 2 * f), (f // 8, d)])
                       + [pltpu.VMEM((tm, d), BF16),
                          pltpu.VMEM((tm, FFN_HEAD_COLS), F32)],
        compiler_params=pltpu.CompilerParams(dimension_semantics=("arbitrary",),
                                             vmem_limit_bytes=VMEM_LIMIT),
        name="ffn_final" if final_norm else "ffn",
    )(x2d, x2d, mod, norm_g.reshape(1, d), final_g.reshape(1, d), w_up, w_down)


def _lru_scan(a_ref, u_ref, carry_ref):
    ts = u_ref.shape[0]
    h = carry_ref[...]
    for t in range(ts // SUBLANES):
        sl = slice(t * SUBLANES, (t + 1) * SUBLANES)
        h = a_ref[sl, :] * h + u_ref[sl, :]
        u_ref[sl, :] = h
    carry_ref[...] = h


def _s5_scan(u_ref, lam_ref, carry_ref, halves):
    ts = u_ref.shape[0]
    half_states = lam_ref.shape[1] // halves
    w = S5_LANE_CHUNK
    for h in range(halves):
        for j in range(half_states // w):
            cre = slice(2 * half_states * h + j * w, 2 * half_states * h + (j + 1) * w)
            cim = slice(2 * half_states * h + half_states + j * w, 2 * half_states * h + half_states + (j + 1) * w)
            lcs = slice(half_states * h + j * w, half_states * h + (j + 1) * w)
            lr = jnp.broadcast_to(lam_ref[0:1, lcs], (SUBLANES, w))
            li = jnp.broadcast_to(lam_ref[1:2, lcs], (SUBLANES, w))
            hr = carry_ref[:, cre]
            hi = carry_ref[:, cim]
            for t in range(ts // SUBLANES):
                sl = slice(t * SUBLANES, (t + 1) * SUBLANES)
                hr, hi = (u_ref[sl, cre] + (lr * hr - li * hi),
                          u_ref[sl, cim] + (lr * hi + li * hr))
                u_ref[sl, cre] = hr
                u_ref[sl, cim] = hi
            carry_ref[:, cre] = hr
            carry_ref[:, cim] = hi


def _mixer_kernel(mod_ref, g_ref, bin_ref, cw_ref, cb_ref,
                  wr_ref, br_ref, wi_ref, bi_ref, lamp_ref,
                  bd_ref, cd_ref, lam_ref, sd_ref, gb_ref,
                  win_hbm, pa_hbm, gw_hbm, pb_hbm, wo_hbm, x_hbm,
                  o_hbm,
                  win_ref, pa_ref, gw_ref, pb_ref, wo_ref, stage_in_ref, stage_sq_ref, sem_ref,
                  xt_ref, ot_ref, sem_x_ref, sem_o_ref,
                  xpad_ref, a_ref, u_ref, s5_ref, hl_ref, hs_ref):
    _, tt, nb, d = xt_ref.shape
    ts = nb * tt
    w5 = sd_ref.shape[1]
    halves = bd_ref.shape[0]
    hs2 = bd_ref.shape[2]
    pad = (CONV_WIDTH - 1) * nb
    step = pl.program_id(0)
    n_steps = pl.num_programs(0)
    slot = step % 2

    def x_copies(tile, sl):
        return [pltpu.make_async_copy(x_hbm.at[b, pl.ds(tile * tt, tt), :], xt_ref.at[sl, :, b, :],
                                      sem_x_ref.at[sl]) for b in range(nb)]

    def o_copies(tile, sl):
        return [pltpu.make_async_copy(ot_ref.at[sl, :, b, :], o_hbm.at[b, pl.ds(tile * tt, tt), :],
                                      sem_o_ref.at[sl]) for b in range(nb)]

    @pl.when(step == 0)
    def _():
        for cp in x_copies(0, 0):
            cp.start()
        xpad_ref[0:pad, :] = jnp.zeros((pad, d), F32)
        hl_ref[...] = jnp.zeros_like(hl_ref)
        hs_ref[...] = jnp.zeros_like(hs_ref)
        _stream_cast([(win_hbm, win_ref, 0), (pa_hbm, pa_ref, 1), (gw_hbm, gw_ref, 1), (pb_hbm, pb_ref, 1),
                      (wo_hbm, wo_ref, 1)], [stage_in_ref, stage_sq_ref], sem_ref)

    @pl.when(step + 1 < n_steps)
    def _():
        for cp in x_copies(step + 1, 1 - slot):
            cp.start()

    for cp in x_copies(step, slot):
        cp.wait()

    def per_batch(v):
        return v[None, :, :]

    x3 = xt_ref[slot]
    sh = per_batch(mod_ref[:, 0:d])
    sc = per_batch(mod_ref[:, d:2 * d])
    gt = per_batch(mod_ref[:, 2 * d:3 * d])
    u_bf = (_rms_norm(x3, g_ref[...]) * (1.0 + sc) + sh).reshape(ts, d).astype(BF16)

    def in_proj(lo, hi):
        return jnp.dot(u_bf, win_ref[:, lo:hi], preferred_element_type=F32) + bin_ref[:, lo:hi]

    xpad_ref[pad:pad + ts, :] = in_proj(0, d)
    xb = in_proj(2 * d, 2 * d + w5)
    xb_bf = xb.astype(BF16)
    for h in range(halves):
        s5_ref[:, h * hs2:(h + 1) * hs2] = jnp.dot(
            xb_bf[:, h * S5_HALF_CH:(h + 1) * S5_HALF_CH], bd_ref[h], preferred_element_type=F32)
    ga_pre = in_proj(d, 2 * d)

    xc = cb_ref[...]
    for k in range(CONV_WIDTH):
        xc = xc + xpad_ref[k * nb:k * nb + ts, :] * cw_ref[k:k + 1, :]
    xpad_ref[0:pad, :] = xpad_ref[ts:ts + pad, :]
    xc_bf = xc.astype(BF16)
    gates = []
    for w_ref, b_ref in ((wr_ref, br_ref), (wi_ref, bi_ref)):
        cols = [jnp.dot(xc_bf[:, j * GATE_TILE:(j + 1) * GATE_TILE], w_ref[j], preferred_element_type=F32)
                for j in range(w_ref.shape[0])]
        gates.append(_sigmoid(jnp.concatenate(cols, axis=1) + b_ref[...]))
    r_gate, i_gate = gates
    mga_pre = in_proj(2 * d + w5, 3 * d + w5)
    nl = -lamp_ref[...]
    softplus = jnp.maximum(nl, 0.0) + jnp.log1p(jnp.exp(-jnp.abs(nl)))
    neg_log_a = (LRU_C * softplus) * r_gate
    a_gate = jnp.exp(-neg_log_a)
    a_ref[...] = a_gate
    one_m_a2 = jnp.tanh(neg_log_a) * (1.0 + a_gate * a_gate)
    mult = jnp.where(one_m_a2 > 0.0, one_m_a2 * lax.rsqrt(one_m_a2), 0.0)
    u_ref[...] = mult * (i_gate * xc)
    _lru_scan(a_ref, u_ref, hl_ref)
    ya = u_ref[...] * _gelu_tanh(ga_pre)
    proj_a = jnp.dot(ya.astype(BF16), pa_ref[...], preferred_element_type=F32)
    mgb_pre = in_proj(3 * d + w5, 4 * d + w5)

    _s5_scan(s5_ref, lam_ref, hs_ref, halves)
    ys = [jnp.dot(s5_ref[:, h * hs2:(h + 1) * hs2].astype(BF16), cd_ref[h], preferred_element_type=F32)
          for h in range(halves)]
    yb = _gelu_tanh(jnp.concatenate(ys, axis=1) + sd_ref[...] * xb)
    yb = yb * _sigmoid(jnp.dot(yb.astype(BF16), gw_ref[...], preferred_element_type=F32) + gb_ref[...])
    proj_b = jnp.dot(yb.astype(BF16), pb_ref[...], preferred_element_type=F32)

    m = _sigmoid(mga_pre) * proj_a + _sigmoid(mgb_pre) * proj_b
    y = jnp.dot(m.astype(BF16), wo_ref[...], preferred_element_type=F32)
    out = x3 + gt * y.reshape(tt, nb, d)

    @pl.when(step >= 2)
    def _():
        for cp in o_copies(step - 2, slot):
            cp.wait()

    ot_ref[slot] = out
    for cp in o_copies(step, slot):
        cp.start()

    @pl.when(step == n_steps - 1)
    def _():
        for cp in o_copies(step - 1, 1 - slot):
            cp.wait()
        for cp in o_copies(step, slot):
            cp.wait()


def _mixer(x3d, mod, sub, norm_g, w_in, b_in, conv_w, conv_b, wr_bd, b_r, wi_bd, b_i, lam_p, proj_a,
           bd, cd, lam, s5_d, glu_w, glu_b, proj_b, w_out):
    bsz, seq, d = x3d.shape
    tt = MIX_TT
    ts = tt * bsz
    n_states2 = bd.shape[0] * bd.shape[2]
    row = lambda v: v.reshape(1, -1)
    consts = [row(norm_g), row(b_in), conv_w, row(conv_b), wr_bd, row(b_r), wi_bd, row(b_i),
              row(lam_p), bd, cd, lam, row(s5_d), row(glu_b)]
    streamed = [w_in, proj_a, glu_w, proj_b, w_out]
    assert seq // tt >= 2, "the write-back drain expects at least two sequence tiles"
    return pl.pallas_call(
        _mixer_kernel,
        grid=(seq // tt,),
        in_specs=[pl.BlockSpec((None,) + mod.shape[1:], lambda i: (sub, 0, 0), pipeline_mode=pl.Buffered(1))]
                 + [_const_spec(c.shape) for c in consts]
                 + [pl.BlockSpec(memory_space=pl.ANY)] * (len(streamed) + 1),
        out_specs=pl.BlockSpec(memory_space=pl.ANY),
        out_shape=jax.ShapeDtypeStruct((bsz, seq, d), F32),
        scratch_shapes=_weight_scratch(streamed, [(d // 32, w_in.shape[1]), (d // 8, d)])
                       + [pltpu.VMEM((2, tt, bsz, d), F32),
                          pltpu.VMEM((2, tt, bsz, d), F32),
                          pltpu.SemaphoreType.DMA((2,)),
                          pltpu.SemaphoreType.DMA((2,)),
                        pltpu.VMEM((ts + (CONV_WIDTH - 1) * bsz, d), F32),
                        pltpu.VMEM((ts, d), F32),
                        pltpu.VMEM((ts, d), F32),
                        pltpu.VMEM((ts, n_states2), F32),
                        pltpu.VMEM((bsz, d), F32),
                        pltpu.VMEM((bsz, n_states2), F32)],
        compiler_params=pltpu.CompilerParams(dimension_semantics=("arbitrary",),
                                             vmem_limit_bytes=VMEM_LIMIT),
        name="mixer",
    )(mod, *consts, *streamed, x3d)


def kernel(x, c, mod_w, mod_b, norm1_g, ffn1_w_up, ffn1_w_down, norm2_g, w_in, b_in, conv_w, conv_b, lru_w_r, lru_b_r, lru_w_i, lru_b_i, lru_lambda, proj_a, s5_a_re, s5_a_im, s5_log_dt, s5_b_re, s5_b_im, s5_c_re, s5_c_im, s5_d, glu_w, glu_b, proj_b, w_out, norm3_g, ffn2_w_up, ffn2_w_down, final_g):
    bsz, seq, d = x.shape
    assert bsz == SUBLANES, "time-major layout puts the batch on the 8 sublanes"
    depth = mod_w.shape[0]
    for l in range(depth):
        mod = _mod(c, mod_w[l], mod_b[l])
        lam, bd, cd = _s5_prep(s5_a_re[l], s5_a_im[l], s5_log_dt[l], s5_b_re[l], s5_b_im[l],
                               s5_c_re[l], s5_c_im[l])
        wr_bd, wi_bd = _gate_prep(lru_w_r[l], lru_w_i[l])
        last = l == depth - 1
        x2d = _ffn(x.reshape(bsz * seq, d), mod, 0, norm1_g[l], ffn1_w_up[l], ffn1_w_down[l], final_g,
                   seq=seq, final_norm=False)
        x3d = _mixer(x2d.reshape(bsz, seq, d), mod, 1, norm2_g[l], w_in[l], b_in[l], conv_w[l], conv_b[l],
                     wr_bd, lru_b_r[l], wi_bd, lru_b_i[l], lru_lambda[l], proj_a[l],
                     bd, cd, lam, s5_d[l], glu_w[l], glu_b[l], proj_b[l], w_out[l])
        x = _ffn(x3d.reshape(bsz * seq, d), mod, 2, norm3_g[l], ffn2_w_up[l], ffn2_w_down[l], final_g,
                 seq=seq, final_norm=last).reshape(bsz, seq, d)
    return x
```

```python
import functools
import math

import jax
import jax.numpy as jnp
from jax import lax
from jax.experimental import pallas as pl
from jax.experimental.pallas import tpu as pltpu

F32 = jnp.float32
BF16 = jnp.bfloat16

EPS = 1e-6
LRU_C = 8.0
CONV_WIDTH = 4
S5_GROUP = 16
S5_STATE = 64
MOD_PER_SUB = 3

SUBLANES = 8
LANES = 128
GATE_TILE = 256
S5_HALF_CH = 256
VMEM_LIMIT = 60 * 1024 * 1024

WEIGHT_STAGE_SLOTS = 4
FFN_TM = 512
FFN_TILES_PER_STEP = 2
FFN_HEAD_COLS = 256
MIX_TT = 64
S5_LANE_CHUNK = 512


def _sigmoid(x):
    return 0.5 * jnp.tanh(0.5 * x) + 0.5


def _gelu_tanh(x):
    c = math.sqrt(2.0 / math.pi)
    return 0.5 * x * (1.0 + jnp.tanh(c * (x + 0.044715 * (x * x * x))))


def _rms_norm(x, g):
    ms = jnp.mean(x * x, axis=-1, keepdims=True)
    return x * lax.rsqrt(ms + EPS) * g


def _const_spec(shape):
    nd = len(shape)
    return pl.BlockSpec(shape, lambda *_: (0,) * nd, pipeline_mode=pl.Buffered(1))


def _mod_kernel(c_ref, *refs):
    n_sub = (len(refs) - 1) // 2
    o_ref = refs[-1]
    c = c_ref[...]
    ca = (c * _sigmoid(c)).astype(BF16)
    for s in range(n_sub):
        o_ref[s] = jnp.dot(ca, refs[s][...].astype(BF16), preferred_element_type=F32) + refs[n_sub + s][...]


def _mod(c, mod_w, mod_b):
    bsz, d = c.shape
    n = mod_w.shape[1]
    n_sub = n // (MOD_PER_SUB * d)
    w_spec = lambda s: pl.BlockSpec((d, d), lambda j: (0, s * MOD_PER_SUB + j))
    b_spec = lambda s: pl.BlockSpec((1, d), lambda j: (0, s * MOD_PER_SUB + j))
    return pl.pallas_call(
        _mod_kernel,
        grid=(MOD_PER_SUB,),
        in_specs=[pl.BlockSpec((bsz, d), lambda j: (0, 0))]
                 + [w_spec(s) for s in range(n_sub)] + [b_spec(s) for s in range(n_sub)],
        out_specs=pl.BlockSpec((n_sub, bsz, d), lambda j: (0, 0, j)),
        out_shape=jax.ShapeDtypeStruct((n_sub, bsz, MOD_PER_SUB * d), F32),
        compiler_params=pltpu.CompilerParams(dimension_semantics=("arbitrary",),
                                             vmem_limit_bytes=VMEM_LIMIT),
        name="mod",
    )(c, *([mod_w] * n_sub), *([mod_b.reshape(1, n)] * n_sub))


def _cmul(ar, ai, br, bi):
    return ar * br - ai * bi, ar * bi + ai * br


def _s5_discretise(a_re, a_im, log_dt):
    dt = jnp.exp(log_dt)
    mag = jnp.exp(a_re * dt)
    lr = mag * jnp.cos(a_im * dt)
    li = mag * jnp.sin(a_im * dt)
    den = a_re * a_re + a_im * a_im
    nr = lr - 1.0
    cr = (nr * a_re + li * a_im) / den
    ci = (li * a_re - nr * a_im) / den
    return lr, li, cr, ci


def _prep_kernel(a3_g, a3_r, b2_t, c2_t, wr_ref, wi_ref, lam_ref, bd_ref, cd_ref, or_ref, oi_ref):
    _gate_prep_kernel(wr_ref, wi_ref, or_ref, oi_ref)
    lr, li, _, _ = _s5_discretise(a3_g[0], a3_g[1], a3_g[2])
    lam_ref[0] = lr
    lam_ref[1] = li

    _, _, cr, ci = _s5_discretise(a3_r[0], a3_r[1], a3_r[2])
    bb_re, bb_im = _cmul(cr, ci, b2_t[0], b2_t[1])
    cre_t, cim_t = c2_t.at[0], c2_t.at[1]
    half_states = (S5_HALF_CH // S5_GROUP) * S5_STATE
    erow = lax.broadcasted_iota(jnp.int32, (S5_STATE, half_states), 0)
    ecol = lax.broadcasted_iota(jnp.int32, (S5_STATE, half_states), 1)
    spread = jnp.where((ecol % S5_STATE) == erow, 1.0, 0.0).astype(BF16)
    brow_g = lax.broadcasted_iota(jnp.int32, (S5_HALF_CH, half_states), 0) // S5_GROUP
    bcol_g = lax.broadcasted_iota(jnp.int32, (S5_HALF_CH, half_states), 1) // S5_STATE
    bmask = brow_g == bcol_g
    crow_g = lax.broadcasted_iota(jnp.int32, (half_states, S5_HALF_CH), 0) // S5_STATE
    ccol_g = lax.broadcasted_iota(jnp.int32, (half_states, S5_HALF_CH), 1) // S5_GROUP
    cmask = crow_g == ccol_g
    reps = half_states // S5_STATE
    for h in range(bd_ref.shape[0]):
        rows = slice(h * S5_HALF_CH, (h + 1) * S5_HALF_CH)
        for k, bb in enumerate((bb_re, bb_im)):
            tiled = jnp.dot(bb[rows, :].astype(BF16), spread, preferred_element_type=F32)
            bd_ref[h, :, k * half_states:(k + 1) * half_states] = jnp.where(bmask, tiled, 0.0).astype(BF16)
        for k, (ct, sign) in enumerate(((cre_t, 1.0), (cim_t, -1.0))):
            tiled = jnp.concatenate([ct[:, rows]] * reps, axis=0)
            cd_ref[h, k * half_states:(k + 1) * half_states, :] = jnp.where(cmask, sign * tiled, 0.0).astype(BF16)


def _prep(a_re, a_im, log_dt, b_re, b_im, c_re, c_im, w_r, w_i):
    g, p = a_re.shape
    hch = b_re.shape[2]
    assert hch == S5_GROUP and p == S5_STATE
    width = g * hch
    halves = width // S5_HALF_CH
    half_states = (S5_HALF_CH // hch) * p
    heads, hd, _ = w_r.shape
    gate_tiles = jax.ShapeDtypeStruct((heads * hd // GATE_TILE, GATE_TILE, GATE_TILE), BF16)
    a3 = jnp.stack([a_re, a_im, jnp.broadcast_to(log_dt[:, None], (g, p))])
    b2_t = jnp.transpose(jnp.stack([b_re, b_im]), (0, 1, 3, 2)).reshape(2, width, p)
    c2_t = jnp.transpose(jnp.stack([c_re, c_im]), (0, 3, 1, 2)).reshape(2, p, width)
    lam, bd, cd, wr_bd, wi_bd = pl.pallas_call(
        _prep_kernel,
        out_shape=(jax.ShapeDtypeStruct((2, g, p), F32),
                   jax.ShapeDtypeStruct((halves, S5_HALF_CH, 2 * half_states), BF16),
                   jax.ShapeDtypeStruct((halves, 2 * half_states, S5_HALF_CH), BF16),
                   gate_tiles, gate_tiles),
        name="prep",
    )(a3, jnp.repeat(a3, hch, axis=1), b2_t, c2_t, w_r, w_i)
    return lam.reshape(2, g * p), bd, cd, wr_bd, wi_bd


def _gate_prep_kernel(wr_ref, wi_ref, or_ref, oi_ref):
    hd = wr_ref.shape[1]
    per_tile = GATE_TILE // hd
    erow = lax.broadcasted_iota(jnp.int32, (hd, GATE_TILE), 0)
    ecol = lax.broadcasted_iota(jnp.int32, (hd, GATE_TILE), 1)
    for w_ref, o_ref in ((wr_ref, or_ref), (wi_ref, oi_ref)):
        for a in range(o_ref.shape[0]):
            for b in range(per_tile):
                place = jnp.where(ecol == erow + b * hd, 1.0, 0.0).astype(BF16)
                o_ref[a, b * hd:(b + 1) * hd, :] = jnp.dot(
                    w_ref[a * per_tile + b].astype(BF16), place, preferred_element_type=F32).astype(BF16)


def _stream_cast(mats, stages, sem_ref):
    chunks = []
    used = [0] * len(stages)
    for src, dst, si in mats:
        rows = stages[si].shape[1]
        assert src.shape[0] % rows == 0 and src.shape[1] <= stages[si].shape[2]
        for r0 in range(0, src.shape[0], rows):
            slot = used[si] % WEIGHT_STAGE_SLOTS
            used[si] += 1
            chunks.append((src, dst, si, r0, rows, slot))

    def copy(k):
        src, _, si, r0, rows, slot = chunks[k]
        return pltpu.make_async_copy(src.at[pl.ds(r0, rows), :],
                                     stages[si].at[slot, :, pl.ds(0, src.shape[1])],
                                     sem_ref.at[si * WEIGHT_STAGE_SLOTS + slot])

    ahead = WEIGHT_STAGE_SLOTS - 1
    for k in range(min(ahead, len(chunks))):
        copy(k).start()
    for k, (src, dst, si, r0, rows, slot) in enumerate(chunks):
        if k + ahead < len(chunks):
            copy(k + ahead).start()
        copy(k).wait()
        dst[r0:r0 + rows, :] = stages[si][slot, :, 0:src.shape[1]].astype(BF16)


def _weight_scratch(weights, stage_shapes):
    return ([pltpu.VMEM(w.shape, BF16) for w in weights]
            + [pltpu.VMEM((WEIGHT_STAGE_SLOTS,) + tuple(s), F32) for s in stage_shapes]
            + [pltpu.SemaphoreType.DMA((WEIGHT_STAGE_SLOTS * len(stage_shapes),))])


def _ffn_kernel(x_ref, xn_ref, mod_ref, g_ref, fg_ref, wup_hbm, wdn_hbm, o_ref,
                wup_ref, wdn_ref, stage_up_ref, stage_dn_ref, sem_ref, u_ref, h0_ref, *,
                steps_per_seq, final_norm):
    d = x_ref.shape[1]
    tm = xn_ref.shape[0]
    n_sub = x_ref.shape[0] // tm
    f = wdn_ref.shape[0]
    c0 = h0_ref.shape[1]
    step = pl.program_id(0)
    batch = step // steps_per_seq
    batch_next = jnp.minimum(step + 1, pl.num_programs(0) - 1) // steps_per_seq

    def mod_row(b, k):
        return mod_ref[pl.ds(b, 1), k * d:(k + 1) * d]

    def modulated(xv, b):
        return (_rms_norm(xv, g_ref[...]) * (1.0 + mod_row(b, 1)) + mod_row(b, 0)).astype(BF16)

    @pl.when(step == 0)
    def _():
        _stream_cast([(wup_hbm, wup_ref, 0), (wdn_hbm, wdn_ref, 1)], [stage_up_ref, stage_dn_ref], sem_ref)
        u0 = modulated(x_ref[0:tm, :], batch)
        u_ref[...] = u0
        h0_ref[...] = jnp.dot(u0, wup_ref[:, :c0], preferred_element_type=F32)

    gt = mod_row(batch, 2)
    for k in range(n_sub):
        rows = slice(k * tm, (k + 1) * tm)
        x = x_ref[rows, :]
        h0 = h0_ref[...]
        h_rest = jnp.dot(u_ref[...], wup_ref[:, c0:], preferred_element_type=F32)
        if k + 1 < n_sub:
            un = modulated(x_ref[(k + 1) * tm:(k + 2) * tm, :], batch)
        else:
            un = modulated(xn_ref[...], batch_next)
        u_ref[...] = un
        h0_ref[...] = jnp.dot(un, wup_ref[:, :c0], preferred_element_type=F32)
        a = jnp.concatenate([h0, h_rest[:, :f - c0]], axis=1)
        b = h_rest[:, f - c0:]
        act = (a * _sigmoid(a)) * b
        y = jnp.dot(act.astype(BF16), wdn_ref[...], preferred_element_type=F32)
        out = x + (0.5 * gt) * y
        if final_norm:
            out = _rms_norm(out, fg_ref[...])
        o_ref[rows, :] = out


def _ffn(x2d, mod, sub, norm_g, w_up, w_down, final_g, *, seq, final_norm):
    t, d = x2d.shape
    f = w_down.shape[0]
    tm = FFN_TM
    rows_per_step = tm * FFN_TILES_PER_STEP
    assert seq % rows_per_step == 0, "a grid step must not straddle two batch elements"
    steps_per_seq = seq // rows_per_step
    n_steps = t // rows_per_step
    n_tiles = t // tm
    nxt = lambda i: jnp.minimum(FFN_TILES_PER_STEP * (i + 1), n_tiles - 1)
    kern = functools.partial(_ffn_kernel, steps_per_seq=steps_per_seq, final_norm=final_norm)
    return pl.pallas_call(
        kern,
        grid=(n_steps,),
        in_specs=[pl.BlockSpec((rows_per_step, d), lambda i: (i, 0)),
                  pl.BlockSpec((tm, d), lambda i: (nxt(i), 0)),
                  pl.BlockSpec((None,) + mod.shape[1:], lambda i: (sub, 0, 0), pipeline_mode=pl.Buffered(1)),
                  _const_spec((1, d)),
                  _const_spec((1, d)),
                  pl.BlockSpec(memory_space=pl.ANY),
                  pl.BlockSpec(memory_space=pl.ANY)],
        out_specs=pl.BlockSpec((rows_per_step, d), lambda i: (i, 0)),
        out_shape=jax.ShapeDtypeStruct((t, d), F32),
        scratch_shapes=_weight_scratch([w_up, w_down], [(d // 16, 2 * f), (f // 8, d)])
                       + [pltpu.VMEM((tm, d), BF16),
                          pltpu.VMEM((tm, FFN_HEAD_COLS), F32)],
        compiler_params=pltpu.CompilerParams(dimension_semantics=("arbitrary",),
                                             vmem_limit_bytes=VMEM_LIMIT),
        name="ffn_final" if final_norm else "ffn",
    )(x2d, x2d, mod, norm_g.reshape(1, d), final_g.reshape(1, d), w_up, w_down)


def _lru_scan(a_ref, u_ref, carry_ref):
    ts = u_ref.shape[0]
    h = carry_ref[...]
    for t in range(ts // SUBLANES):
        sl = slice(t * SUBLANES, (t + 1) * SUBLANES)
        h = a_ref[sl, :] * h + u_ref[sl, :]
        u_ref[sl, :] = h
    carry_ref[...] = h


def _s5_scan(u_ref, lam_ref, carry_ref, halves):
    ts = u_ref.shape[0]
    half_states = lam_ref.shape[1] // halves
    w = S5_LANE_CHUNK
    for h in range(halves):
        for j in range(half_states // w):
            cre = slice(2 * half_states * h + j * w, 2 * half_states * h + (j + 1) * w)
            cim = slice(2 * half_states * h + half_states + j * w, 2 * half_states * h + half_states + (j + 1) * w)
            lcs = slice(half_states * h + j * w, half_states * h + (j + 1) * w)
            lr = jnp.broadcast_to(lam_ref[0:1, lcs], (SUBLANES, w))
            li = jnp.broadcast_to(lam_ref[1:2, lcs], (SUBLANES, w))
            hr = carry_ref[:, cre]
            hi = carry_ref[:, cim]
            for t in range(ts // SUBLANES):
                sl = slice(t * SUBLANES, (t + 1) * SUBLANES)
                hr, hi = (u_ref[sl, cre] + (lr * hr - li * hi),
                          u_ref[sl, cim] + (lr * hi + li * hr))
                u_ref[sl, cre] = hr
                u_ref[sl, cim] = hi
            carry_ref[:, cre] = hr
            carry_ref[:, cim] = hi


def _mixer_kernel(mod_ref, g_ref, bin_ref, cw_ref, cb_ref,
                  wr_ref, br_ref, wi_ref, bi_ref, lamp_ref,
                  bd_ref, cd_ref, lam_ref, sd_ref, gb_ref,
                  win_hbm, pa_hbm, gw_hbm, pb_hbm, wo_hbm, x_hbm,
                  o_hbm,
                  win_ref, pa_ref, gw_ref, pb_ref, wo_ref, stage_in_ref, stage_sq_ref, sem_ref,
                  xt_ref, ot_ref, sem_x_ref, sem_o_ref,
                  xpad_ref, a_ref, u_ref, s5_ref, hl_ref, hs_ref):
    _, tt, nb, d = xt_ref.shape
    ts = nb * tt
    w5 = sd_ref.shape[1]
    halves = bd_ref.shape[0]
    hs2 = bd_ref.shape[2]
    pad = (CONV_WIDTH - 1) * nb
    step = pl.program_id(0)
    n_steps = pl.num_programs(0)
    slot = step % 2

    def x_copies(tile, sl):
        return [pltpu.make_async_copy(x_hbm.at[b, pl.ds(tile * tt, tt), :], xt_ref.at[sl, :, b, :],
                                      sem_x_ref.at[sl]) for b in range(nb)]

    def o_copies(tile, sl):
        return [pltpu.make_async_copy(ot_ref.at[sl, :, b, :], o_hbm.at[b, pl.ds(tile * tt, tt), :],
                                      sem_o_ref.at[sl]) for b in range(nb)]

    @pl.when(step == 0)
    def _():
        for cp in x_copies(0, 0):
            cp.start()
        xpad_ref[0:pad, :] = jnp.zeros((pad, d), F32)
        hl_ref[...] = jnp.zeros_like(hl_ref)
        hs_ref[...] = jnp.zeros_like(hs_ref)
        _stream_cast([(win_hbm, win_ref, 0), (pa_hbm, pa_ref, 1), (gw_hbm, gw_ref, 1), (pb_hbm, pb_ref, 1),
                      (wo_hbm, wo_ref, 1)], [stage_in_ref, stage_sq_ref], sem_ref)

    @pl.when(step + 1 < n_steps)
    def _():
        for cp in x_copies(step + 1, 1 - slot):
            cp.start()

    for cp in x_copies(step, slot):
        cp.wait()

    def per_batch(v):
        return v[None, :, :]

    x3 = xt_ref[slot]
    sh = per_batch(mod_ref[:, 0:d])
    sc = per_batch(mod_ref[:, d:2 * d])
    gt = per_batch(mod_ref[:, 2 * d:3 * d])
    u_bf = (_rms_norm(x3, g_ref[...]) * (1.0 + sc) + sh).reshape(ts, d).astype(BF16)

    def in_proj(lo, hi):
        return jnp.dot(u_bf, win_ref[:, lo:hi], preferred_element_type=F32) + bin_ref[:, lo:hi]

    xpad_ref[pad:pad + ts, :] = in_proj(0, d)
    xb = in_proj(2 * d, 2 * d + w5)
    xb_bf = xb.astype(BF16)
    for h in range(halves):
        s5_ref[:, h * hs2:(h + 1) * hs2] = jnp.dot(
            xb_bf[:, h * S5_HALF_CH:(h + 1) * S5_HALF_CH], bd_ref[h], preferred_element_type=F32)
    ga_pre = in_proj(d, 2 * d)

    xc = cb_ref[...]
    for k in range(CONV_WIDTH):
        xc = xc + xpad_ref[k * nb:k * nb + ts, :] * cw_ref[k:k + 1, :]
    xpad_ref[0:pad, :] = xpad_ref[ts:ts + pad, :]
    xc_bf = xc.astype(BF16)
    gates = []
    for w_ref, b_ref in ((wr_ref, br_ref), (wi_ref, bi_ref)):
        cols = [jnp.dot(xc_bf[:, j * GATE_TILE:(j + 1) * GATE_TILE], w_ref[j], preferred_element_type=F32)
                for j in range(w_ref.shape[0])]
        gates.append(_sigmoid(jnp.concatenate(cols, axis=1) + b_ref[...]))
    r_gate, i_gate = gates
    mga_pre = in_proj(2 * d + w5, 3 * d + w5)
    nl = -lamp_ref[...]
    softplus = jnp.maximum(nl, 0.0) + jnp.log1p(jnp.exp(-jnp.abs(nl)))
    neg_log_a = (LRU_C * softplus) * r_gate
    a_gate = jnp.exp(-neg_log_a)
    a_ref[...] = a_gate
    one_m_a2 = jnp.tanh(neg_log_a) * (1.0 + a_gate * a_gate)
    mult = jnp.where(one_m_a2 > 0.0, one_m_a2 * lax.rsqrt(one_m_a2), 0.0)
    u_ref[...] = mult * (i_gate * xc)
    _lru_scan(a_ref, u_ref, hl_ref)
    ya = u_ref[...] * _gelu_tanh(ga_pre)
    proj_a = jnp.dot(ya.astype(BF16), pa_ref[...], preferred_element_type=F32)
    mgb_pre = in_proj(3 * d + w5, 4 * d + w5)

    _s5_scan(s5_ref, lam_ref, hs_ref, halves)
    ys = [jnp.dot(s5_ref[:, h * hs2:(h + 1) * hs2].astype(BF16), cd_ref[h], preferred_element_type=F32)
          for h in range(halves)]
    yb = _gelu_tanh(jnp.concatenate(ys, axis=1) + sd_ref[...] * xb)
    yb = yb * _sigmoid(jnp.dot(yb.astype(BF16), gw_ref[...], preferred_element_type=F32) + gb_ref[...])
    proj_b = jnp.dot(yb.astype(BF16), pb_ref[...], preferred_element_type=F32)

    m = _sigmoid(mga_pre) * proj_a + _sigmoid(mgb_pre) * proj_b
    y = jnp.dot(m.astype(BF16), wo_ref[...], preferred_element_type=F32)
    out = x3 + gt * y.reshape(tt, nb, d)

    @pl.when(step >= 2)
    def _():
        for cp in o_copies(step - 2, slot):
            cp.wait()

    ot_ref[slot] = out
    for cp in o_copies(step, slot):
        cp.start()

    @pl.when(step == n_steps - 1)
    def _():
        for cp in o_copies(step - 1, 1 - slot):
            cp.wait()
        for cp in o_copies(step, slot):
            cp.wait()


def _mixer(x3d, mod, sub, norm_g, w_in, b_in, conv_w, conv_b, wr_bd, b_r, wi_bd, b_i, lam_p, proj_a,
           bd, cd, lam, s5_d, glu_w, glu_b, proj_b, w_out):
    bsz, seq, d = x3d.shape
    tt = MIX_TT
    ts = tt * bsz
    n_states2 = bd.shape[0] * bd.shape[2]
    row = lambda v: v.reshape(1, -1)
    consts = [row(norm_g), row(b_in), conv_w, row(conv_b), wr_bd, row(b_r), wi_bd, row(b_i),
              row(lam_p), bd, cd, lam, row(s5_d), row(glu_b)]
    streamed = [w_in, proj_a, glu_w, proj_b, w_out]
    assert seq // tt >= 2, "the write-back drain expects at least two sequence tiles"
    return pl.pallas_call(
        _mixer_kernel,
        grid=(seq // tt,),
        in_specs=[pl.BlockSpec((None,) + mod.shape[1:], lambda i: (sub, 0, 0), pipeline_mode=pl.Buffered(1))]
                 + [_const_spec(c.shape) for c in consts]
                 + [pl.BlockSpec(memory_space=pl.ANY)] * (len(streamed) + 1),
        out_specs=pl.BlockSpec(memory_space=pl.ANY),
        out_shape=jax.ShapeDtypeStruct((bsz, seq, d), F32),
        scratch_shapes=_weight_scratch(streamed, [(d // 32, w_in.shape[1]), (d // 8, d)])
                       + [pltpu.VMEM((2, tt, bsz, d), F32),
                          pltpu.VMEM((2, tt, bsz, d), F32),
                          pltpu.SemaphoreType.DMA((2,)),
                          pltpu.SemaphoreType.DMA((2,)),
                        pltpu.VMEM((ts + (CONV_WIDTH - 1) * bsz, d), F32),
                        pltpu.VMEM((ts, d), F32),
                        pltpu.VMEM((ts, d), F32),
                        pltpu.VMEM((ts, n_states2), F32),
                        pltpu.VMEM((bsz, d), F32),
                        pltpu.VMEM((bsz, n_states2), F32)],
        compiler_params=pltpu.CompilerParams(dimension_semantics=("arbitrary",),
                                             vmem_limit_bytes=VMEM_LIMIT),
        name="mixer",
    )(mod, *consts, *streamed, x3d)


def kernel(x, c, mod_w, mod_b, norm1_g, ffn1_w_up, ffn1_w_down, norm2_g, w_in, b_in, conv_w, conv_b, lru_w_r, lru_b_r, lru_w_i, lru_b_i, lru_lambda, proj_a, s5_a_re, s5_a_im, s5_log_dt, s5_b_re, s5_b_im, s5_c_re, s5_c_im, s5_d, glu_w, glu_b, proj_b, w_out, norm3_g, ffn2_w_up, ffn2_w_down, final_g):
    bsz, seq, d = x.shape
    assert bsz == SUBLANES, "time-major layout puts the batch on the 8 sublanes"
    depth = mod_w.shape[0]
    for l in range(depth):
        mod = _mod(c, mod_w[l], mod_b[l])
        lam, bd, cd, wr_bd, wi_bd = _prep(s5_a_re[l], s5_a_im[l], s5_log_dt[l], s5_b_re[l], s5_b_im[l],
                                          s5_c_re[l], s5_c_im[l], lru_w_r[l], lru_w_i[l])
        last = l == depth - 1
        x2d = _ffn(x.reshape(bsz * seq, d), mod, 0, norm1_g[l], ffn1_w_up[l], ffn1_w_down[l], final_g,
                   seq=seq, final_norm=False)
        x3d = _mixer(x2d.reshape(bsz, seq, d), mod, 1, norm2_g[l], w_in[l], b_in[l], conv_w[l], conv_b[l],
                     wr_bd, lru_b_r[l], wi_bd, lru_b_i[l], lru_lambda[l], proj_a[l],
                     bd, cd, lam, s5_d[l], glu_w[l], glu_b[l], proj_b[l], w_out[l])
        x = _ffn(x3d.reshape(bsz * seq, d), mod, 2, norm3_g[l], ffn2_w_up[l], ffn2_w_down[l], final_g,
                 seq=seq, final_norm=last).reshape(bsz, seq, d)
    return x
```

```python
import functools
import math

import jax
import jax.numpy as jnp
from jax import lax
from jax.experimental import pallas as pl
from jax.experimental.pallas import tpu as pltpu

F32 = jnp.float32
BF16 = jnp.bfloat16

EPS = 1e-6
LRU_C = 8.0
CONV_WIDTH = 4
S5_GROUP = 16
S5_STATE = 64
MOD_PER_SUB = 3

SUBLANES = 8
LANES = 128
GATE_TILE = 256
S5_HALF_CH = 256
VMEM_LIMIT = 60 * 1024 * 1024

WEIGHT_STAGE_SLOTS = 4
FFN_TM = 512
FFN_TILES_PER_STEP = 2
FFN_HEAD_COLS = 256
MIX_TT = 64
S5_LANE_CHUNK = 512


def _sigmoid(x):
    return 0.5 * jnp.tanh(0.5 * x) + 0.5


def _gelu_tanh(x):
    c = math.sqrt(2.0 / math.pi)
    return 0.5 * x * (1.0 + jnp.tanh(c * (x + 0.044715 * (x * x * x))))


def _rms_norm(x, g):
    ms = jnp.mean(x * x, axis=-1, keepdims=True)
    return x * lax.rsqrt(ms + EPS) * g


def _const_spec(shape):
    nd = len(shape)
    return pl.BlockSpec(shape, lambda *_: (0,) * nd, pipeline_mode=pl.Buffered(1))


N_PREP_IN, N_PREP_OUT = 6, 5


def _mod_prep_kernel(c_ref, *refs):
    n_sub = (len(refs) - N_PREP_IN - 1 - N_PREP_OUT) // 2
    prep_in = refs[2 * n_sub:2 * n_sub + N_PREP_IN]
    o_ref = refs[2 * n_sub + N_PREP_IN]
    prep_out = refs[2 * n_sub + N_PREP_IN + 1:]

    @pl.when(pl.program_id(0) == 0)
    def _():
        _prep_kernel(*prep_in, *prep_out)

    c = c_ref[...]
    ca = (c * _sigmoid(c)).astype(BF16)
    for s in range(n_sub):
        o_ref[s] = jnp.dot(ca, refs[s][...].astype(BF16), preferred_element_type=F32) + refs[n_sub + s][...]


def _mod_prep(c, mod_w, mod_b, a_re, a_im, log_dt, b_re, b_im, c_re, c_im, w_r, w_i):
    bsz, d = c.shape
    n = mod_w.shape[1]
    n_sub = n // (MOD_PER_SUB * d)
    g, p = a_re.shape
    hch = b_re.shape[2]
    assert hch == S5_GROUP and p == S5_STATE
    width = g * hch
    halves = width // S5_HALF_CH
    half_states = (S5_HALF_CH // hch) * p
    heads, hd, _ = w_r.shape
    a3 = jnp.stack([a_re, a_im, jnp.broadcast_to(log_dt[:, None], (g, p))])
    b2_t = jnp.transpose(jnp.stack([b_re, b_im]), (0, 1, 3, 2)).reshape(2, width, p)
    c2_t = jnp.transpose(jnp.stack([c_re, c_im]), (0, 3, 1, 2)).reshape(2, p, width)
    prep_in = [a3, jnp.repeat(a3, hch, axis=1), b2_t, c2_t, w_r, w_i]
    gate_tiles = jax.ShapeDtypeStruct((heads * hd // GATE_TILE, GATE_TILE, GATE_TILE), BF16)
    prep_out = [jax.ShapeDtypeStruct((2, g, p), F32),
                jax.ShapeDtypeStruct((halves, S5_HALF_CH, 2 * half_states), BF16),
                jax.ShapeDtypeStruct((halves, 2 * half_states, S5_HALF_CH), BF16),
                gate_tiles, gate_tiles]
    assert len(prep_in) == N_PREP_IN and len(prep_out) == N_PREP_OUT
    whole = lambda a: pl.BlockSpec(a.shape, lambda j, nd=len(a.shape): (0,) * nd)
    w_spec = lambda s: pl.BlockSpec((d, d), lambda j: (0, s * MOD_PER_SUB + j))
    b_spec = lambda s: pl.BlockSpec((1, d), lambda j: (0, s * MOD_PER_SUB + j))
    mod, lam, bd, cd, wr_bd, wi_bd = pl.pallas_call(
        _mod_prep_kernel,
        grid=(MOD_PER_SUB,),
        in_specs=[pl.BlockSpec((bsz, d), lambda j: (0, 0))]
                 + [w_spec(s) for s in range(n_sub)] + [b_spec(s) for s in range(n_sub)]
                 + [_const_spec(a.shape) for a in prep_in],
        out_specs=[pl.BlockSpec((n_sub, bsz, d), lambda j: (0, 0, j))] + [whole(a) for a in prep_out],
        out_shape=[jax.ShapeDtypeStruct((n_sub, bsz, MOD_PER_SUB * d), F32)] + prep_out,
        compiler_params=pltpu.CompilerParams(dimension_semantics=("arbitrary",),
                                             vmem_limit_bytes=VMEM_LIMIT),
        name="mod_prep",
    )(c, *([mod_w] * n_sub), *([mod_b.reshape(1, n)] * n_sub), *prep_in)
    return mod, lam.reshape(2, g * p), bd, cd, wr_bd, wi_bd


def _cmul(ar, ai, br, bi):
    return ar * br - ai * bi, ar * bi + ai * br


def _s5_discretise(a_re, a_im, log_dt):
    dt = jnp.exp(log_dt)
    mag = jnp.exp(a_re * dt)
    lr = mag * jnp.cos(a_im * dt)
    li = mag * jnp.sin(a_im * dt)
    den = a_re * a_re + a_im * a_im
    nr = lr - 1.0
    cr = (nr * a_re + li * a_im) / den
    ci = (li * a_re - nr * a_im) / den
    return lr, li, cr, ci


def _prep_kernel(a3_g, a3_r, b2_t, c2_t, wr_ref, wi_ref, lam_ref, bd_ref, cd_ref, or_ref, oi_ref):
    _gate_prep_kernel(wr_ref, wi_ref, or_ref, oi_ref)
    lr, li, _, _ = _s5_discretise(a3_g[0], a3_g[1], a3_g[2])
    lam_ref[0] = lr
    lam_ref[1] = li

    _, _, cr, ci = _s5_discretise(a3_r[0], a3_r[1], a3_r[2])
    bb_re, bb_im = _cmul(cr, ci, b2_t[0], b2_t[1])
    cre_t, cim_t = c2_t.at[0], c2_t.at[1]
    half_states = (S5_HALF_CH // S5_GROUP) * S5_STATE
    erow = lax.broadcasted_iota(jnp.int32, (S5_STATE, half_states), 0)
    ecol = lax.broadcasted_iota(jnp.int32, (S5_STATE, half_states), 1)
    spread = jnp.where((ecol % S5_STATE) == erow, 1.0, 0.0).astype(BF16)
    brow_g = lax.broadcasted_iota(jnp.int32, (S5_HALF_CH, half_states), 0) // S5_GROUP
    bcol_g = lax.broadcasted_iota(jnp.int32, (S5_HALF_CH, half_states), 1) // S5_STATE
    bmask = brow_g == bcol_g
    crow_g = lax.broadcasted_iota(jnp.int32, (half_states, S5_HALF_CH), 0) // S5_STATE
    ccol_g = lax.broadcasted_iota(jnp.int32, (half_states, S5_HALF_CH), 1) // S5_GROUP
    cmask = crow_g == ccol_g
    reps = half_states // S5_STATE
    for h in range(bd_ref.shape[0]):
        rows = slice(h * S5_HALF_CH, (h + 1) * S5_HALF_CH)
        for k, bb in enumerate((bb_re, bb_im)):
            tiled = jnp.dot(bb[rows, :].astype(BF16), spread, preferred_element_type=F32)
            bd_ref[h, :, k * half_states:(k + 1) * half_states] = jnp.where(bmask, tiled, 0.0).astype(BF16)
        for k, (ct, sign) in enumerate(((cre_t, 1.0), (cim_t, -1.0))):
            tiled = jnp.concatenate([ct[:, rows]] * reps, axis=0)
            cd_ref[h, k * half_states:(k + 1) * half_states, :] = jnp.where(cmask, sign * tiled, 0.0).astype(BF16)


def _gate_prep_kernel(wr_ref, wi_ref, or_ref, oi_ref):
    hd = wr_ref.shape[1]
    per_tile = GATE_TILE // hd
    erow = lax.broadcasted_iota(jnp.int32, (hd, GATE_TILE), 0)
    ecol = lax.broadcasted_iota(jnp.int32, (hd, GATE_TILE), 1)
    for w_ref, o_ref in ((wr_ref, or_ref), (wi_ref, oi_ref)):
        for a in range(o_ref.shape[0]):
            for b in range(per_tile):
                place = jnp.where(ecol == erow + b * hd, 1.0, 0.0).astype(BF16)
                o_ref[a, b * hd:(b + 1) * hd, :] = jnp.dot(
                    w_ref[a * per_tile + b].astype(BF16), place, preferred_element_type=F32).astype(BF16)


def _stream_cast(mats, stages, sem_ref):
    chunks = []
    used = [0] * len(stages)
    for src, dst, si in mats:
        rows = stages[si].shape[1]
        assert src.shape[0] % rows == 0 and src.shape[1] <= stages[si].shape[2]
        for r0 in range(0, src.shape[0], rows):
            slot = used[si] % WEIGHT_STAGE_SLOTS
            used[si] += 1
            chunks.append((src, dst, si, r0, rows, slot))

    def copy(k):
        src, _, si, r0, rows, slot = chunks[k]
        return pltpu.make_async_copy(src.at[pl.ds(r0, rows), :],
                                     stages[si].at[slot, :, pl.ds(0, src.shape[1])],
                                     sem_ref.at[si * WEIGHT_STAGE_SLOTS + slot])

    ahead = WEIGHT_STAGE_SLOTS - 1
    for k in range(min(ahead, len(chunks))):
        copy(k).start()
    for k, (src, dst, si, r0, rows, slot) in enumerate(chunks):
        if k + ahead < len(chunks):
            copy(k + ahead).start()
        copy(k).wait()
        dst[r0:r0 + rows, :] = stages[si][slot, :, 0:src.shape[1]].astype(BF16)


def _weight_scratch(weights, stage_shapes):
    return ([pltpu.VMEM(w.shape, BF16) for w in weights]
            + [pltpu.VMEM((WEIGHT_STAGE_SLOTS,) + tuple(s), F32) for s in stage_shapes]
            + [pltpu.SemaphoreType.DMA((WEIGHT_STAGE_SLOTS * len(stage_shapes),))])


def _ffn_kernel(x_ref, xn_ref, mod_ref, g_ref, fg_ref, wup_hbm, wdn_hbm, o_ref,
                wup_ref, wdn_ref, stage_up_ref, stage_dn_ref, sem_ref, u_ref, h0_ref, *,
                steps_per_seq, final_norm):
    d = x_ref.shape[1]
    tm = xn_ref.shape[0]
    n_sub = x_ref.shape[0] // tm
    f = wdn_ref.shape[0]
    c0 = h0_ref.shape[1]
    step = pl.program_id(0)
    batch = step // steps_per_seq
    batch_next = jnp.minimum(step + 1, pl.num_programs(0) - 1) // steps_per_seq

    def mod_row(b, k):
        return mod_ref[pl.ds(b, 1), k * d:(k + 1) * d]

    def modulated(xv, b):
        return (_rms_norm(xv, g_ref[...]) * (1.0 + mod_row(b, 1)) + mod_row(b, 0)).astype(BF16)

    @pl.when(step == 0)
    def _():
        _stream_cast([(wup_hbm, wup_ref, 0), (wdn_hbm, wdn_ref, 1)], [stage_up_ref, stage_dn_ref], sem_ref)
        u0 = modulated(x_ref[0:tm, :], batch)
        u_ref[...] = u0
        h0_ref[...] = jnp.dot(u0, wup_ref[:, :c0], preferred_element_type=F32)

    gt = mod_row(batch, 2)
    for k in range(n_sub):
        rows = slice(k * tm, (k + 1) * tm)
        x = x_ref[rows, :]
        h0 = h0_ref[...]
        h_rest = jnp.dot(u_ref[...], wup_ref[:, c0:], preferred_element_type=F32)
        if k + 1 < n_sub:
            un = modulated(x_ref[(k + 1) * tm:(k + 2) * tm, :], batch)
        else:
            un = modulated(xn_ref[...], batch_next)
        u_ref[...] = un
        h0_ref[...] = jnp.dot(un, wup_ref[:, :c0], preferred_element_type=F32)
        a = jnp.concatenate([h0, h_rest[:, :f - c0]], axis=1)
        b = h_rest[:, f - c0:]
        act = (a * _sigmoid(a)) * b
        y = jnp.dot(act.astype(BF16), wdn_ref[...], preferred_element_type=F32)
        out = x + (0.5 * gt) * y
        if final_norm:
            out = _rms_norm(out, fg_ref[...])
        o_ref[rows, :] = out


def _ffn(x2d, mod, sub, norm_g, w_up, w_down, final_g, *, seq, final_norm):
    t, d = x2d.shape
    f = w_down.shape[0]
    tm = FFN_TM
    rows_per_step = tm * FFN_TILES_PER_STEP
    assert seq % rows_per_step == 0, "a grid step must not straddle two batch elements"
    steps_per_seq = seq // rows_per_step
    n_steps = t // rows_per_step
    n_tiles = t // tm
    nxt = lambda i: jnp.minimum(FFN_TILES_PER_STEP * (i + 1), n_tiles - 1)
    kern = functools.partial(_ffn_kernel, steps_per_seq=steps_per_seq, final_norm=final_norm)
    return pl.pallas_call(
        kern,
        grid=(n_steps,),
        in_specs=[pl.BlockSpec((rows_per_step, d), lambda i: (i, 0)),
                  pl.BlockSpec((tm, d), lambda i: (nxt(i), 0)),
                  pl.BlockSpec((None,) + mod.shape[1:], lambda i: (sub, 0, 0), pipeline_mode=pl.Buffered(1)),
                  _const_spec((1, d)),
                  _const_spec((1, d)),
                  pl.BlockSpec(memory_space=pl.ANY),
                  pl.BlockSpec(memory_space=pl.ANY)],
        out_specs=pl.BlockSpec((rows_per_step, d), lambda i: (i, 0)),
        out_shape=jax.ShapeDtypeStruct((t, d), F32),
        scratch_shapes=_weight_scratch([w_up, w_down], [(d // 16, 2 * f), (f // 8, d)])
                       + [pltpu.VMEM((tm, d), BF16),
                          pltpu.VMEM((tm, FFN_HEAD_COLS), F32)],
        compiler_params=pltpu.CompilerParams(dimension_semantics=("arbitrary",),
                                             vmem_limit_bytes=VMEM_LIMIT),
        name="ffn_final" if final_norm else "ffn",
    )(x2d, x2d, mod, norm_g.reshape(1, d), final_g.reshape(1, d), w_up, w_down)


def _lru_scan(a_ref, u_ref, carry_ref):
    ts = u_ref.shape[0]
    h = carry_ref[...]
    for t in range(ts // SUBLANES):
        sl = slice(t * SUBLANES, (t + 1) * SUBLANES)
        h = a_ref[sl, :] * h + u_ref[sl, :]
        u_ref[sl, :] = h
    carry_ref[...] = h


def _s5_scan(u_ref, lam_ref, carry_ref, halves):
    ts = u_ref.shape[0]
    half_states = lam_ref.shape[1] // halves
    w = S5_LANE_CHUNK
    for h in range(halves):
        for j in range(half_states // w):
            cre = slice(2 * half_states * h + j * w, 2 * half_states * h + (j + 1) * w)
            cim = slice(2 * half_states * h + half_states + j * w, 2 * half_states * h + half_states + (j + 1) * w)
            lcs = slice(half_states * h + j * w, half_states * h + (j + 1) * w)
            lr = jnp.broadcast_to(lam_ref[0:1, lcs], (SUBLANES, w))
            li = jnp.broadcast_to(lam_ref[1:2, lcs], (SUBLANES, w))
            hr = carry_ref[:, cre]
            hi = carry_ref[:, cim]
            for t in range(ts // SUBLANES):
                sl = slice(t * SUBLANES, (t + 1) * SUBLANES)
                hr, hi = (u_ref[sl, cre] + (lr * hr - li * hi),
                          u_ref[sl, cim] + (lr * hi + li * hr))
                u_ref[sl, cre] = hr
                u_ref[sl, cim] = hi
            carry_ref[:, cre] = hr
            carry_ref[:, cim] = hi


def _mixer_kernel(mod_ref, g_ref, bin_ref, cw_ref, cb_ref,
                  wr_ref, br_ref, wi_ref, bi_ref, lamp_ref,
                  bd_ref, cd_ref, lam_ref, sd_ref, gb_ref,
                  win_hbm, pa_hbm, gw_hbm, pb_hbm, wo_hbm, x_hbm,
                  o_hbm,
                  win_ref, pa_ref, gw_ref, pb_ref, wo_ref, stage_in_ref, stage_sq_ref, sem_ref,
                  xt_ref, ot_ref, sem_x_ref, sem_o_ref,
                  xpad_ref, a_ref, u_ref, s5_ref, hl_ref, hs_ref):
    _, tt, nb, d = xt_ref.shape
    ts = nb * tt
    w5 = sd_ref.shape[1]
    halves = bd_ref.shape[0]
    hs2 = bd_ref.shape[2]
    pad = (CONV_WIDTH - 1) * nb
    step = pl.program_id(0)
    n_steps = pl.num_programs(0)
    slot = step % 2

    def x_copies(tile, sl):
        return [pltpu.make_async_copy(x_hbm.at[b, pl.ds(tile * tt, tt), :], xt_ref.at[sl, :, b, :],
                                      sem_x_ref.at[sl]) for b in range(nb)]

    def o_copies(tile, sl):
        return [pltpu.make_async_copy(ot_ref.at[sl, :, b, :], o_hbm.at[b, pl.ds(tile * tt, tt), :],
                                      sem_o_ref.at[sl]) for b in range(nb)]

    @pl.when(step == 0)
    def _():
        for cp in x_copies(0, 0):
            cp.start()
        xpad_ref[0:pad, :] = jnp.zeros((pad, d), F32)
        hl_ref[...] = jnp.zeros_like(hl_ref)
        hs_ref[...] = jnp.zeros_like(hs_ref)
        _stream_cast([(win_hbm, win_ref, 0), (pa_hbm, pa_ref, 1), (gw_hbm, gw_ref, 1), (pb_hbm, pb_ref, 1),
                      (wo_hbm, wo_ref, 1)], [stage_in_ref, stage_sq_ref], sem_ref)

    @pl.when(step + 1 < n_steps)
    def _():
        for cp in x_copies(step + 1, 1 - slot):
            cp.start()

    for cp in x_copies(step, slot):
        cp.wait()

    def per_batch(v):
        return v[None, :, :]

    x3 = xt_ref[slot]
    sh = per_batch(mod_ref[:, 0:d])
    sc = per_batch(mod_ref[:, d:2 * d])
    gt = per_batch(mod_ref[:, 2 * d:3 * d])
    u_bf = (_rms_norm(x3, g_ref[...]) * (1.0 + sc) + sh).reshape(ts, d).astype(BF16)

    def in_proj(lo, hi):
        return jnp.dot(u_bf, win_ref[:, lo:hi], preferred_element_type=F32) + bin_ref[:, lo:hi]

    xpad_ref[pad:pad + ts, :] = in_proj(0, d)
    xb = in_proj(2 * d, 2 * d + w5)
    xb_bf = xb.astype(BF16)
    for h in range(halves):
        s5_ref[:, h * hs2:(h + 1) * hs2] = jnp.dot(
            xb_bf[:, h * S5_HALF_CH:(h + 1) * S5_HALF_CH], bd_ref[h], preferred_element_type=F32)
    ga_pre = in_proj(d, 2 * d)

    xc = cb_ref[...]
    for k in range(CONV_WIDTH):
        xc = xc + xpad_ref[k * nb:k * nb + ts, :] * cw_ref[k:k + 1, :]
    xpad_ref[0:pad, :] = xpad_ref[ts:ts + pad, :]
    xc_bf = xc.astype(BF16)
    gates = []
    for w_ref, b_ref in ((wr_ref, br_ref), (wi_ref, bi_ref)):
        cols = [jnp.dot(xc_bf[:, j * GATE_TILE:(j + 1) * GATE_TILE], w_ref[j], preferred_element_type=F32)
                for j in range(w_ref.shape[0])]
        gates.append(_sigmoid(jnp.concatenate(cols, axis=1) + b_ref[...]))
    r_gate, i_gate = gates
    mga_pre = in_proj(2 * d + w5, 3 * d + w5)
    nl = -lamp_ref[...]
    softplus = jnp.maximum(nl, 0.0) + jnp.log1p(jnp.exp(-jnp.abs(nl)))
    neg_log_a = (LRU_C * softplus) * r_gate
    a_gate = jnp.exp(-neg_log_a)
    a_ref[...] = a_gate
    one_m_a2 = jnp.tanh(neg_log_a) * (1.0 + a_gate * a_gate)
    mult = jnp.where(one_m_a2 > 0.0, one_m_a2 * lax.rsqrt(one_m_a2), 0.0)
    u_ref[...] = mult * (i_gate * xc)
    _lru_scan(a_ref, u_ref, hl_ref)
    ya = u_ref[...] * _gelu_tanh(ga_pre)
    proj_a = jnp.dot(ya.astype(BF16), pa_ref[...], preferred_element_type=F32)
    mgb_pre = in_proj(3 * d + w5, 4 * d + w5)

    _s5_scan(s5_ref, lam_ref, hs_ref, halves)
    ys = [jnp.dot(s5_ref[:, h * hs2:(h + 1) * hs2].astype(BF16), cd_ref[h], preferred_element_type=F32)
          for h in range(halves)]
    yb = _gelu_tanh(jnp.concatenate(ys, axis=1) + sd_ref[...] * xb)
    yb = yb * _sigmoid(jnp.dot(yb.astype(BF16), gw_ref[...], preferred_element_type=F32) + gb_ref[...])
    proj_b = jnp.dot(yb.astype(BF16), pb_ref[...], preferred_element_type=F32)

    m = _sigmoid(mga_pre) * proj_a + _sigmoid(mgb_pre) * proj_b
    y = jnp.dot(m.astype(BF16), wo_ref[...], preferred_element_type=F32)
    out = x3 + gt * y.reshape(tt, nb, d)

    @pl.when(step >= 2)
    def _():
        for cp in o_copies(step - 2, slot):
            cp.wait()

    ot_ref[slot] = out
    for cp in o_copies(step, slot):
        cp.start()

    @pl.when(step == n_steps - 1)
    def _():
        for cp in o_copies(step - 1, 1 - slot):
            cp.wait()
        for cp in o_copies(step, slot):
            cp.wait()


def _mixer(x3d, mod, sub, norm_g, w_in, b_in, conv_w, conv_b, wr_bd, b_r, wi_bd, b_i, lam_p, proj_a,
           bd, cd, lam, s5_d, glu_w, glu_b, proj_b, w_out):
    bsz, seq, d = x3d.shape
    tt = MIX_TT
    ts = tt * bsz
    n_states2 = bd.shape[0] * bd.shape[2]
    row = lambda v: v.reshape(1, -1)
    consts = [row(norm_g), row(b_in), conv_w, row(conv_b), wr_bd, row(b_r), wi_bd, row(b_i),
              row(lam_p), bd, cd, lam, row(s5_d), row(glu_b)]
    streamed = [w_in, proj_a, glu_w, proj_b, w_out]
    assert seq // tt >= 2, "the write-back drain expects at least two sequence tiles"
    return pl.pallas_call(
        _mixer_kernel,
        grid=(seq // tt,),
        in_specs=[pl.BlockSpec((None,) + mod.shape[1:], lambda i: (sub, 0, 0), pipeline_mode=pl.Buffered(1))]
                 + [_const_spec(c.shape) for c in consts]
                 + [pl.BlockSpec(memory_space=pl.ANY)] * (len(streamed) + 1),
        out_specs=pl.BlockSpec(memory_space=pl.ANY),
        out_shape=jax.ShapeDtypeStruct((bsz, seq, d), F32),
        scratch_shapes=_weight_scratch(streamed, [(d // 32, w_in.shape[1]), (d // 8, d)])
                       + [pltpu.VMEM((2, tt, bsz, d), F32),
                          pltpu.VMEM((2, tt, bsz, d), F32),
                          pltpu.SemaphoreType.DMA((2,)),
                          pltpu.SemaphoreType.DMA((2,)),
                        pltpu.VMEM((ts + (CONV_WIDTH - 1) * bsz, d), F32),
                        pltpu.VMEM((ts, d), F32),
                        pltpu.VMEM((ts, d), F32),
                        pltpu.VMEM((ts, n_states2), F32),
                        pltpu.VMEM((bsz, d), F32),
                        pltpu.VMEM((bsz, n_states2), F32)],
        compiler_params=pltpu.CompilerParams(dimension_semantics=("arbitrary",),
                                             vmem_limit_bytes=VMEM_LIMIT),
        name="mixer",
    )(mod, *consts, *streamed, x3d)


def kernel(x, c, mod_w, mod_b, norm1_g, ffn1_w_up, ffn1_w_down, norm2_g, w_in, b_in, conv_w, conv_b, lru_w_r, lru_b_r, lru_w_i, lru_b_i, lru_lambda, proj_a, s5_a_re, s5_a_im, s5_log_dt, s5_b_re, s5_b_im, s5_c_re, s5_c_im, s5_d, glu_w, glu_b, proj_b, w_out, norm3_g, ffn2_w_up, ffn2_w_down, final_g):
    bsz, seq, d = x.shape
    assert bsz == SUBLANES, "time-major layout puts the batch on the 8 sublanes"
    depth = mod_w.shape[0]
    for l in range(depth):
        mod, lam, bd, cd, wr_bd, wi_bd = _mod_prep(
            c, mod_w[l], mod_b[l], s5_a_re[l], s5_a_im[l], s5_log_dt[l], s5_b_re[l], s5_b_im[l],
            s5_c_re[l], s5_c_im[l], lru_w_r[l], lru_w_i[l])
        last = l == depth - 1
        x2d = _ffn(x.reshape(bsz * seq, d), mod, 0, norm1_g[l], ffn1_w_up[l], ffn1_w_down[l], final_g,
                   seq=seq, final_norm=False)
        x3d = _mixer(x2d.reshape(bsz, seq, d), mod, 1, norm2_g[l], w_in[l], b_in[l], conv_w[l], conv_b[l],
                     wr_bd, lru_b_r[l], wi_bd, lru_b_i[l], lru_lambda[l], proj_a[l],
                     bd, cd, lam, s5_d[l], glu_w[l], glu_b[l], proj_b[l], w_out[l])
        x = _ffn(x3d.reshape(bsz * seq, d), mod, 2, norm3_g[l], ffn2_w_up[l], ffn2_w_down[l], final_g,
                 seq=seq, final_norm=last).reshape(bsz, seq, d)
    return x
```

```python
import functools
import math

import jax
import jax.numpy as jnp
from jax import lax
from jax.experimental import pallas as pl
from jax.experimental.pallas import tpu as pltpu

F32 = jnp.float32
BF16 = jnp.bfloat16

EPS = 1e-6
LRU_C = 8.0
CONV_WIDTH = 4
S5_GROUP = 16
S5_STATE = 64
MOD_PER_SUB = 3

SUBLANES = 8
LANES = 128
GATE_TILE = 256
S5_HALF_CH = 256
VMEM_LIMIT = 60 * 1024 * 1024

WEIGHT_STAGE_SLOTS = 4
FFN_TM = 512
FFN_TILES_PER_STEP = 2
FFN_HEAD_COLS = 256
MIX_TT = 64
S5_LANE_CHUNK = 512


def _sigmoid(x):
    return 0.5 * jnp.tanh(0.5 * x) + 0.5


def _gelu_tanh(x):
    c = math.sqrt(2.0 / math.pi)
    return 0.5 * x * (1.0 + jnp.tanh(c * (x + 0.044715 * (x * x * x))))


def _rms_norm(x, g):
    ms = jnp.mean(x * x, axis=-1, keepdims=True)
    return x * lax.rsqrt(ms + EPS) * g


def _const_spec(shape):
    nd = len(shape)
    return pl.BlockSpec(shape, lambda *_: (0,) * nd, pipeline_mode=pl.Buffered(1))


N_PREP_IN, N_PREP_OUT = 6, 5


def _mod_prep_kernel(c_ref, b_ref, *refs):
    prep_in = refs[:N_PREP_IN]
    w_hbm, o_ref = refs[N_PREP_IN:N_PREP_IN + 2]
    prep_out = refs[N_PREP_IN + 2:N_PREP_IN + 2 + N_PREP_OUT]
    w_ref, sem_ref = refs[N_PREP_IN + 2 + N_PREP_OUT:]
    n_blocks, d, _ = w_ref.shape
    copies = [pltpu.make_async_copy(w_hbm.at[:, pl.ds(k * d, d)], w_ref.at[k], sem_ref.at[k])
              for k in range(n_blocks)]
    for cp in copies:
        cp.start()
    _prep_kernel(*prep_in, *prep_out)

    c = c_ref[...]
    ca = (c * _sigmoid(c)).astype(BF16)
    for k, cp in enumerate(copies):
        cp.wait()
        s, j = divmod(k, MOD_PER_SUB)
        o_ref[s, :, j * d:(j + 1) * d] = (jnp.dot(ca, w_ref[k].astype(BF16), preferred_element_type=F32)
                                          + b_ref[:, k * d:(k + 1) * d])


def _mod_prep(c, mod_w, mod_b, a_re, a_im, log_dt, b_re, b_im, c_re, c_im, w_r, w_i):
    bsz, d = c.shape
    n = mod_w.shape[1]
    n_sub = n // (MOD_PER_SUB * d)
    g, p = a_re.shape
    hch = b_re.shape[2]
    assert hch == S5_GROUP and p == S5_STATE
    width = g * hch
    halves = width // S5_HALF_CH
    half_states = (S5_HALF_CH // hch) * p
    heads, hd, _ = w_r.shape
    a3 = jnp.stack([a_re, a_im, jnp.broadcast_to(log_dt[:, None], (g, p))])
    b2_t = jnp.transpose(jnp.stack([b_re, b_im]), (0, 1, 3, 2)).reshape(2, width, p)
    c2_t = jnp.transpose(jnp.stack([c_re, c_im]), (0, 3, 1, 2)).reshape(2, p, width)
    prep_in = [a3, jnp.repeat(a3, hch, axis=1), b2_t, c2_t, w_r, w_i]
    gate_tiles = jax.ShapeDtypeStruct((heads * hd // GATE_TILE, GATE_TILE, GATE_TILE), BF16)
    prep_out = [jax.ShapeDtypeStruct((2, g, p), F32),
                jax.ShapeDtypeStruct((halves, S5_HALF_CH, 2 * half_states), BF16),
                jax.ShapeDtypeStruct((halves, 2 * half_states, S5_HALF_CH), BF16),
                gate_tiles, gate_tiles]
    assert len(prep_in) == N_PREP_IN and len(prep_out) == N_PREP_OUT
    out_shapes = [jax.ShapeDtypeStruct((n_sub, bsz, MOD_PER_SUB * d), F32)] + prep_out
    mod, lam, bd, cd, wr_bd, wi_bd = pl.pallas_call(
        _mod_prep_kernel,
        grid=(1,),
        in_specs=[_const_spec((bsz, d)), _const_spec((1, n))] + [_const_spec(a.shape) for a in prep_in]
                 + [pl.BlockSpec(memory_space=pl.ANY)],
        out_specs=[_const_spec(a.shape) for a in out_shapes],
        out_shape=out_shapes,
        scratch_shapes=[pltpu.VMEM((n // d, d, d), F32), pltpu.SemaphoreType.DMA((n // d,))],
        compiler_params=pltpu.CompilerParams(dimension_semantics=("arbitrary",),
                                             vmem_limit_bytes=VMEM_LIMIT),
        name="mod_prep",
    )(c, mod_b.reshape(1, n), *prep_in, mod_w)
    return mod, lam.reshape(2, g * p), bd, cd, wr_bd, wi_bd


def _cmul(ar, ai, br, bi):
    return ar * br - ai * bi, ar * bi + ai * br


def _s5_discretise(a_re, a_im, log_dt):
    dt = jnp.exp(log_dt)
    mag = jnp.exp(a_re * dt)
    lr = mag * jnp.cos(a_im * dt)
    li = mag * jnp.sin(a_im * dt)
    den = a_re * a_re + a_im * a_im
    nr = lr - 1.0
    cr = (nr * a_re + li * a_im) / den
    ci = (li * a_re - nr * a_im) / den
    return lr, li, cr, ci


def _prep_kernel(a3_g, a3_r, b2_t, c2_t, wr_ref, wi_ref, lam_ref, bd_ref, cd_ref, or_ref, oi_ref):
    _gate_prep_kernel(wr_ref, wi_ref, or_ref, oi_ref)
    lr, li, _, _ = _s5_discretise(a3_g[0], a3_g[1], a3_g[2])
    lam_ref[0] = lr
    lam_ref[1] = li

    _, _, cr, ci = _s5_discretise(a3_r[0], a3_r[1], a3_r[2])
    bb_re, bb_im = _cmul(cr, ci, b2_t[0], b2_t[1])
    cre_t, cim_t = c2_t.at[0], c2_t.at[1]
    half_states = (S5_HALF_CH // S5_GROUP) * S5_STATE
    erow = lax.broadcasted_iota(jnp.int32, (S5_STATE, half_states), 0)
    ecol = lax.broadcasted_iota(jnp.int32, (S5_STATE, half_states), 1)
    spread = jnp.where((ecol % S5_STATE) == erow, 1.0, 0.0).astype(BF16)
    brow_g = lax.broadcasted_iota(jnp.int32, (S5_HALF_CH, half_states), 0) // S5_GROUP
    bcol_g = lax.broadcasted_iota(jnp.int32, (S5_HALF_CH, half_states), 1) // S5_STATE
    bmask = brow_g == bcol_g
    crow_g = lax.broadcasted_iota(jnp.int32, (half_states, S5_HALF_CH), 0) // S5_STATE
    ccol_g = lax.broadcasted_iota(jnp.int32, (half_states, S5_HALF_CH), 1) // S5_GROUP
    cmask = crow_g == ccol_g
    reps = half_states // S5_STATE
    for h in range(bd_ref.shape[0]):
        rows = slice(h * S5_HALF_CH, (h + 1) * S5_HALF_CH)
        for k, bb in enumerate((bb_re, bb_im)):
            tiled = jnp.dot(bb[rows, :].astype(BF16), spread, preferred_element_type=F32)
            bd_ref[h, :, k * half_states:(k + 1) * half_states] = jnp.where(bmask, tiled, 0.0).astype(BF16)
        for k, (ct, sign) in enumerate(((cre_t, 1.0), (cim_t, -1.0))):
            tiled = jnp.concatenate([ct[:, rows]] * reps, axis=0)
            cd_ref[h, k * half_states:(k + 1) * half_states, :] = jnp.where(cmask, sign * tiled, 0.0).astype(BF16)


def _gate_prep_kernel(wr_ref, wi_ref, or_ref, oi_ref):
    hd = wr_ref.shape[1]
    per_tile = GATE_TILE // hd
    erow = lax.broadcasted_iota(jnp.int32, (hd, GATE_TILE), 0)
    ecol = lax.broadcasted_iota(jnp.int32, (hd, GATE_TILE), 1)
    for w_ref, o_ref in ((wr_ref, or_ref), (wi_ref, oi_ref)):
        for a in range(o_ref.shape[0]):
            for b in range(per_tile):
                place = jnp.where(ecol == erow + b * hd, 1.0, 0.0).astype(BF16)
                o_ref[a, b * hd:(b + 1) * hd, :] = jnp.dot(
                    w_ref[a * per_tile + b].astype(BF16), place, preferred_element_type=F32).astype(BF16)


def _stream_cast(mats, stages, sem_ref):
    chunks = []
    used = [0] * len(stages)
    for src, dst, si in mats:
        rows = stages[si].shape[1]
        assert src.shape[0] % rows == 0 and src.shape[1] <= stages[si].shape[2]
        for r0 in range(0, src.shape[0], rows):
            slot = used[si] % WEIGHT_STAGE_SLOTS
            used[si] += 1
            chunks.append((src, dst, si, r0, rows, slot))

    def copy(k):
        src, _, si, r0, rows, slot = chunks[k]
        return pltpu.make_async_copy(src.at[pl.ds(r0, rows), :],
                                     stages[si].at[slot, :, pl.ds(0, src.shape[1])],
                                     sem_ref.at[si * WEIGHT_STAGE_SLOTS + slot])

    ahead = WEIGHT_STAGE_SLOTS - 1
    for k in range(min(ahead, len(chunks))):
        copy(k).start()
    for k, (src, dst, si, r0, rows, slot) in enumerate(chunks):
        if k + ahead < len(chunks):
            copy(k + ahead).start()
        copy(k).wait()
        dst[r0:r0 + rows, :] = stages[si][slot, :, 0:src.shape[1]].astype(BF16)


def _weight_scratch(weights, stage_shapes):
    return ([pltpu.VMEM(w.shape, BF16) for w in weights]
            + [pltpu.VMEM((WEIGHT_STAGE_SLOTS,) + tuple(s), F32) for s in stage_shapes]
            + [pltpu.SemaphoreType.DMA((WEIGHT_STAGE_SLOTS * len(stage_shapes),))])


def _ffn_kernel(x_ref, xn_ref, mod_ref, g_ref, fg_ref, wup_hbm, wdn_hbm, o_ref,
                wup_ref, wdn_ref, stage_up_ref, stage_dn_ref, sem_ref, u_ref, h0_ref, *,
                steps_per_seq, final_norm):
    d = x_ref.shape[1]
    tm = xn_ref.shape[0]
    n_sub = x_ref.shape[0] // tm
    f = wdn_ref.shape[0]
    c0 = h0_ref.shape[1]
    step = pl.program_id(0)
    batch = step // steps_per_seq
    batch_next = jnp.minimum(step + 1, pl.num_programs(0) - 1) // steps_per_seq

    def mod_row(b, k):
        return mod_ref[pl.ds(b, 1), k * d:(k + 1) * d]

    def modulated(xv, b):
        return (_rms_norm(xv, g_ref[...]) * (1.0 + mod_row(b, 1)) + mod_row(b, 0)).astype(BF16)

    @pl.when(step == 0)
    def _():
        _stream_cast([(wup_hbm, wup_ref, 0), (wdn_hbm, wdn_ref, 1)], [stage_up_ref, stage_dn_ref], sem_ref)
        u0 = modulated(x_ref[0:tm, :], batch)
        u_ref[...] = u0
        h0_ref[...] = jnp.dot(u0, wup_ref[:, :c0], preferred_element_type=F32)

    gt = mod_row(batch, 2)
    for k in range(n_sub):
        rows = slice(k * tm, (k + 1) * tm)
        x = x_ref[rows, :]
        h0 = h0_ref[...]
        h_rest = jnp.dot(u_ref[...], wup_ref[:, c0:], preferred_element_type=F32)
        if k + 1 < n_sub:
            un = modulated(x_ref[(k + 1) * tm:(k + 2) * tm, :], batch)
        else:
            un = modulated(xn_ref[...], batch_next)
        u_ref[...] = un
        h0_ref[...] = jnp.dot(un, wup_ref[:, :c0], preferred_element_type=F32)
        a = jnp.concatenate([h0, h_rest[:, :f - c0]], axis=1)
        b = h_rest[:, f - c0:]
        act = (a * _sigmoid(a)) * b
        y = jnp.dot(act.astype(BF16), wdn_ref[...], preferred_element_type=F32)
        out = x + (0.5 * gt) * y
        if final_norm:
            out = _rms_norm(out, fg_ref[...])
        o_ref[rows, :] = out


def _ffn(x2d, mod, sub, norm_g, w_up, w_down, final_g, *, seq, final_norm):
    t, d = x2d.shape
    f = w_down.shape[0]
    tm = FFN_TM
    rows_per_step = tm * FFN_TILES_PER_STEP
    assert seq % rows_per_step == 0, "a grid step must not straddle two batch elements"
    steps_per_seq = seq // rows_per_step
    n_steps = t // rows_per_step
    n_tiles = t // tm
    nxt = lambda i: jnp.minimum(FFN_TILES_PER_STEP * (i + 1), n_tiles - 1)
    kern = functools.partial(_ffn_kernel, steps_per_seq=steps_per_seq, final_norm=final_norm)
    return pl.pallas_call(
        kern,
        grid=(n_steps,),
        in_specs=[pl.BlockSpec((rows_per_step, d), lambda i: (i, 0)),
                  pl.BlockSpec((tm, d), lambda i: (nxt(i), 0)),
                  pl.BlockSpec((None,) + mod.shape[1:], lambda i: (sub, 0, 0), pipeline_mode=pl.Buffered(1)),
                  _const_spec((1, d)),
                  _const_spec((1, d)),
                  pl.BlockSpec(memory_space=pl.ANY),
                  pl.BlockSpec(memory_space=pl.ANY)],
        out_specs=pl.BlockSpec((rows_per_step, d), lambda i: (i, 0)),
        out_shape=jax.ShapeDtypeStruct((t, d), F32),
        scratch_shapes=_weight_scratch([w_up, w_down], [(d // 16, 2 * f), (f // 8, d)])
                       + [pltpu.VMEM((tm, d), BF16),
                          pltpu.VMEM((tm, FFN_HEAD_COLS), F32)],
        compiler_params=pltpu.CompilerParams(dimension_semantics=("arbitrary",),
                                             vmem_limit_bytes=VMEM_LIMIT),
        name="ffn_final" if final_norm else "ffn",
    )(x2d, x2d, mod, norm_g.reshape(1, d), final_g.reshape(1, d), w_up, w_down)


def _lru_scan(a_ref, u_ref, carry_ref):
    ts = u_ref.shape[0]
    h = carry_ref[...]
    for t in range(ts // SUBLANES):
        sl = slice(t * SUBLANES, (t + 1) * SUBLANES)
        h = a_ref[sl, :] * h + u_ref[sl, :]
        u_ref[sl, :] = h
    carry_ref[...] = h


def _s5_scan(u_ref, lam_ref, carry_ref, halves):
    ts = u_ref.shape[0]
    half_states = lam_ref.shape[1] // halves
    w = S5_LANE_CHUNK
    for h in range(halves):
        for j in range(half_states // w):
            cre = slice(2 * half_states * h + j * w, 2 * half_states * h + (j + 1) * w)
            cim = slice(2 * half_states * h + half_states + j * w, 2 * half_states * h + half_states + (j + 1) * w)
            lcs = slice(half_states * h + j * w, half_states * h + (j + 1) * w)
            lr = jnp.broadcast_to(lam_ref[0:1, lcs], (SUBLANES, w))
            li = jnp.broadcast_to(lam_ref[1:2, lcs], (SUBLANES, w))
            hr = carry_ref[:, cre]
            hi = carry_ref[:, cim]
            for t in range(ts // SUBLANES):
                sl = slice(t * SUBLANES, (t + 1) * SUBLANES)
                hr, hi = (u_ref[sl, cre] + (lr * hr - li * hi),
                          u_ref[sl, cim] + (lr * hi + li * hr))
                u_ref[sl, cre] = hr
                u_ref[sl, cim] = hi
            carry_ref[:, cre] = hr
            carry_ref[:, cim] = hi


def _mixer_kernel(mod_ref, g_ref, bin_ref, cw_ref, cb_ref,
                  wr_ref, br_ref, wi_ref, bi_ref, lamp_ref,
                  bd_ref, cd_ref, lam_ref, sd_ref, gb_ref,
                  win_hbm, pa_hbm, gw_hbm, pb_hbm, wo_hbm, x_hbm,
                  o_hbm,
                  win_ref, pa_ref, gw_ref, pb_ref, wo_ref, stage_in_ref, stage_sq_ref, sem_ref,
                  xt_ref, ot_ref, sem_x_ref, sem_o_ref,
                  xpad_ref, a_ref, u_ref, s5_ref, hl_ref, hs_ref):
    _, tt, nb, d = xt_ref.shape
    ts = nb * tt
    w5 = sd_ref.shape[1]
    halves = bd_ref.shape[0]
    hs2 = bd_ref.shape[2]
    pad = (CONV_WIDTH - 1) * nb
    step = pl.program_id(0)
    n_steps = pl.num_programs(0)
    slot = step % 2

    def x_copies(tile, sl):
        return [pltpu.make_async_copy(x_hbm.at[b, pl.ds(tile * tt, tt), :], xt_ref.at[sl, :, b, :],
                                      sem_x_ref.at[sl]) for b in range(nb)]

    def o_copies(tile, sl):
        return [pltpu.make_async_copy(ot_ref.at[sl, :, b, :], o_hbm.at[b, pl.ds(tile * tt, tt), :],
                                      sem_o_ref.at[sl]) for b in range(nb)]

    @pl.when(step == 0)
    def _():
        for cp in x_copies(0, 0):
            cp.start()
        xpad_ref[0:pad, :] = jnp.zeros((pad, d), F32)
        hl_ref[...] = jnp.zeros_like(hl_ref)
        hs_ref[...] = jnp.zeros_like(hs_ref)
        _stream_cast([(win_hbm, win_ref, 0), (pa_hbm, pa_ref, 1), (gw_hbm, gw_ref, 1), (pb_hbm, pb_ref, 1),
                      (wo_hbm, wo_ref, 1)], [stage_in_ref, stage_sq_ref], sem_ref)

    @pl.when(step + 1 < n_steps)
    def _():
        for cp in x_copies(step + 1, 1 - slot):
            cp.start()

    for cp in x_copies(step, slot):
        cp.wait()

    def per_batch(v):
        return v[None, :, :]

    x3 = xt_ref[slot]
    sh = per_batch(mod_ref[:, 0:d])
    sc = per_batch(mod_ref[:, d:2 * d])
    gt = per_batch(mod_ref[:, 2 * d:3 * d])
    u_bf = (_rms_norm(x3, g_ref[...]) * (1.0 + sc) + sh).reshape(ts, d).astype(BF16)

    def in_proj(lo, hi):
        return jnp.dot(u_bf, win_ref[:, lo:hi], preferred_element_type=F32) + bin_ref[:, lo:hi]

    xpad_ref[pad:pad + ts, :] = in_proj(0, d)
    xb = in_proj(2 * d, 2 * d + w5)
    xb_bf = xb.astype(BF16)
    for h in range(halves):
        s5_ref[:, h * hs2:(h + 1) * hs2] = jnp.dot(
            xb_bf[:, h * S5_HALF_CH:(h + 1) * S5_HALF_CH], bd_ref[h], preferred_element_type=F32)
    ga_pre = in_proj(d, 2 * d)

    xc = cb_ref[...]
    for k in range(CONV_WIDTH):
        xc = xc + xpad_ref[k * nb:k * nb + ts, :] * cw_ref[k:k + 1, :]
    xpad_ref[0:pad, :] = xpad_ref[ts:ts + pad, :]
    xc_bf = xc.astype(BF16)
    gates = []
    for w_ref, b_ref in ((wr_ref, br_ref), (wi_ref, bi_ref)):
        cols = [jnp.dot(xc_bf[:, j * GATE_TILE:(j + 1) * GATE_TILE], w_ref[j], preferred_element_type=F32)
                for j in range(w_ref.shape[0])]
        gates.append(_sigmoid(jnp.concatenate(cols, axis=1) + b_ref[...]))
    r_gate, i_gate = gates
    mga_pre = in_proj(2 * d + w5, 3 * d + w5)
    nl = -lamp_ref[...]
    softplus = jnp.maximum(nl, 0.0) + jnp.log1p(jnp.exp(-jnp.abs(nl)))
    neg_log_a = (LRU_C * softplus) * r_gate
    a_gate = jnp.exp(-neg_log_a)
    a_ref[...] = a_gate
    one_m_a2 = jnp.tanh(neg_log_a) * (1.0 + a_gate * a_gate)
    mult = jnp.where(one_m_a2 > 0.0, one_m_a2 * lax.rsqrt(one_m_a2), 0.0)
    u_ref[...] = mult * (i_gate * xc)
    _lru_scan(a_ref, u_ref, hl_ref)
    ya = u_ref[...] * _gelu_tanh(ga_pre)
    proj_a = jnp.dot(ya.astype(BF16), pa_ref[...], preferred_element_type=F32)
    mgb_pre = in_proj(3 * d + w5, 4 * d + w5)

    _s5_scan(s5_ref, lam_ref, hs_ref, halves)
    ys = [jnp.dot(s5_ref[:, h * hs2:(h + 1) * hs2].astype(BF16), cd_ref[h], preferred_element_type=F32)
          for h in range(halves)]
    yb = _gelu_tanh(jnp.concatenate(ys, axis=1) + sd_ref[...] * xb)
    yb = yb * _sigmoid(jnp.dot(yb.astype(BF16), gw_ref[...], preferred_element_type=F32) + gb_ref[...])
    proj_b = jnp.dot(yb.astype(BF16), pb_ref[...], preferred_element_type=F32)

    m = _sigmoid(mga_pre) * proj_a + _sigmoid(mgb_pre) * proj_b
    y = jnp.dot(m.astype(BF16), wo_ref[...], preferred_element_type=F32)
    out = x3 + gt * y.reshape(tt, nb, d)

    @pl.when(step >= 2)
    def _():
        for cp in o_copies(step - 2, slot):
            cp.wait()

    ot_ref[slot] = out
    for cp in o_copies(step, slot):
        cp.start()

    @pl.when(step == n_steps - 1)
    def _():
        for cp in o_copies(step - 1, 1 - slot):
            cp.wait()
        for cp in o_copies(step, slot):
            cp.wait()


def _mixer(x3d, mod, sub, norm_g, w_in, b_in, conv_w, conv_b, wr_bd, b_r, wi_bd, b_i, lam_p, proj_a,
           bd, cd, lam, s5_d, glu_w, glu_b, proj_b, w_out):
    bsz, seq, d = x3d.shape
    tt = MIX_TT
    ts = tt * bsz
    n_states2 = bd.shape[0] * bd.shape[2]
    row = lambda v: v.reshape(1, -1)
    consts = [row(norm_g), row(b_in), conv_w, row(conv_b), wr_bd, row(b_r), wi_bd, row(b_i),
              row(lam_p), bd, cd, lam, row(s5_d), row(glu_b)]
    streamed = [w_in, proj_a, glu_w, proj_b, w_out]
    assert seq // tt >= 2, "the write-back drain expects at least two sequence tiles"
    return pl.pallas_call(
        _mixer_kernel,
        grid=(seq // tt,),
        in_specs=[pl.BlockSpec((None,) + mod.shape[1:], lambda i: (sub, 0, 0), pipeline_mode=pl.Buffered(1))]
                 + [_const_spec(c.shape) for c in consts]
                 + [pl.BlockSpec(memory_space=pl.ANY)] * (len(streamed) + 1),
        out_specs=pl.BlockSpec(memory_space=pl.ANY),
        out_shape=jax.ShapeDtypeStruct((bsz, seq, d), F32),
        scratch_shapes=_weight_scratch(streamed, [(d // 32, w_in.shape[1]), (d // 8, d)])
                       + [pltpu.VMEM((2, tt, bsz, d), F32),
                          pltpu.VMEM((2, tt, bsz, d), F32),
                          pltpu.SemaphoreType.DMA((2,)),
                          pltpu.SemaphoreType.DMA((2,)),
                        pltpu.VMEM((ts + (CONV_WIDTH - 1) * bsz, d), F32),
                        pltpu.VMEM((ts, d), F32),
                        pltpu.VMEM((ts, d), F32),
                        pltpu.VMEM((ts, n_states2), F32),
                        pltpu.VMEM((bsz, d), F32),
                        pltpu.VMEM((bsz, n_states2), F32)],
        compiler_params=pltpu.CompilerParams(dimension_semantics=("arbitrary",),
                                             vmem_limit_bytes=VMEM_LIMIT),
        name="mixer",
    )(mod, *consts, *streamed, x3d)


def kernel(x, c, mod_w, mod_b, norm1_g, ffn1_w_up, ffn1_w_down, norm2_g, w_in, b_in, conv_w, conv_b, lru_w_r, lru_b_r, lru_w_i, lru_b_i, lru_lambda, proj_a, s5_a_re, s5_a_im, s5_log_dt, s5_b_re, s5_b_im, s5_c_re, s5_c_im, s5_d, glu_w, glu_b, proj_b, w_out, norm3_g, ffn2_w_up, ffn2_w_down, final_g):
    bsz, seq, d = x.shape
    assert bsz == SUBLANES, "time-major layout puts the batch on the 8 sublanes"
    depth = mod_w.shape[0]
    for l in range(depth):
        mod, lam, bd, cd, wr_bd, wi_bd = _mod_prep(
            c, mod_w[l], mod_b[l], s5_a_re[l], s5_a_im[l], s5_log_dt[l], s5_b_re[l], s5_b_im[l],
            s5_c_re[l], s5_c_im[l], lru_w_r[l], lru_w_i[l])
        last = l == depth - 1
        x2d = _ffn(x.reshape(bsz * seq, d), mod, 0, norm1_g[l], ffn1_w_up[l], ffn1_w_down[l], final_g,
                   seq=seq, final_norm=False)
        x3d = _mixer(x2d.reshape(bsz, seq, d), mod, 1, norm2_g[l], w_in[l], b_in[l], conv_w[l], conv_b[l],
                     wr_bd, lru_b_r[l], wi_bd, lru_b_i[l], lru_lambda[l], proj_a[l],
                     bd, cd, lam, s5_d[l], glu_w[l], glu_b[l], proj_b[l], w_out[l])
        x = _ffn(x3d.reshape(bsz * seq, d), mod, 2, norm3_g[l], ffn2_w_up[l], ffn2_w_down[l], final_g,
                 seq=seq, final_norm=last).reshape(bsz, seq, d)
    return x
```

```python
import functools
import math

import jax
import jax.numpy as jnp
from jax import lax
from jax.experimental import pallas as pl
from jax.experimental.pallas import tpu as pltpu

F32 = jnp.float32
BF16 = jnp.bfloat16

EPS = 1e-6
LRU_C = 8.0
CONV_WIDTH = 4
S5_GROUP = 16
S5_STATE = 64
MOD_PER_SUB = 3

SUBLANES = 8
LANES = 128
GATE_TILE = 256
S5_HALF_CH = 256
VMEM_LIMIT = 60 * 1024 * 1024

WEIGHT_STAGE_SLOTS = 4
FFN_TM = 512
FFN_TILES_PER_STEP = 2
FFN_HEAD_COLS = 256
MIX_TT = 64
MIX_PARTS = 2
S5_LANE_CHUNK = 512


def _sigmoid(x):
    return 0.5 * jnp.tanh(0.5 * x) + 0.5


def _gelu_tanh(x):
    c = math.sqrt(2.0 / math.pi)
    return 0.5 * x * (1.0 + jnp.tanh(c * (x + 0.044715 * (x * x * x))))


def _rms_norm(x, g):
    ms = jnp.mean(x * x, axis=-1, keepdims=True)
    return x * lax.rsqrt(ms + EPS) * g


def _const_spec(shape):
    nd = len(shape)
    return pl.BlockSpec(shape, lambda *_: (0,) * nd, pipeline_mode=pl.Buffered(1))


N_PREP_IN, N_PREP_OUT = 6, 5


def _mod_prep_kernel(c_ref, b_ref, *refs):
    prep_in = refs[:N_PREP_IN]
    w_hbm, o_ref = refs[N_PREP_IN:N_PREP_IN + 2]
    prep_out = refs[N_PREP_IN + 2:N_PREP_IN + 2 + N_PREP_OUT]
    w_ref, sem_ref = refs[N_PREP_IN + 2 + N_PREP_OUT:]
    n_blocks, d, _ = w_ref.shape
    copies = [pltpu.make_async_copy(w_hbm.at[:, pl.ds(k * d, d)], w_ref.at[k], sem_ref.at[k])
              for k in range(n_blocks)]
    for cp in copies:
        cp.start()
    _prep_kernel(*prep_in, *prep_out)

    c = c_ref[...]
    ca = (c * _sigmoid(c)).astype(BF16)
    for k, cp in enumerate(copies):
        cp.wait()
        s, j = divmod(k, MOD_PER_SUB)
        o_ref[s, :, j * d:(j + 1) * d] = (jnp.dot(ca, w_ref[k].astype(BF16), preferred_element_type=F32)
                                          + b_ref[:, k * d:(k + 1) * d])


def _mod_prep(c, mod_w, mod_b, a_re, a_im, log_dt, b_re, b_im, c_re, c_im, w_r, w_i):
    bsz, d = c.shape
    n = mod_w.shape[1]
    n_sub = n // (MOD_PER_SUB * d)
    g, p = a_re.shape
    hch = b_re.shape[2]
    assert hch == S5_GROUP and p == S5_STATE
    width = g * hch
    halves = width // S5_HALF_CH
    half_states = (S5_HALF_CH // hch) * p
    heads, hd, _ = w_r.shape
    a3 = jnp.stack([a_re, a_im, jnp.broadcast_to(log_dt[:, None], (g, p))])
    b2_t = jnp.transpose(jnp.stack([b_re, b_im]), (0, 1, 3, 2)).reshape(2, width, p)
    c2_t = jnp.transpose(jnp.stack([c_re, c_im]), (0, 3, 1, 2)).reshape(2, p, width)
    prep_in = [a3, jnp.repeat(a3, hch, axis=1), b2_t, c2_t, w_r, w_i]
    gate_tiles = jax.ShapeDtypeStruct((heads * hd // GATE_TILE, GATE_TILE, GATE_TILE), BF16)
    prep_out = [jax.ShapeDtypeStruct((2, g, p), F32),
                jax.ShapeDtypeStruct((halves, S5_HALF_CH, 2 * half_states), BF16),
                jax.ShapeDtypeStruct((halves, 2 * half_states, S5_HALF_CH), BF16),
                gate_tiles, gate_tiles]
    assert len(prep_in) == N_PREP_IN and len(prep_out) == N_PREP_OUT
    out_shapes = [jax.ShapeDtypeStruct((n_sub, bsz, MOD_PER_SUB * d), F32)] + prep_out
    mod, lam, bd, cd, wr_bd, wi_bd = pl.pallas_call(
        _mod_prep_kernel,
        grid=(1,),
        in_specs=[_const_spec((bsz, d)), _const_spec((1, n))] + [_const_spec(a.shape) for a in prep_in]
                 + [pl.BlockSpec(memory_space=pl.ANY)],
        out_specs=[_const_spec(a.shape) for a in out_shapes],
        out_shape=out_shapes,
        scratch_shapes=[pltpu.VMEM((n // d, d, d), F32), pltpu.SemaphoreType.DMA((n // d,))],
        compiler_params=pltpu.CompilerParams(dimension_semantics=("arbitrary",),
                                             vmem_limit_bytes=VMEM_LIMIT),
        name="mod_prep",
    )(c, mod_b.reshape(1, n), *prep_in, mod_w)
    return mod, lam.reshape(2, g * p), bd, cd, wr_bd, wi_bd


def _cmul(ar, ai, br, bi):
    return ar * br - ai * bi, ar * bi + ai * br


def _s5_discretise(a_re, a_im, log_dt):
    dt = jnp.exp(log_dt)
    mag = jnp.exp(a_re * dt)
    lr = mag * jnp.cos(a_im * dt)
    li = mag * jnp.sin(a_im * dt)
    den = a_re * a_re + a_im * a_im
    nr = lr - 1.0
    cr = (nr * a_re + li * a_im) / den
    ci = (li * a_re - nr * a_im) / den
    return lr, li, cr, ci


def _prep_kernel(a3_g, a3_r, b2_t, c2_t, wr_ref, wi_ref, lam_ref, bd_ref, cd_ref, or_ref, oi_ref):
    _gate_prep_kernel(wr_ref, wi_ref, or_ref, oi_ref)
    lr, li, _, _ = _s5_discretise(a3_g[0], a3_g[1], a3_g[2])
    lam_ref[0] = lr
    lam_ref[1] = li

    _, _, cr, ci = _s5_discretise(a3_r[0], a3_r[1], a3_r[2])
    bb_re, bb_im = _cmul(cr, ci, b2_t[0], b2_t[1])
    cre_t, cim_t = c2_t.at[0], c2_t.at[1]
    half_states = (S5_HALF_CH // S5_GROUP) * S5_STATE
    erow = lax.broadcasted_iota(jnp.int32, (S5_STATE, half_states), 0)
    ecol = lax.broadcasted_iota(jnp.int32, (S5_STATE, half_states), 1)
    spread = jnp.where((ecol % S5_STATE) == erow, 1.0, 0.0).astype(BF16)
    brow_g = lax.broadcasted_iota(jnp.int32, (S5_HALF_CH, half_states), 0) // S5_GROUP
    bcol_g = lax.broadcasted_iota(jnp.int32, (S5_HALF_CH, half_states), 1) // S5_STATE
    bmask = brow_g == bcol_g
    crow_g = lax.broadcasted_iota(jnp.int32, (half_states, S5_HALF_CH), 0) // S5_STATE
    ccol_g = lax.broadcasted_iota(jnp.int32, (half_states, S5_HALF_CH), 1) // S5_GROUP
    cmask = crow_g == ccol_g
    reps = half_states // S5_STATE
    for h in range(bd_ref.shape[0]):
        rows = slice(h * S5_HALF_CH, (h + 1) * S5_HALF_CH)
        for k, bb in enumerate((bb_re, bb_im)):
            tiled = jnp.dot(bb[rows, :].astype(BF16), spread, preferred_element_type=F32)
            bd_ref[h, :, k * half_states:(k + 1) * half_states] = jnp.where(bmask, tiled, 0.0).astype(BF16)
        for k, (ct, sign) in enumerate(((cre_t, 1.0), (cim_t, -1.0))):
            tiled = jnp.concatenate([ct[:, rows]] * reps, axis=0)
            cd_ref[h, k * half_states:(k + 1) * half_states, :] = jnp.where(cmask, sign * tiled, 0.0).astype(BF16)


def _gate_prep_kernel(wr_ref, wi_ref, or_ref, oi_ref):
    hd = wr_ref.shape[1]
    per_tile = GATE_TILE // hd
    erow = lax.broadcasted_iota(jnp.int32, (hd, GATE_TILE), 0)
    ecol = lax.broadcasted_iota(jnp.int32, (hd, GATE_TILE), 1)
    for w_ref, o_ref in ((wr_ref, or_ref), (wi_ref, oi_ref)):
        for a in range(o_ref.shape[0]):
            for b in range(per_tile):
                place = jnp.where(ecol == erow + b * hd, 1.0, 0.0).astype(BF16)
                o_ref[a, b * hd:(b + 1) * hd, :] = jnp.dot(
                    w_ref[a * per_tile + b].astype(BF16), place, preferred_element_type=F32).astype(BF16)


def _stream_cast(mats, stages, sem_ref):
    chunks = []
    used = [0] * len(stages)
    for src, dst, si in mats:
        rows = stages[si].shape[1]
        assert src.shape[0] % rows == 0 and src.shape[1] <= stages[si].shape[2]
        for r0 in range(0, src.shape[0], rows):
            slot = used[si] % WEIGHT_STAGE_SLOTS
            used[si] += 1
            chunks.append((src, dst, si, r0, rows, slot))

    def copy(k):
        src, _, si, r0, rows, slot = chunks[k]
        return pltpu.make_async_copy(src.at[pl.ds(r0, rows), :],
                                     stages[si].at[slot, :, pl.ds(0, src.shape[1])],
                                     sem_ref.at[si * WEIGHT_STAGE_SLOTS + slot])

    ahead = WEIGHT_STAGE_SLOTS - 1
    for k in range(min(ahead, len(chunks))):
        copy(k).start()
    for k, (src, dst, si, r0, rows, slot) in enumerate(chunks):
        if k + ahead < len(chunks):
            copy(k + ahead).start()
        copy(k).wait()
        dst[r0:r0 + rows, :] = stages[si][slot, :, 0:src.shape[1]].astype(BF16)


def _weight_scratch(weights, stage_shapes):
    return ([pltpu.VMEM(w.shape, BF16) for w in weights]
            + [pltpu.VMEM((WEIGHT_STAGE_SLOTS,) + tuple(s), F32) for s in stage_shapes]
            + [pltpu.SemaphoreType.DMA((WEIGHT_STAGE_SLOTS * len(stage_shapes),))])


def _ffn_kernel(x_ref, xn_ref, mod_ref, g_ref, fg_ref, wup_hbm, wdn_hbm, o_ref,
                wup_ref, wdn_ref, stage_up_ref, stage_dn_ref, sem_ref, u_ref, h0_ref, *,
                steps_per_seq, final_norm):
    d = x_ref.shape[1]
    tm = xn_ref.shape[0]
    n_sub = x_ref.shape[0] // tm
    f = wdn_ref.shape[0]
    c0 = h0_ref.shape[1]
    step = pl.program_id(0)
    batch = step // steps_per_seq
    batch_next = jnp.minimum(step + 1, pl.num_programs(0) - 1) // steps_per_seq

    def mod_row(b, k):
        return mod_ref[pl.ds(b, 1), k * d:(k + 1) * d]

    def modulated(xv, b):
        return (_rms_norm(xv, g_ref[...]) * (1.0 + mod_row(b, 1)) + mod_row(b, 0)).astype(BF16)

    @pl.when(step == 0)
    def _():
        _stream_cast([(wup_hbm, wup_ref, 0), (wdn_hbm, wdn_ref, 1)], [stage_up_ref, stage_dn_ref], sem_ref)
        u0 = modulated(x_ref[0:tm, :], batch)
        u_ref[...] = u0
        h0_ref[...] = jnp.dot(u0, wup_ref[:, :c0], preferred_element_type=F32)

    gt = mod_row(batch, 2)
    for k in range(n_sub):
        rows = slice(k * tm, (k + 1) * tm)
        x = x_ref[rows, :]
        h0 = h0_ref[...]
        h_rest = jnp.dot(u_ref[...], wup_ref[:, c0:], preferred_element_type=F32)
        if k + 1 < n_sub:
            un = modulated(x_ref[(k + 1) * tm:(k + 2) * tm, :], batch)
        else:
            un = modulated(xn_ref[...], batch_next)
        u_ref[...] = un
        h0_ref[...] = jnp.dot(un, wup_ref[:, :c0], preferred_element_type=F32)
        a = jnp.concatenate([h0, h_rest[:, :f - c0]], axis=1)
        b = h_rest[:, f - c0:]
        act = (a * _sigmoid(a)) * b
        y = jnp.dot(act.astype(BF16), wdn_ref[...], preferred_element_type=F32)
        out = x + (0.5 * gt) * y
        if final_norm:
            out = _rms_norm(out, fg_ref[...])
        o_ref[rows, :] = out


def _ffn(x2d, mod, sub, norm_g, w_up, w_down, final_g, *, seq, final_norm):
    t, d = x2d.shape
    f = w_down.shape[0]
    tm = FFN_TM
    rows_per_step = tm * FFN_TILES_PER_STEP
    assert seq % rows_per_step == 0, "a grid step must not straddle two batch elements"
    steps_per_seq = seq // rows_per_step
    n_steps = t // rows_per_step
    n_tiles = t // tm
    nxt = lambda i: jnp.minimum(FFN_TILES_PER_STEP * (i + 1), n_tiles - 1)
    kern = functools.partial(_ffn_kernel, steps_per_seq=steps_per_seq, final_norm=final_norm)
    return pl.pallas_call(
        kern,
        grid=(n_steps,),
        in_specs=[pl.BlockSpec((rows_per_step, d), lambda i: (i, 0)),
                  pl.BlockSpec((tm, d), lambda i: (nxt(i), 0)),
                  pl.BlockSpec((None,) + mod.shape[1:], lambda i: (sub, 0, 0), pipeline_mode=pl.Buffered(1)),
                  _const_spec((1, d)),
                  _const_spec((1, d)),
                  pl.BlockSpec(memory_space=pl.ANY),
                  pl.BlockSpec(memory_space=pl.ANY)],
        out_specs=pl.BlockSpec((rows_per_step, d), lambda i: (i, 0)),
        out_shape=jax.ShapeDtypeStruct((t, d), F32),
        scratch_shapes=_weight_scratch([w_up, w_down], [(d // 16, 2 * f), (f // 8, d)])
                       + [pltpu.VMEM((tm, d), BF16),
                          pltpu.VMEM((tm, FFN_HEAD_COLS), F32)],
        compiler_params=pltpu.CompilerParams(dimension_semantics=("arbitrary",),
                                             vmem_limit_bytes=VMEM_LIMIT),
        name="ffn_final" if final_norm else "ffn",
    )(x2d, x2d, mod, norm_g.reshape(1, d), final_g.reshape(1, d), w_up, w_down)


def _lru_scan(a_ref, u_ref, carry_ref):
    ts = u_ref.shape[0]
    h = carry_ref[...]
    for t in range(ts // SUBLANES):
        sl = slice(t * SUBLANES, (t + 1) * SUBLANES)
        h = a_ref[sl, :] * h + u_ref[sl, :]
        u_ref[sl, :] = h
    carry_ref[...] = h


def _s5_scan(u_ref, lam_ref, carry_ref, halves):
    ts = u_ref.shape[0]
    half_states = lam_ref.shape[1] // halves
    w = S5_LANE_CHUNK
    for h in range(halves):
        for j in range(half_states // w):
            cre = slice(2 * half_states * h + j * w, 2 * half_states * h + (j + 1) * w)
            cim = slice(2 * half_states * h + half_states + j * w, 2 * half_states * h + half_states + (j + 1) * w)
            lcs = slice(half_states * h + j * w, half_states * h + (j + 1) * w)
            lr = jnp.broadcast_to(lam_ref[0:1, lcs], (SUBLANES, w))
            li = jnp.broadcast_to(lam_ref[1:2, lcs], (SUBLANES, w))
            hr = carry_ref[:, cre]
            hi = carry_ref[:, cim]
            for t in range(ts // SUBLANES):
                sl = slice(t * SUBLANES, (t + 1) * SUBLANES)
                hr, hi = (u_ref[sl, cre] + (lr * hr - li * hi),
                          u_ref[sl, cim] + (lr * hi + li * hr))
                u_ref[sl, cre] = hr
                u_ref[sl, cim] = hi
            carry_ref[:, cre] = hr
            carry_ref[:, cim] = hi


def _mixer_kernel(mod_ref, g_ref, bin_ref, cw_ref, cb_ref,
                  wr_ref, br_ref, wi_ref, bi_ref, lamp_ref,
                  bd_ref, cd_ref, lam_ref, sd_ref, gb_ref,
                  win_hbm, pa_hbm, gw_hbm, pb_hbm, wo_hbm, x_hbm,
                  o_hbm,
                  win_ref, pa_ref, gw_ref, pb_ref, wo_ref, stage_in_ref, stage_sq_ref, sem_ref,
                  xt_ref, ot_ref, sem_x_ref, sem_o_ref,
                  xpad_ref, a_ref, u_ref, s5_ref, hl_ref, hs_ref):
    _, tt, nb, d = xt_ref.shape
    ts = nb * tt
    w5 = sd_ref.shape[1]
    halves = bd_ref.shape[0]
    hs2 = bd_ref.shape[2]
    pad = (CONV_WIDTH - 1) * nb
    step = pl.program_id(0)
    n_steps = pl.num_programs(0)
    slot = step % 2

    def x_copies(tile, sl):
        return [pltpu.make_async_copy(x_hbm.at[b, pl.ds(tile * tt, tt), :], xt_ref.at[sl, :, b, :],
                                      sem_x_ref.at[sl]) for b in range(nb)]

    def o_copies(tile, sl):
        return [pltpu.make_async_copy(ot_ref.at[sl, :, b, :], o_hbm.at[b, pl.ds(tile * tt, tt), :],
                                      sem_o_ref.at[sl]) for b in range(nb)]

    @pl.when(step == 0)
    def _():
        for cp in x_copies(0, 0):
            cp.start()
        xpad_ref[0:pad, :] = jnp.zeros((pad, d), F32)
        hl_ref[...] = jnp.zeros_like(hl_ref)
        hs_ref[...] = jnp.zeros_like(hs_ref)
        _stream_cast([(win_hbm, win_ref, 0), (pa_hbm, pa_ref, 1), (gw_hbm, gw_ref, 1), (pb_hbm, pb_ref, 1),
                      (wo_hbm, wo_ref, 1)], [stage_in_ref, stage_sq_ref], sem_ref)

    @pl.when(step + 1 < n_steps)
    def _():
        for cp in x_copies(step + 1, 1 - slot):
            cp.start()

    for cp in x_copies(step, slot):
        cp.wait()

    def per_batch(v):
        return v[None, :, :]

    @pl.when(step >= 2)
    def _():
        for cp in o_copies(step - 2, slot):
            cp.wait()

    sh = per_batch(mod_ref[:, 0:d])
    sc = per_batch(mod_ref[:, d:2 * d])
    gt = per_batch(mod_ref[:, 2 * d:3 * d])
    nl = -lamp_ref[...]
    softplus = jnp.maximum(nl, 0.0) + jnp.log1p(jnp.exp(-jnp.abs(nl)))

    tp = tt // MIX_PARTS
    rp = tp * nb

    def front(p):
        r0 = p * rp
        x3 = xt_ref[slot, p * tp:(p + 1) * tp]
        u_bf = (_rms_norm(x3, g_ref[...]) * (1.0 + sc) + sh).reshape(rp, d).astype(BF16)

        def in_proj(lo, hi):
            return jnp.dot(u_bf, win_ref[:, lo:hi], preferred_element_type=F32) + bin_ref[:, lo:hi]

        xpad_ref[pad + r0:pad + r0 + rp, :] = in_proj(0, d)
        xb = in_proj(2 * d, 2 * d + w5)
        xb_bf = xb.astype(BF16)
        for h in range(halves):
            s5_ref[r0:r0 + rp, h * hs2:(h + 1) * hs2] = jnp.dot(
                xb_bf[:, h * S5_HALF_CH:(h + 1) * S5_HALF_CH], bd_ref[h], preferred_element_type=F32)
        ga_pre = in_proj(d, 2 * d)

        xc = cb_ref[...]
        for k in range(CONV_WIDTH):
            xc = xc + xpad_ref[r0 + k * nb:r0 + k * nb + rp, :] * cw_ref[k:k + 1, :]
        xc_bf = xc.astype(BF16)
        gates = []
        for w_ref, b_ref in ((wr_ref, br_ref), (wi_ref, bi_ref)):
            cols = [jnp.dot(xc_bf[:, j * GATE_TILE:(j + 1) * GATE_TILE], w_ref[j], preferred_element_type=F32)
                    for j in range(w_ref.shape[0])]
            gates.append(_sigmoid(jnp.concatenate(cols, axis=1) + b_ref[...]))
        mga_pre = in_proj(2 * d + w5, 3 * d + w5)
        mgb_pre = in_proj(3 * d + w5, 4 * d + w5)
        return x3, xb, ga_pre, xc, gates, mga_pre, mgb_pre

    def back(p, x3, xb, ga_pre, xc, gates, mga_pre, mgb_pre):
        r0 = p * rp
        rows = pl.ds(r0, rp)
        r_gate, i_gate = gates
        neg_log_a = (LRU_C * softplus) * r_gate
        a_gate = jnp.exp(-neg_log_a)
        a_ref[rows, :] = a_gate
        one_m_a2 = jnp.tanh(neg_log_a) * (1.0 + a_gate * a_gate)
        mult = jnp.where(one_m_a2 > 0.0, one_m_a2 * lax.rsqrt(one_m_a2), 0.0)
        u_ref[rows, :] = mult * (i_gate * xc)
        _lru_scan(a_ref.at[rows], u_ref.at[rows], hl_ref)
        ya = u_ref[rows, :] * _gelu_tanh(ga_pre)
        proj_a = jnp.dot(ya.astype(BF16), pa_ref[...], preferred_element_type=F32)

        _s5_scan(s5_ref.at[rows], lam_ref, hs_ref, halves)
        ys = [jnp.dot(s5_ref[rows, h * hs2:(h + 1) * hs2].astype(BF16), cd_ref[h], preferred_element_type=F32)
              for h in range(halves)]
        yb = _gelu_tanh(jnp.concatenate(ys, axis=1) + sd_ref[...] * xb)
        yb = yb * _sigmoid(jnp.dot(yb.astype(BF16), gw_ref[...], preferred_element_type=F32) + gb_ref[...])
        proj_b = jnp.dot(yb.astype(BF16), pb_ref[...], preferred_element_type=F32)

        m = _sigmoid(mga_pre) * proj_a + _sigmoid(mgb_pre) * proj_b
        y = jnp.dot(m.astype(BF16), wo_ref[...], preferred_element_type=F32)
        ot_ref[slot, p * tp:(p + 1) * tp] = x3 + gt * y.reshape(tp, nb, d)

    fronts = [front(p) for p in range(MIX_PARTS)]
    xpad_ref[0:pad, :] = xpad_ref[ts:ts + pad, :]
    for p in range(MIX_PARTS):
        back(p, *fronts[p])

    for cp in o_copies(step, slot):
        cp.start()

    @pl.when(step == n_steps - 1)
    def _():
        for cp in o_copies(step - 1, 1 - slot):
            cp.wait()
        for cp in o_copies(step, slot):
            cp.wait()


def _mixer(x3d, mod, sub, norm_g, w_in, b_in, conv_w, conv_b, wr_bd, b_r, wi_bd, b_i, lam_p, proj_a,
           bd, cd, lam, s5_d, glu_w, glu_b, proj_b, w_out):
    bsz, seq, d = x3d.shape
    tt = MIX_TT
    ts = tt * bsz
    n_states2 = bd.shape[0] * bd.shape[2]
    row = lambda v: v.reshape(1, -1)
    consts = [row(norm_g), row(b_in), conv_w, row(conv_b), wr_bd, row(b_r), wi_bd, row(b_i),
              row(lam_p), bd, cd, lam, row(s5_d), row(glu_b)]
    streamed = [w_in, proj_a, glu_w, proj_b, w_out]
    assert seq // tt >= 2, "the write-back drain expects at least two sequence tiles"
    return pl.pallas_call(
        _mixer_kernel,
        grid=(seq // tt,),
        in_specs=[pl.BlockSpec((None,) + mod.shape[1:], lambda i: (sub, 0, 0), pipeline_mode=pl.Buffered(1))]
                 + [_const_spec(c.shape) for c in consts]
                 + [pl.BlockSpec(memory_space=pl.ANY)] * (len(streamed) + 1),
        out_specs=pl.BlockSpec(memory_space=pl.ANY),
        out_shape=jax.ShapeDtypeStruct((bsz, seq, d), F32),
        scratch_shapes=_weight_scratch(streamed, [(d // 32, w_in.shape[1]), (d // 8, d)])
                       + [pltpu.VMEM((2, tt, bsz, d), F32),
                          pltpu.VMEM((2, tt, bsz, d), F32),
                          pltpu.SemaphoreType.DMA((2,)),
                          pltpu.SemaphoreType.DMA((2,)),
                        pltpu.VMEM((ts + (CONV_WIDTH - 1) * bsz, d), F32),
                        pltpu.VMEM((ts, d), F32),
                        pltpu.VMEM((ts, d), F32),
                        pltpu.VMEM((ts, n_states2), F32),
                        pltpu.VMEM((bsz, d), F32),
                        pltpu.VMEM((bsz, n_states2), F32)],
        compiler_params=pltpu.CompilerParams(dimension_semantics=("arbitrary",),
                                             vmem_limit_bytes=VMEM_LIMIT),
        name="mixer",
    )(mod, *consts, *streamed, x3d)


def kernel(x, c, mod_w, mod_b, norm1_g, ffn1_w_up, ffn1_w_down, norm2_g, w_in, b_in, conv_w, conv_b, lru_w_r, lru_b_r, lru_w_i, lru_b_i, lru_lambda, proj_a, s5_a_re, s5_a_im, s5_log_dt, s5_b_re, s5_b_im, s5_c_re, s5_c_im, s5_d, glu_w, glu_b, proj_b, w_out, norm3_g, ffn2_w_up, ffn2_w_down, final_g):
    bsz, seq, d = x.shape
    assert bsz == SUBLANES, "time-major layout puts the batch on the 8 sublanes"
    depth = mod_w.shape[0]
    for l in range(depth):
        mod, lam, bd, cd, wr_bd, wi_bd = _mod_prep(
            c, mod_w[l], mod_b[l], s5_a_re[l], s5_a_im[l], s5_log_dt[l], s5_b_re[l], s5_b_im[l],
            s5_c_re[l], s5_c_im[l], lru_w_r[l], lru_w_i[l])
        last = l == depth - 1
        x2d = _ffn(x.reshape(bsz * seq, d), mod, 0, norm1_g[l], ffn1_w_up[l], ffn1_w_down[l], final_g,
                   seq=seq, final_norm=False)
        x3d = _mixer(x2d.reshape(bsz, seq, d), mod, 1, norm2_g[l], w_in[l], b_in[l], conv_w[l], conv_b[l],
                     wr_bd, lru_b_r[l], wi_bd, lru_b_i[l], lru_lambda[l], proj_a[l],
                     bd, cd, lam, s5_d[l], glu_w[l], glu_b[l], proj_b[l], w_out[l])
        x = _ffn(x3d.reshape(bsz * seq, d), mod, 2, norm3_g[l], ffn2_w_up[l], ffn2_w_down[l], final_g,
                 seq=seq, final_norm=last).reshape(bsz, seq, d)
    return x
```

```python
import functools
import math

import jax
import jax.numpy as jnp
from jax import lax
from jax.experimental import pallas as pl
from jax.experimental.pallas import tpu as pltpu

F32 = jnp.float32
BF16 = jnp.bfloat16

EPS = 1e-6
LRU_C = 8.0
CONV_WIDTH = 4
S5_GROUP = 16
S5_STATE = 64
MOD_PER_SUB = 3

SUBLANES = 8
LANES = 128
GATE_TILE = 256
S5_HALF_CH = 256
VMEM_LIMIT = 60 * 1024 * 1024

WEIGHT_STAGE_SLOTS = 4
FFN_TM = 512
FFN_TILES_PER_STEP = 2
FFN_HEAD_COLS = 256
MIX_TT = 64
S5_LANE_CHUNK = 512


def _sigmoid(x):
    return 0.5 * jnp.tanh(0.5 * x) + 0.5


def _gelu_tanh(x):
    c = math.sqrt(2.0 / math.pi)
    return 0.5 * x * (1.0 + jnp.tanh(c * (x + 0.044715 * (x * x * x))))


def _rms_norm(x, g):
    ms = jnp.mean(x * x, axis=-1, keepdims=True)
    return x * lax.rsqrt(ms + EPS) * g


def _const_spec(shape):
    nd = len(shape)
    return pl.BlockSpec(shape, lambda *_: (0,) * nd, pipeline_mode=pl.Buffered(1))


N_PREP_IN, N_PREP_OUT = 6, 5


def _mod_prep_kernel(c_ref, b_ref, *refs):
    prep_in = refs[:N_PREP_IN]
    w_hbm, o_ref = refs[N_PREP_IN:N_PREP_IN + 2]
    prep_out = refs[N_PREP_IN + 2:N_PREP_IN + 2 + N_PREP_OUT]
    w_ref, sem_ref = refs[N_PREP_IN + 2 + N_PREP_OUT:]
    n_blocks, d, _ = w_ref.shape
    copies = [pltpu.make_async_copy(w_hbm.at[:, pl.ds(k * d, d)], w_ref.at[k], sem_ref.at[k])
              for k in range(n_blocks)]
    for cp in copies:
        cp.start()
    _prep_kernel(*prep_in, *prep_out)

    c = c_ref[...]
    ca = (c * _sigmoid(c)).astype(BF16)
    for k, cp in enumerate(copies):
        cp.wait()
        s, j = divmod(k, MOD_PER_SUB)
        o_ref[s, :, j * d:(j + 1) * d] = (jnp.dot(ca, w_ref[k].astype(BF16), preferred_element_type=F32)
                                          + b_ref[:, k * d:(k + 1) * d])


def _mod_prep(c, mod_w, mod_b, a_re, a_im, log_dt, b_re, b_im, c_re, c_im, w_r, w_i):
    bsz, d = c.shape
    n = mod_w.shape[1]
    n_sub = n // (MOD_PER_SUB * d)
    g, p = a_re.shape
    hch = b_re.shape[2]
    assert hch == S5_GROUP and p == S5_STATE
    width = g * hch
    halves = width // S5_HALF_CH
    half_states = (S5_HALF_CH // hch) * p
    heads, hd, _ = w_r.shape
    a3 = jnp.stack([a_re, a_im, jnp.broadcast_to(log_dt[:, None], (g, p))])
    b2_t = jnp.transpose(jnp.stack([b_re, b_im]), (0, 1, 3, 2)).reshape(2, width, p)
    c2_t = jnp.transpose(jnp.stack([c_re, c_im]), (0, 3, 1, 2)).reshape(2, p, width)
    prep_in = [a3, jnp.repeat(a3, hch, axis=1), b2_t, c2_t, w_r, w_i]
    gate_tiles = jax.ShapeDtypeStruct((heads * hd // GATE_TILE, GATE_TILE, GATE_TILE), BF16)
    prep_out = [jax.ShapeDtypeStruct((2, g, p), F32),
                jax.ShapeDtypeStruct((halves, S5_HALF_CH, 2 * half_states), BF16),
                jax.ShapeDtypeStruct((halves, 2 * half_states, S5_HALF_CH), BF16),
                gate_tiles, gate_tiles]
    assert len(prep_in) == N_PREP_IN and len(prep_out) == N_PREP_OUT
    out_shapes = [jax.ShapeDtypeStruct((n_sub, bsz, MOD_PER_SUB * d), F32)] + prep_out
    mod, lam, bd, cd, wr_bd, wi_bd = pl.pallas_call(
        _mod_prep_kernel,
        grid=(1,),
        in_specs=[_const_spec((bsz, d)), _const_spec((1, n))] + [_const_spec(a.shape) for a in prep_in]
                 + [pl.BlockSpec(memory_space=pl.ANY)],
        out_specs=[_const_spec(a.shape) for a in out_shapes],
        out_shape=out_shapes,
        scratch_shapes=[pltpu.VMEM((n // d, d, d), F32), pltpu.SemaphoreType.DMA((n // d,))],
        compiler_params=pltpu.CompilerParams(dimension_semantics=("arbitrary",),
                                             vmem_limit_bytes=VMEM_LIMIT),
        name="mod_prep",
    )(c, mod_b.reshape(1, n), *prep_in, mod_w)
    return mod, lam.reshape(2, g * p), bd, cd, wr_bd, wi_bd


def _cmul(ar, ai, br, bi):
    return ar * br - ai * bi, ar * bi + ai * br


def _s5_discretise(a_re, a_im, log_dt):
    dt = jnp.exp(log_dt)
    mag = jnp.exp(a_re * dt)
    lr = mag * jnp.cos(a_im * dt)
    li = mag * jnp.sin(a_im * dt)
    den = a_re * a_re + a_im * a_im
    nr = lr - 1.0
    cr = (nr * a_re + li * a_im) / den
    ci = (li * a_re - nr * a_im) / den
    return lr, li, cr, ci


def _prep_kernel(a3_g, a3_r, b2_t, c2_t, wr_ref, wi_ref, lam_ref, bd_ref, cd_ref, or_ref, oi_ref):
    _gate_prep_kernel(wr_ref, wi_ref, or_ref, oi_ref)
    lr, li, _, _ = _s5_discretise(a3_g[0], a3_g[1], a3_g[2])
    lam_ref[0] = lr
    lam_ref[1] = li

    _, _, cr, ci = _s5_discretise(a3_r[0], a3_r[1], a3_r[2])
    bb_re, bb_im = _cmul(cr, ci, b2_t[0], b2_t[1])
    cre_t, cim_t = c2_t.at[0], c2_t.at[1]
    half_states = (S5_HALF_CH // S5_GROUP) * S5_STATE
    erow = lax.broadcasted_iota(jnp.int32, (S5_STATE, half_states), 0)
    ecol = lax.broadcasted_iota(jnp.int32, (S5_STATE, half_states), 1)
    spread = jnp.where((ecol % S5_STATE) == erow, 1.0, 0.0).astype(BF16)
    brow_g = lax.broadcasted_iota(jnp.int32, (S5_HALF_CH, half_states), 0) // S5_GROUP
    bcol_g = lax.broadcasted_iota(jnp.int32, (S5_HALF_CH, half_states), 1) // S5_STATE
    bmask = brow_g == bcol_g
    crow_g = lax.broadcasted_iota(jnp.int32, (half_states, S5_HALF_CH), 0) // S5_STATE
    ccol_g = lax.broadcasted_iota(jnp.int32, (half_states, S5_HALF_CH), 1) // S5_GROUP
    cmask = crow_g == ccol_g
    reps = half_states // S5_STATE
    for h in range(bd_ref.shape[0]):
        rows = slice(h * S5_HALF_CH, (h + 1) * S5_HALF_CH)
        for k, bb in enumerate((bb_re, bb_im)):
            tiled = jnp.dot(bb[rows, :].astype(BF16), spread, preferred_element_type=F32)
            bd_ref[h, :, k * half_states:(k + 1) * half_states] = jnp.where(bmask, tiled, 0.0).astype(BF16)
        for k, (ct, sign) in enumerate(((cre_t, 1.0), (cim_t, -1.0))):
            tiled = jnp.concatenate([ct[:, rows]] * reps, axis=0)
            cd_ref[h, k * half_states:(k + 1) * half_states, :] = jnp.where(cmask, sign * tiled, 0.0).astype(BF16)


def _gate_prep_kernel(wr_ref, wi_ref, or_ref, oi_ref):
    hd = wr_ref.shape[1]
    per_tile = GATE_TILE // hd
    erow = lax.broadcasted_iota(jnp.int32, (hd, GATE_TILE), 0)
    ecol = lax.broadcasted_iota(jnp.int32, (hd, GATE_TILE), 1)
    for w_ref, o_ref in ((wr_ref, or_ref), (wi_ref, oi_ref)):
        for a in range(o_ref.shape[0]):
            for b in range(per_tile):
                place = jnp.where(ecol == erow + b * hd, 1.0, 0.0).astype(BF16)
                o_ref[a, b * hd:(b + 1) * hd, :] = jnp.dot(
                    w_ref[a * per_tile + b].astype(BF16), place, preferred_element_type=F32).astype(BF16)


def _stream_cast(mats, stages, sem_ref):
    chunks = []
    used = [0] * len(stages)
    for src, dst, si in mats:
        rows = stages[si].shape[1]
        assert src.shape[0] % rows == 0 and src.shape[1] <= stages[si].shape[2]
        for r0 in range(0, src.shape[0], rows):
            slot = used[si] % WEIGHT_STAGE_SLOTS
            used[si] += 1
            chunks.append((src, dst, si, r0, rows, slot))

    def copy(k):
        src, _, si, r0, rows, slot = chunks[k]
        return pltpu.make_async_copy(src.at[pl.ds(r0, rows), :],
                                     stages[si].at[slot, :, pl.ds(0, src.shape[1])],
                                     sem_ref.at[si * WEIGHT_STAGE_SLOTS + slot])

    ahead = WEIGHT_STAGE_SLOTS - 1
    for k in range(min(ahead, len(chunks))):
        copy(k).start()
    for k, (src, dst, si, r0, rows, slot) in enumerate(chunks):
        if k + ahead < len(chunks):
            copy(k + ahead).start()
        copy(k).wait()
        dst[r0:r0 + rows, :] = stages[si][slot, :, 0:src.shape[1]].astype(BF16)


def _weight_scratch(weights, stage_shapes):
    return ([pltpu.VMEM(w.shape, BF16) for w in weights]
            + [pltpu.VMEM((WEIGHT_STAGE_SLOTS,) + tuple(s), F32) for s in stage_shapes]
            + [pltpu.SemaphoreType.DMA((WEIGHT_STAGE_SLOTS * len(stage_shapes),))])


def _ffn_kernel(x_ref, xn_ref, mod_ref, g_ref, fg_ref, wup_hbm, wdn_hbm, o_ref,
                wup_ref, wdn_ref, stage_up_ref, stage_dn_ref, sem_ref, u_ref, h0_ref, *,
                steps_per_seq, final_norm):
    d = x_ref.shape[1]
    tm = xn_ref.shape[0]
    n_sub = x_ref.shape[0] // tm
    f = wdn_ref.shape[0]
    c0 = h0_ref.shape[1]
    step = pl.program_id(0)
    batch = step // steps_per_seq
    batch_next = jnp.minimum(step + 1, pl.num_programs(0) - 1) // steps_per_seq

    def mod_row(b, k):
        return mod_ref[pl.ds(b, 1), k * d:(k + 1) * d]

    def modulated(xv, b):
        return (_rms_norm(xv, g_ref[...]) * (1.0 + mod_row(b, 1)) + mod_row(b, 0)).astype(BF16)

    @pl.when(step == 0)
    def _():
        _stream_cast([(wup_hbm, wup_ref, 0), (wdn_hbm, wdn_ref, 1)], [stage_up_ref, stage_dn_ref], sem_ref)
        u0 = modulated(x_ref[0:tm, :], batch)
        u_ref[...] = u0
        h0_ref[...] = jnp.dot(u0, wup_ref[:, :c0], preferred_element_type=F32)

    gt = mod_row(batch, 2)
    for k in range(n_sub):
        rows = slice(k * tm, (k + 1) * tm)
        x = x_ref[rows, :]
        h0 = h0_ref[...]
        h_rest = jnp.dot(u_ref[...], wup_ref[:, c0:], preferred_element_type=F32)
        if k + 1 < n_sub:
            un = modulated(x_ref[(k + 1) * tm:(k + 2) * tm, :], batch)
        else:
            un = modulated(xn_ref[...], batch_next)
        u_ref[...] = un
        h0_ref[...] = jnp.dot(un, wup_ref[:, :c0], preferred_element_type=F32)
        a = jnp.concatenate([h0, h_rest[:, :f - c0]], axis=1)
        b = h_rest[:, f - c0:]
        a, b = a.astype(BF16), b.astype(BF16)
        act = (a * _sigmoid(a)) * b
        y = jnp.dot(act, wdn_ref[...], preferred_element_type=F32)
        out = x + (0.5 * gt) * y
        if final_norm:
            out = _rms_norm(out, fg_ref[...])
        o_ref[rows, :] = out


def _ffn(x2d, mod, sub, norm_g, w_up, w_down, final_g, *, seq, final_norm):
    t, d = x2d.shape
    f = w_down.shape[0]
    tm = FFN_TM
    rows_per_step = tm * FFN_TILES_PER_STEP
    assert seq % rows_per_step == 0, "a grid step must not straddle two batch elements"
    steps_per_seq = seq // rows_per_step
    n_steps = t // rows_per_step
    n_tiles = t // tm
    nxt = lambda i: jnp.minimum(FFN_TILES_PER_STEP * (i + 1), n_tiles - 1)
    kern = functools.partial(_ffn_kernel, steps_per_seq=steps_per_seq, final_norm=final_norm)
    return pl.pallas_call(
        kern,
        grid=(n_steps,),
        in_specs=[pl.BlockSpec((rows_per_step, d), lambda i: (i, 0)),
                  pl.BlockSpec((tm, d), lambda i: (nxt(i), 0)),
                  pl.BlockSpec((None,) + mod.shape[1:], lambda i: (sub, 0, 0), pipeline_mode=pl.Buffered(1)),
                  _const_spec((1, d)),
                  _const_spec((1, d)),
                  pl.BlockSpec(memory_space=pl.ANY),
                  pl.BlockSpec(memory_space=pl.ANY)],
        out_specs=pl.BlockSpec((rows_per_step, d), lambda i: (i, 0)),
        out_shape=jax.ShapeDtypeStruct((t, d), F32),
        scratch_shapes=_weight_scratch([w_up, w_down], [(d // 16, 2 * f), (f // 8, d)])
                       + [pltpu.VMEM((tm, d), BF16),
                          pltpu.VMEM((tm, FFN_HEAD_COLS), F32)],
        compiler_params=pltpu.CompilerParams(dimension_semantics=("arbitrary",),
                                             vmem_limit_bytes=VMEM_LIMIT),
        name="ffn_final" if final_norm else "ffn",
    )(x2d, x2d, mod, norm_g.reshape(1, d), final_g.reshape(1, d), w_up, w_down)


def _lru_scan(a_ref, u_ref, carry_ref):
    ts = u_ref.shape[0]
    h = carry_ref[...]
    for t in range(ts // SUBLANES):
        sl = slice(t * SUBLANES, (t + 1) * SUBLANES)
        h = a_ref[sl, :] * h + u_ref[sl, :]
        u_ref[sl, :] = h
    carry_ref[...] = h


def _s5_scan(u_ref, lam_ref, carry_ref, halves):
    ts = u_ref.shape[0]
    half_states = lam_ref.shape[1] // halves
    w = S5_LANE_CHUNK
    for h in range(halves):
        for j in range(half_states // w):
            cre = slice(2 * half_states * h + j * w, 2 * half_states * h + (j + 1) * w)
            cim = slice(2 * half_states * h + half_states + j * w, 2 * half_states * h + half_states + (j + 1) * w)
            lcs = slice(half_states * h + j * w, half_states * h + (j + 1) * w)
            lr = jnp.broadcast_to(lam_ref[0:1, lcs], (SUBLANES, w))
            li = jnp.broadcast_to(lam_ref[1:2, lcs], (SUBLANES, w))
            hr = carry_ref[:, cre]
            hi = carry_ref[:, cim]
            for t in range(ts // SUBLANES):
                sl = slice(t * SUBLANES, (t + 1) * SUBLANES)
                hr, hi = (u_ref[sl, cre] + (lr * hr - li * hi),
                          u_ref[sl, cim] + (lr * hi + li * hr))
                u_ref[sl, cre] = hr
                u_ref[sl, cim] = hi
            carry_ref[:, cre] = hr
            carry_ref[:, cim] = hi


def _mixer_kernel(mod_ref, g_ref, bin_ref, cw_ref, cb_ref,
                  wr_ref, br_ref, wi_ref, bi_ref, lamp_ref,
                  bd_ref, cd_ref, lam_ref, sd_ref, gb_ref,
                  win_hbm, pa_hbm, gw_hbm, pb_hbm, wo_hbm, x_hbm,
                  o_hbm,
                  win_ref, pa_ref, gw_ref, pb_ref, wo_ref, stage_in_ref, stage_sq_ref, sem_ref,
                  xt_ref, ot_ref, sem_x_ref, sem_o_ref,
                  xpad_ref, a_ref, u_ref, s5_ref, hl_ref, hs_ref):
    _, tt, nb, d = xt_ref.shape
    ts = nb * tt
    w5 = sd_ref.shape[1]
    halves = bd_ref.shape[0]
    hs2 = bd_ref.shape[2]
    pad = (CONV_WIDTH - 1) * nb
    step = pl.program_id(0)
    n_steps = pl.num_programs(0)
    slot = step % 2

    def x_copies(tile, sl):
        return [pltpu.make_async_copy(x_hbm.at[b, pl.ds(tile * tt, tt), :], xt_ref.at[sl, :, b, :],
                                      sem_x_ref.at[sl]) for b in range(nb)]

    def o_copies(tile, sl):
        return [pltpu.make_async_copy(ot_ref.at[sl, :, b, :], o_hbm.at[b, pl.ds(tile * tt, tt), :],
                                      sem_o_ref.at[sl]) for b in range(nb)]

    @pl.when(step == 0)
    def _():
        for cp in x_copies(0, 0):
            cp.start()
        xpad_ref[0:pad, :] = jnp.zeros((pad, d), F32)
        hl_ref[...] = jnp.zeros_like(hl_ref)
        hs_ref[...] = jnp.zeros_like(hs_ref)
        _stream_cast([(win_hbm, win_ref, 0), (pa_hbm, pa_ref, 1), (gw_hbm, gw_ref, 1), (pb_hbm, pb_ref, 1),
                      (wo_hbm, wo_ref, 1)], [stage_in_ref, stage_sq_ref], sem_ref)

    @pl.when(step + 1 < n_steps)
    def _():
        for cp in x_copies(step + 1, 1 - slot):
            cp.start()

    for cp in x_copies(step, slot):
        cp.wait()

    def per_batch(v):
        return v[None, :, :]

    @pl.when(step >= 2)
    def _():
        for cp in o_copies(step - 2, slot):
            cp.wait()

    sh = per_batch(mod_ref[:, 0:d])
    sc = per_batch(mod_ref[:, d:2 * d])
    gt = per_batch(mod_ref[:, 2 * d:3 * d])
    nl = -lamp_ref[...]
    softplus = jnp.maximum(nl, 0.0) + jnp.log1p(jnp.exp(-jnp.abs(nl)))

    def front():
        x3 = xt_ref[slot]
        u_bf = (_rms_norm(x3, g_ref[...]) * (1.0 + sc) + sh).reshape(ts, d).astype(BF16)

        def in_proj(lo, hi):
            return jnp.dot(u_bf, win_ref[:, lo:hi], preferred_element_type=F32) + bin_ref[:, lo:hi]

        xpad_ref[pad:pad + ts, :] = in_proj(0, d)
        xb = in_proj(2 * d, 2 * d + w5)
        xb_bf = xb.astype(BF16)
        for h in range(halves):
            s5_ref[:, h * hs2:(h + 1) * hs2] = jnp.dot(
                xb_bf[:, h * S5_HALF_CH:(h + 1) * S5_HALF_CH], bd_ref[h], preferred_element_type=F32)
        ga_pre = in_proj(d, 2 * d)

        xc = cb_ref[...]
        for k in range(CONV_WIDTH):
            xc = xc + xpad_ref[k * nb:k * nb + ts, :] * cw_ref[k:k + 1, :]
        xpad_ref[0:pad, :] = xpad_ref[ts:ts + pad, :]
        xc_bf = xc.astype(BF16)
        gates = []
        for w_ref, b_ref in ((wr_ref, br_ref), (wi_ref, bi_ref)):
            cols = [jnp.dot(xc_bf[:, j * GATE_TILE:(j + 1) * GATE_TILE], w_ref[j], preferred_element_type=F32)
                    for j in range(w_ref.shape[0])]
            gates.append(_sigmoid(jnp.concatenate(cols, axis=1) + b_ref[...]))
        mga_pre = in_proj(2 * d + w5, 3 * d + w5)
        mgb_pre = in_proj(3 * d + w5, 4 * d + w5)
        return x3, xb, ga_pre, xc, gates, mga_pre, mgb_pre

    def back(x3, xb, ga_pre, xc, gates, mga_pre, mgb_pre):
        r_gate, i_gate = gates
        neg_log_a = (LRU_C * softplus) * r_gate
        a_gate = jnp.exp(-neg_log_a)
        a_ref[...] = a_gate
        one_m_a2 = jnp.tanh(neg_log_a) * (1.0 + a_gate * a_gate)
        mult = jnp.where(one_m_a2 > 0.0, one_m_a2 * lax.rsqrt(one_m_a2), 0.0)
        u_ref[...] = mult * (i_gate * xc)
        _lru_scan(a_ref, u_ref, hl_ref)
        ya = u_ref[...].astype(BF16) * _gelu_tanh(ga_pre.astype(BF16))
        proj_a = jnp.dot(ya, pa_ref[...], preferred_element_type=F32)

        _s5_scan(s5_ref, lam_ref, hs_ref, halves)
        ys = [jnp.dot(s5_ref[:, h * hs2:(h + 1) * hs2].astype(BF16), cd_ref[h], preferred_element_type=F32)
              for h in range(halves)]
        yb = _gelu_tanh((jnp.concatenate(ys, axis=1) + sd_ref[...] * xb).astype(BF16))
        glu = jnp.dot(yb, gw_ref[...], preferred_element_type=F32) + gb_ref[...]
        yb = yb * _sigmoid(glu.astype(BF16))
        proj_b = jnp.dot(yb, pb_ref[...], preferred_element_type=F32)

        m = (_sigmoid(mga_pre.astype(BF16)) * proj_a.astype(BF16)
             + _sigmoid(mgb_pre.astype(BF16)) * proj_b.astype(BF16))
        y = jnp.dot(m, wo_ref[...], preferred_element_type=F32)
        ot_ref[slot] = x3 + gt * y.reshape(tt, nb, d)

    back(*front())

    for cp in o_copies(step, slot):
        cp.start()

    @pl.when(step == n_steps - 1)
    def _():
        for cp in o_copies(step - 1, 1 - slot):
            cp.wait()
        for cp in o_copies(step, slot):
            cp.wait()


def _mixer(x3d, mod, sub, norm_g, w_in, b_in, conv_w, conv_b, wr_bd, b_r, wi_bd, b_i, lam_p, proj_a,
           bd, cd, lam, s5_d, glu_w, glu_b, proj_b, w_out):
    bsz, seq, d = x3d.shape
    tt = MIX_TT
    ts = tt * bsz
    n_states2 = bd.shape[0] * bd.shape[2]
    row = lambda v: v.reshape(1, -1)
    consts = [row(norm_g), row(b_in), conv_w, row(conv_b), wr_bd, row(b_r), wi_bd, row(b_i),
              row(lam_p), bd, cd, lam, row(s5_d), row(glu_b)]
    streamed = [w_in, proj_a, glu_w, proj_b, w_out]
    assert seq // tt >= 2, "the write-back drain expects at least two sequence tiles"
    return pl.pallas_call(
        _mixer_kernel,
        grid=(seq // tt,),
        in_specs=[pl.BlockSpec((None,) + mod.shape[1:], lambda i: (sub, 0, 0), pipeline_mode=pl.Buffered(1))]
                 + [_const_spec(c.shape) for c in consts]
                 + [pl.BlockSpec(memory_space=pl.ANY)] * (len(streamed) + 1),
        out_specs=pl.BlockSpec(memory_space=pl.ANY),
        out_shape=jax.ShapeDtypeStruct((bsz, seq, d), F32),
        scratch_shapes=_weight_scratch(streamed, [(d // 32, w_in.shape[1]), (d // 8, d)])
                       + [pltpu.VMEM((2, tt, bsz, d), F32),
                          pltpu.VMEM((2, tt, bsz, d), F32),
                          pltpu.SemaphoreType.DMA((2,)),
                          pltpu.SemaphoreType.DMA((2,)),
                        pltpu.VMEM((ts + (CONV_WIDTH - 1) * bsz, d), F32),
                        pltpu.VMEM((ts, d), F32),
                        pltpu.VMEM((ts, d), F32),
                        pltpu.VMEM((ts, n_states2), F32),
                        pltpu.VMEM((bsz, d), F32),
                        pltpu.VMEM((bsz, n_states2), F32)],
        compiler_params=pltpu.CompilerParams(dimension_semantics=("arbitrary",),
                                             vmem_limit_bytes=VMEM_LIMIT),
        name="mixer",
    )(mod, *consts, *streamed, x3d)


def kernel(x, c, mod_w, mod_b, norm1_g, ffn1_w_up, ffn1_w_down, norm2_g, w_in, b_in, conv_w, conv_b, lru_w_r, lru_b_r, lru_w_i, lru_b_i, lru_lambda, proj_a, s5_a_re, s5_a_im, s5_log_dt, s5_b_re, s5_b_im, s5_c_re, s5_c_im, s5_d, glu_w, glu_b, proj_b, w_out, norm3_g, ffn2_w_up, ffn2_w_down, final_g):
    bsz, seq, d = x.shape
    assert bsz == SUBLANES, "time-major layout puts the batch on the 8 sublanes"
    depth = mod_w.shape[0]
    for l in range(depth):
        mod, lam, bd, cd, wr_bd, wi_bd = _mod_prep(
            c, mod_w[l], mod_b[l], s5_a_re[l], s5_a_im[l], s5_log_dt[l], s5_b_re[l], s5_b_im[l],
            s5_c_re[l], s5_c_im[l], lru_w_r[l], lru_w_i[l])
        last = l == depth - 1
        x2d = _ffn(x.reshape(bsz * seq, d), mod, 0, norm1_g[l], ffn1_w_up[l], ffn1_w_down[l], final_g,
                   seq=seq, final_norm=False)
        x3d = _mixer(x2d.reshape(bsz, seq, d), mod, 1, norm2_g[l], w_in[l], b_in[l], conv_w[l], conv_b[l],
                     wr_bd, lru_b_r[l], wi_bd, lru_b_i[l], lru_lambda[l], proj_a[l],
                     bd, cd, lam, s5_d[l], glu_w[l], glu_b[l], proj_b[l], w_out[l])
        x = _ffn(x3d.reshape(bsz * seq, d), mod, 2, norm3_g[l], ffn2_w_up[l], ffn2_w_down[l], final_g,
                 seq=seq, final_norm=last).reshape(bsz, seq, d)
    return x
```
